```python
import jax, jax.numpy as jnp
from jax import lax
import numpy as np

D_MODEL = 1024
BATCH = 8
SEQ = 4096
DEPTH = 4

N_MIXERS = 2
N_POOL_LAYERS = (DEPTH + 1) // 2
N_GLA_LAYERS = DEPTH // 2
POOL_WINDOWS = (2, 4, 8, 16)
POOL_GROUPS = len(POOL_WINDOWS)
POOL_GROUP_DIM = D_MODEL // POOL_GROUPS
GLA_HEADS = 4
GLA_KEY_DIM = D_MODEL // 2
GLA_VALUE_DIM = D_MODEL
GLA_HEAD_K = GLA_KEY_DIM // GLA_HEADS
GLA_HEAD_V = GLA_VALUE_DIM // GLA_HEADS
GLA_GATE_RANK = 16
GLA_GATE_TEMP = 16.0
GLA_CHUNK = 64
GLA_IN_DIM = 2 * GLA_KEY_DIM + 2 * GLA_VALUE_DIM + GLA_GATE_RANK
MOE_GROUPS = 8
MOE_EXPERTS_PER_GROUP = 8
MOE_EXPERTS = MOE_GROUPS * MOE_EXPERTS_PER_GROUP
MOE_TOP_E = 2
MOE_D_FF = 384
NORM_EPS = 1e-6

kernel_name = "hybrid_pool_gla_hiermoe_adaln"


def rmsnorm(x, g):
    xf = x.astype(jnp.float32)
    y = xf * lax.rsqrt(jnp.mean(xf * xf, axis=-1, keepdims=True) + NORM_EPS)
    return (y * g.astype(jnp.float32)).astype(x.dtype)


def pool_mixer(h, w, b, scale):
    B, S, _ = h.shape
    hf = h.astype(jnp.float32).reshape(B, S, POOL_GROUPS, POOL_GROUP_DIM)
    cs = jnp.cumsum(hf, axis=1)
    pos = jnp.arange(1, S + 1, dtype=jnp.float32)
    outs = []
    for gi, win in enumerate(POOL_WINDOWS):
        csg = cs[:, :, gi]
        lag = jnp.pad(csg, ((0, 0), (win, 0), (0, 0)))[:, :S]
        mean = (csg - lag) / jnp.minimum(pos, float(win))[None, :, None]
        outs.append(mean - hf[:, :, gi])
    d = jnp.stack(outs, axis=2).astype(h.dtype)
    y = jnp.einsum('bsgc,gcd->bsgd', d, w).reshape(B, S, D_MODEL) + b
    return y * scale


def gla_mixer(h, w_in, w_gate, b_gate, norm_g, w_out):
    B, S, _ = h.shape
    C = GLA_CHUNK
    N = S // C
    proj = h @ w_in
    q, k, v, r, z = jnp.split(
        proj, [GLA_KEY_DIM, 2 * GLA_KEY_DIM, 2 * GLA_KEY_DIM + GLA_VALUE_DIM,
               2 * GLA_KEY_DIM + 2 * GLA_VALUE_DIM], axis=-1)
    g = jax.nn.log_sigmoid((z @ w_gate + b_gate).astype(jnp.float32)) / GLA_GATE_TEMP

    def to_chunks(t, dh):
        return t.astype(jnp.float32).reshape(B, N, C, GLA_HEADS, dh).transpose(0, 3, 1, 2, 4)

    qc = to_chunks(q, GLA_HEAD_K) * (GLA_HEAD_K ** -0.5)
    kc = to_chunks(k, GLA_HEAD_K)
    vc = to_chunks(v, GLA_HEAD_V)
    gc = to_chunks(g, GLA_HEAD_K)
    bcum = jnp.cumsum(gc, axis=3)
    b_last = bcum[:, :, :, -1:, :]
    q_e = qc * jnp.exp(bcum)
    k_e = kc * jnp.exp(-bcum)
    k_dec = kc * jnp.exp(b_last - bcum)
    causal = jnp.tril(jnp.ones((C, C), dtype=bool))
    scores = jnp.where(causal, jnp.einsum('bhnid,bhnjd->bhnij', q_e, k_e), 0.0)
    o_intra = jnp.einsum('bhnij,bhnjv->bhniv', scores, vc)

    def step(state, inp):
        q_n, k_n, v_n, decay_n = inp
        o_n = jnp.einsum('bhid,bhdv->bhiv', q_n, state)
        state = decay_n[..., None] * state + jnp.einsum('bhjd,bhjv->bhdv', k_n, v_n)
        return state, o_n

    xs = (jnp.moveaxis(q_e, 2, 0), jnp.moveaxis(k_dec, 2, 0), jnp.moveaxis(vc, 2, 0),
          jnp.moveaxis(jnp.exp(b_last[:, :, :, 0, :]), 2, 0))
    s0 = jnp.zeros((B, GLA_HEADS, GLA_HEAD_K, GLA_HEAD_V), jnp.float32)
    _, o_inter = lax.scan(step, s0, xs)
    o = o_intra + jnp.moveaxis(o_inter, 0, 2)
    o = o * lax.rsqrt(jnp.mean(o * o, axis=-1, keepdims=True) + NORM_EPS) * norm_g.astype(jnp.float32)
    o = o.transpose(0, 2, 3, 1, 4).reshape(B, S, GLA_VALUE_DIM).astype(h.dtype)
    o = o * jax.nn.silu(r)
    return o @ w_out


def hier_moe(h, w_group, b_group, w_expert, b_expert, w_in, w_out):
    B, S, D = h.shape
    t = h.reshape(-1, D)
    T = t.shape[0]
    p_group = jax.nn.softmax((t @ w_group + b_group).astype(jnp.float32), axis=-1)
    p_g, g_idx = lax.top_k(p_group, 1)
    logits_e = (t @ w_expert + b_expert).astype(jnp.float32).reshape(T, MOE_GROUPS, MOE_EXPERTS_PER_GROUP)
    logits_sel = jnp.take_along_axis(logits_e, g_idx[:, :, None], axis=1)[:, 0]
    p_e = jax.nn.softmax(logits_sel, axis=-1)
    p_top, e_idx = lax.top_k(p_e, MOE_TOP_E)
    weights = p_g * p_top / jnp.sum(p_top, axis=-1, keepdims=True)
    flat_id = (g_idx * MOE_EXPERTS_PER_GROUP + e_idx).reshape(-1)
    order = jnp.argsort(flat_id)
    token_of = order // MOE_TOP_E
    xs = t[token_of]
    group_sizes = jnp.bincount(flat_id, length=MOE_EXPERTS).astype(jnp.int32)
    gu = lax.ragged_dot(xs, w_in, group_sizes)
    gate, up = jnp.split(gu, 2, axis=-1)
    ys = lax.ragged_dot(jax.nn.silu(gate) * up, w_out, group_sizes)
    ys = ys * weights.reshape(-1)[order][:, None].astype(ys.dtype)
    out = jnp.zeros_like(t).at[token_of].add(ys)
    return out.reshape(B, S, D)


def setup_inputs(seed: int = 0) -> dict:
    key = jax.random.key(seed)
    ks = jax.random.split(key, 22)
    D = D_MODEL
    f32 = jnp.float32
    nrm = lambda k, shape, s: jax.random.normal(k, shape, f32) * s
    return {
        "x": nrm(ks[0], (BATCH, SEQ, D), 1.0),
        "c": nrm(ks[1], (BATCH, D), 1.0),
        "norm_gain": 1.0 + nrm(ks[2], (DEPTH, 2, D), 0.1),
        "w_mod": nrm(ks[3], (DEPTH, D, 6 * D), 0.5 * D ** -0.5),
        "b_mod": nrm(ks[4], (DEPTH, 6 * D), 0.02),
        "pool_w": nrm(ks[5], (N_POOL_LAYERS, POOL_GROUPS, POOL_GROUP_DIM, POOL_GROUP_DIM), POOL_GROUP_DIM ** -0.5),
        "pool_b": nrm(ks[6], (N_POOL_LAYERS, D), 0.02),
        "pool_scale": 1.0 + nrm(ks[7], (N_POOL_LAYERS, D), 0.1),
        "gla_w_in": nrm(ks[8], (N_GLA_LAYERS, D, GLA_IN_DIM), D ** -0.5),
        "gla_w_gate": nrm(ks[9], (N_GLA_LAYERS, GLA_GATE_RANK, GLA_KEY_DIM), GLA_GATE_RANK ** -0.5),
        "gla_b_gate": nrm(ks[10], (N_GLA_LAYERS, GLA_KEY_DIM), 0.1),
        "gla_norm_g": 1.0 + nrm(ks[11], (N_GLA_LAYERS, GLA_HEAD_V), 0.1),
        "gla_w_out": nrm(ks[12], (N_GLA_LAYERS, GLA_VALUE_DIM, D), GLA_VALUE_DIM ** -0.5),
        "moe_w_group": nrm(ks[13], (DEPTH, D, MOE_GROUPS), D ** -0.5),
        "moe_b_group": nrm(ks[14], (DEPTH, MOE_GROUPS), 0.01),
        "moe_w_expert": nrm(ks[15], (DEPTH, D, MOE_EXPERTS), D ** -0.5),
        "moe_b_expert": nrm(ks[16], (DEPTH, MOE_EXPERTS), 0.01),
        "moe_w_in": nrm(ks[17], (DEPTH, MOE_EXPERTS, D, 2 * MOE_D_FF), D ** -0.5),
        "moe_w_out": nrm(ks[18], (DEPTH, MOE_EXPERTS, MOE_D_FF, D), MOE_D_FF ** -0.5),
        "final_norm_g": 1.0 + nrm(ks[19], (D,), 0.1),
    }


def reference(x, c, norm_gain, w_mod, b_mod, pool_w, pool_b, pool_scale,
              gla_w_in, gla_w_gate, gla_b_gate, gla_norm_g, gla_w_out,
              moe_w_group, moe_b_group, moe_w_expert, moe_b_expert, moe_w_in, moe_w_out,
              final_norm_g):
    for i in range(DEPTH):
        mod = (jax.nn.silu(c) @ w_mod[i] + b_mod[i])[:, None, :]
        shift1, scale1, gate1, shift2, scale2, gate2 = jnp.split(mod, 6, axis=-1)
        h = rmsnorm(x, norm_gain[i, 0]) * (1.0 + scale1) + shift1
        j = i // N_MIXERS
        if i % N_MIXERS == 0:
            y = pool_mixer(h, pool_w[j], pool_b[j], pool_scale[j])
        else:
            y = gla_mixer(h, gla_w_in[j], gla_w_gate[j], gla_b_gate[j], gla_norm_g[j], gla_w_out[j])
        x = x + gate1 * y
        h = rmsnorm(x, norm_gain[i, 1]) * (1.0 + scale2) + shift2
        x = x + gate2 * hier_moe(h, moe_w_group[i], moe_b_group[i], moe_w_expert[i], moe_b_expert[i],
                                 moe_w_in[i], moe_w_out[i])
    return rmsnorm(x, final_norm_g)
```

```python
import functools

import jax
import jax.numpy as jnp
from jax import lax
from jax.experimental import pallas as pl
from jax.experimental.pallas import tpu as pltpu

F32 = jnp.float32
BF16 = jnp.bfloat16

NORM_EPS = 1e-6
POOL_WINDOWS = (2, 4, 8, 16)
POOL_HALO = 16
GLA_CHUNK = 64
GLA_GATE_TEMP = 16.0
MOE_TOP_E = 2
LANES = 128
ROUTER_ROWS = 128
VMEM_LIMIT = 56 * 1024 * 1024

SEQ_TILE_POOL = 512
SEQ_TILE_GLA = 256
GMM_TILE = 256
ROW_TILE = 256


def _dot(a, b):
    return jnp.dot(a, b, preferred_element_type=F32)


def _dot_nt(a, b):
    return lax.dot_general(a, b, (((1,), (1,)), ((), ())), preferred_element_type=F32)


def _dot_tn(a, b):
    return lax.dot_general(a, b, (((0,), (0,)), ((), ())), preferred_element_type=F32)


def _split_bf16(x):
    hi = x.astype(BF16)
    lo = (x - hi.astype(F32)).astype(BF16)
    return hi, lo


def _norm_mod(x, gain, scale, shift):
    ms = jnp.mean(x * x, axis=-1, keepdims=True)
    return x * lax.rsqrt(ms + NORM_EPS) * gain * (1.0 + scale) + shift


def _mod_kernel(c_ref, w_ref, b_ref, o_ref):
    c = c_ref[...]
    sc = (c * (1.0 / (1.0 + jnp.exp(-c)))).astype(BF16)
    o_ref[0] = _dot(sc, w_ref[0].astype(BF16)) + b_ref[0]


def _modulation(c, w_mod, b_mod):
    depth, d, n = w_mod.shape
    b = c.shape[0]
    tn = n // 4
    return pl.pallas_call(
        _mod_kernel,
        grid=(depth, n // tn),
        in_specs=[
            pl.BlockSpec((b, d), lambda i, j: (0, 0)),
            pl.BlockSpec((1, d, tn), lambda i, j: (i, 0, j)),
            pl.BlockSpec((1, 1, tn), lambda i, j: (i, 0, j)),
        ],
        out_specs=pl.BlockSpec((1, b, tn), lambda i, j: (i, 0, j)),
        out_shape=jax.ShapeDtypeStruct((depth, b, n), F32),
        compiler_params=pltpu.CompilerParams(vmem_limit_bytes=VMEM_LIMIT),
    )(c, w_mod, b_mod.reshape(depth, 1, n))


def _router_epilogue(x1, mod, gain2, wr_hi_ref, wr_lo_ref, br_ref, n_groups, n_experts,
                     h2_ref, ids_ref, wts_ref, hist_ref, first_step):
    h2 = _norm_mod(x1, gain2, mod[4:5], mod[3:4])
    h2_ref[0] = h2
    hh, hl = _split_bf16(h2)
    wh = wr_hi_ref[...]
    lt = _dot_nt(wh, hh) + _dot_nt(wh, hl) + _dot_nt(wr_lo_ref[...], hh) + br_ref[...]
    ts = lt.shape[1]
    per = n_experts // n_groups
    lg = lt[0:n_groups]
    mg = jnp.max(lg, axis=0, keepdims=True)
    p_g = 1.0 / jnp.sum(jnp.exp(lg - mg), axis=0, keepdims=True)
    gi = lax.broadcasted_iota(jnp.int32, lg.shape, 0)
    g_idx = jnp.min(jnp.where(lg == mg, gi, n_groups), axis=0, keepdims=True)
    sel = jnp.zeros((per, ts), F32)
    for g in range(n_groups):
        sel = jnp.where(g_idx == g, lt[n_groups + g * per:n_groups + (g + 1) * per], sel)
    ei = lax.broadcasted_iota(jnp.int32, sel.shape, 0)
    m1 = jnp.max(sel, axis=0, keepdims=True)
    i1 = jnp.min(jnp.where(sel == m1, ei, per), axis=0, keepdims=True)
    rest = jnp.where(ei == i1, -jnp.inf, sel)
    m2 = jnp.max(rest, axis=0, keepdims=True)
    i2 = jnp.min(jnp.where(rest == m2, ei, per), axis=0, keepdims=True)
    e21 = jnp.exp(m2 - m1)
    w1 = p_g / (1.0 + e21)
    w2 = p_g * e21 / (1.0 + e21)
    id1 = g_idx * per + i1
    id2 = g_idx * per + i2
    ids_ref[...] = jnp.concatenate([id1, id2], axis=0)
    wts_ref[...] = jnp.concatenate([w1, w2], axis=0)
    xi = lax.broadcasted_iota(jnp.int32, (n_experts, ts), 0)
    cnt = (xi == id1).astype(F32) + (xi == id2).astype(F32)

    @pl.when(first_step)
    def _():
        hist_ref[...] = cnt

    @pl.when(jnp.logical_not(first_step))
    def _():
        hist_ref[...] += cnt


def _router_specs(d, ts, n_seq_tiles, n_experts):
    in_specs = [
        pl.BlockSpec((ROUTER_ROWS, d), lambda b, s: (0, 0)),
        pl.BlockSpec((ROUTER_ROWS, d), lambda b, s: (0, 0)),
        pl.BlockSpec((ROUTER_ROWS, 1), lambda b, s: (0, 0)),
    ]
    out_specs = [
        pl.BlockSpec((1, ts, d), lambda b, s: (b, s, 0)),
        pl.BlockSpec((1, ts, d), lambda b, s: (b, s, 0)),
        pl.BlockSpec((MOE_TOP_E, ts), lambda b, s: (0, b * n_seq_tiles + s)),
        pl.BlockSpec((MOE_TOP_E, ts), lambda b, s: (0, b * n_seq_tiles + s)),
        pl.BlockSpec((n_experts, ts), lambda b, s: (0, 0)),
    ]
    return in_specs, out_specs


def _router_out_shapes(b, s, d, ts, n_experts):
    return [
        jax.ShapeDtypeStruct((b, s, d), F32),
        jax.ShapeDtypeStruct((b, s, d), F32),
        jax.ShapeDtypeStruct((MOE_TOP_E, b * s), jnp.int32),
        jax.ShapeDtypeStruct((MOE_TOP_E, b * s), F32),
        jax.ShapeDtypeStruct((n_experts, ts), F32),
    ]


def _pool_kernel(x_ref, mod_ref, gain_ref, pw_ref, pb_ref, ps_ref, wr_hi_ref, wr_lo_ref, br_ref,
                 x1_ref, h2_ref, ids_ref, wts_ref, hist_ref, hbuf_ref, *, n_groups, n_experts):
    b, s = pl.program_id(0), pl.program_id(1)
    x = x_ref[0]
    ts, d = x.shape
    mod = mod_ref[0]
    h = _norm_mod(x, gain_ref[0:1], mod[1:2], mod[0:1])

    @pl.when(s == 0)
    def _():
        hbuf_ref[0:POOL_HALO] = jnp.zeros((POOL_HALO, d), F32)

    hbuf_ref[POOL_HALO:POOL_HALO + ts] = h
    pos = (s * ts + 1 + lax.broadcasted_iota(jnp.int32, (ts, 1), 0)).astype(F32)
    cg = d // len(POOL_WINDOWS)
    ys = []
    for gi, win in enumerate(POOL_WINDOWS):
        cols = slice(gi * cg, (gi + 1) * cg)
        acc = h[:, cols]
        for j in range(1, win):
            acc = acc + hbuf_ref[POOL_HALO - j:POOL_HALO - j + ts, cols]
        dgi = acc / jnp.minimum(pos, float(win)) - h[:, cols]
        ys.append(_dot(dgi.astype(BF16), pw_ref[gi]))
    y = (jnp.concatenate(ys, axis=-1) + pb_ref[...]) * ps_ref[...]
    hbuf_ref[0:POOL_HALO] = hbuf_ref[ts:ts + POOL_HALO]
    x1 = x + mod[2:3] * y
    x1_ref[0] = x1
    _router_epilogue(x1, mod, gain_ref[1:2], wr_hi_ref, wr_lo_ref, br_ref, n_groups, n_experts,
                     h2_ref, ids_ref, wts_ref, hist_ref, jnp.logical_and(b == 0, s == 0))


def _pool_layer(x, mod, gain, pw, pb, ps, wr_hi, wr_lo, br, n_groups, n_experts):
    b, s, d = x.shape
    ts = min(SEQ_TILE_POOL, s)
    nst = s // ts
    g, cg = pw.shape[0], pw.shape[1]
    r_in, r_out = _router_specs(d, ts, nst, n_experts)
    return pl.pallas_call(
        functools.partial(_pool_kernel, n_groups=n_groups, n_experts=n_experts),
        grid=(b, nst),
        in_specs=[
            pl.BlockSpec((1, ts, d), lambda bb, ss: (bb, ss, 0)),
            pl.BlockSpec((1, 6, d), lambda bb, ss: (bb, 0, 0)),
            pl.BlockSpec((2, d), lambda bb, ss: (0, 0)),
            pl.BlockSpec((g, cg, cg), lambda bb, ss: (0, 0, 0)),
            pl.BlockSpec((1, d), lambda bb, ss: (0, 0)),
            pl.BlockSpec((1, d), lambda bb, ss: (0, 0)),
        ] + r_in,
        out_specs=r_out,
        out_shape=_router_out_shapes(b, s, d, ts, n_experts),
        scratch_shapes=[pltpu.VMEM((POOL_HALO + ts, d), F32)],
        compiler_params=pltpu.CompilerParams(
            dimension_semantics=("arbitrary", "arbitrary"), vmem_limit_bytes=VMEM_LIMIT),
    )(x, mod, gain, pw.astype(BF16), pb.reshape(1, d), ps.reshape(1, d), wr_hi, wr_lo, br)


def _gla_kernel(x_ref, mod_ref, gain_ref, wp_ref, wz_ref, wg_ref, bg_ref, ng_ref, wo_ref,
                wr_hi_ref, wr_lo_ref, br_ref,
                x1_ref, h2_ref, ids_ref, wts_ref, hist_ref,
                state_ref, q_s, k_s, g_s, v_s, r_s, o_s, *, n_heads, n_groups, n_experts):
    b, s = pl.program_id(0), pl.program_id(1)
    x = x_ref[0]
    ts, d = x.shape
    dk_all = q_s.shape[1]
    dk = dk_all // n_heads
    dv = d // n_heads
    mod = mod_ref[0]
    h = _norm_mod(x, gain_ref[0:1], mod[1:2], mod[0:1]).astype(BF16)

    @pl.when(s == 0)
    def _():
        state_ref[...] = jnp.zeros(state_ref.shape, F32)

    q_s[...] = _dot(h, wp_ref[:, 0:dk_all]) * (dk ** -0.5)
    k_s[...] = _dot(h, wp_ref[:, dk_all:2 * dk_all])
    v_s[...] = _dot(h, wp_ref[:, 2 * dk_all:2 * dk_all + d]).astype(BF16)
    r = _dot(h, wp_ref[:, 2 * dk_all + d:2 * dk_all + 2 * d])
    r_s[...] = r * (1.0 / (1.0 + jnp.exp(-r)))
    z = _dot(h, wz_ref[...]).astype(BF16)
    u = _dot(z, wg_ref[...]) + bg_ref[...]
    g_s[...] = (jnp.minimum(u, 0.0) - jnp.log(1.0 + jnp.exp(-jnp.abs(u)))) * (1.0 / GLA_GATE_TEMP)

    c = GLA_CHUNK
    ri = lax.broadcasted_iota(jnp.int32, (c, c), 0)
    ci = lax.broadcasted_iota(jnp.int32, (c, c), 1)
    causal = ri >= ci
    tril = jnp.where(causal, 1.0, 0.0).astype(BF16)
    ng = ng_ref[...]

    def chunk(n, carry):
        rows = pl.ds(pl.multiple_of(n * c, c), c)
        g_hi, g_lo = _split_bf16(g_s[rows, :])
        bc = _dot(tril, g_hi) + _dot(tril, g_lo)
        bl = bc[c - 1:c, :]
        q_e = (q_s[rows, :] * jnp.exp(bc)).astype(BF16)
        kk = k_s[rows, :]
        k_e = (kk * jnp.exp(-bc)).astype(BF16)
        k_d = (kk * jnp.exp(bl - bc)).astype(BF16)
        e_l = jnp.exp(bl)
        vv = v_s[rows, :]
        for hd in range(n_heads):
            ks = slice(hd * dk, (hd + 1) * dk)
            vs = slice(hd * dv, (hd + 1) * dv)
            sc = jnp.where(causal, _dot_nt(q_e[:, ks], k_e[:, ks]), 0.0).astype(BF16)
            st = state_ref[hd]
            o = _dot(sc, vv[:, vs]) + _dot_nt(q_e[:, ks], st.astype(BF16))
            state_ref[hd] = st * e_l[:, ks] + _dot_tn(vv[:, vs], k_d[:, ks])
            o = o * lax.rsqrt(jnp.mean(o * o, axis=-1, keepdims=True) + NORM_EPS) * ng
            o_s[rows, vs] = (o * r_s[rows, vs]).astype(BF16)
        return carry

    lax.fori_loop(0, ts // c, chunk, 0)
    x1 = x + mod[2:3] * _dot(o_s[...], wo_ref[...])
    x1_ref[0] = x1
    _router_epilogue(x1, mod, gain_ref[1:2], wr_hi_ref, wr_lo_ref, br_ref, n_groups, n_experts,
                     h2_ref, ids_ref, wts_ref, hist_ref, jnp.logical_and(b == 0, s == 0))


def _gla_layer(x, mod, gain, w_in, w_gate, b_gate, norm_g, w_out, wr_hi, wr_lo, br, n_groups, n_experts):
    b, s, d = x.shape
    ts = min(SEQ_TILE_GLA, s)
    nst = s // ts
    rank, dk_all = w_gate.shape
    dv = norm_g.shape[0]
    n_heads = d // dv
    n_proj = 2 * dk_all + 2 * d
    wp = w_in[:, :n_proj].astype(BF16)
    wz = jnp.pad(w_in[:, n_proj:], ((0, 0), (0, LANES - rank))).astype(BF16)
    wg = jnp.pad(w_gate, ((0, LANES - rank), (0, 0))).astype(BF16)
    r_in, r_out = _router_specs(d, ts, nst, n_experts)
    const = lambda bb, ss: (0, 0)
    return pl.pallas_call(
        functools.partial(_gla_kernel, n_heads=n_heads, n_groups=n_groups, n_experts=n_experts),
        grid=(b, nst),
        in_specs=[
            pl.BlockSpec((1, ts, d), lambda bb, ss: (bb, ss, 0)),
            pl.BlockSpec((1, 6, d), lambda bb, ss: (bb, 0, 0)),
            pl.BlockSpec((2, d), const),
            pl.BlockSpec((d, n_proj), const),
            pl.BlockSpec((d, LANES), const),
            pl.BlockSpec((LANES, dk_all), const),
            pl.BlockSpec((1, dk_all), const),
            pl.BlockSpec((1, dv), const),
            pl.BlockSpec((d, d), const),
        ] + r_in,
        out_specs=r_out,
        out_shape=_router_out_shapes(b, s, d, ts, n_experts),
        scratch_shapes=[
            pltpu.VMEM((n_heads, dv, dk_all // n_heads), F32),
            pltpu.VMEM((ts, dk_all), F32),
            pltpu.VMEM((ts, dk_all), F32),
            pltpu.VMEM((ts, dk_all), F32),
            pltpu.VMEM((ts, d), BF16),
            pltpu.VMEM((ts, d), F32),
            pltpu.VMEM((ts, d), BF16),
        ],
        compiler_params=pltpu.CompilerParams(
            dimension_semantics=("arbitrary", "arbitrary"), vmem_limit_bytes=VMEM_LIMIT),
    )(x, mod, gain, wp, wz, wg, b_gate.reshape(1, dk_all), norm_g.reshape(1, dv), w_out.astype(BF16),
      wr_hi, wr_lo, br)


def _pos_kernel(ids_ref, off_ref, pos_ref, run_ref):
    i = pl.program_id(0)
    ids = ids_ref[...]
    tp = ids.shape[1]
    n_e = off_ref.shape[0]

    @pl.when(i == 0)
    def _():
        run_ref[...] = jnp.zeros(run_ref.shape, F32)

    ei = lax.broadcasted_iota(jnp.int32, (n_e, tp), 0)
    oh0 = ei == ids[0:1]
    oh1 = ei == ids[1:2]
    tri = jnp.where(lax.broadcasted_iota(jnp.int32, (tp, tp), 0) <= lax.broadcasted_iota(jnp.int32, (tp, tp), 1),
                    1.0, 0.0).astype(BF16)
    inc0 = _dot(jnp.where(oh0, 1.0, 0.0).astype(BF16), tri)
    inc1 = _dot(jnp.where(oh1, 1.0, 0.0).astype(BF16), tri)
    tot0 = inc0[:, tp - 1:tp]
    tot1 = inc1[:, tp - 1:tp]
    base = off_ref[...] + run_ref[...]
    p0 = jnp.sum(jnp.where(oh0, base + inc0 - 1.0, 0.0), axis=0, keepdims=True)
    p1 = jnp.sum(jnp.where(oh1, base + tot0 + inc1 - 1.0, 0.0), axis=0, keepdims=True)
    pos_ref[0] = jnp.concatenate([p0, p1], axis=0).astype(jnp.int32)
    run_ref[...] += tot0 + tot1


def _positions(ids, offsets):
    t = ids.shape[1]
    tp = min(ROW_TILE, t)
    n_e = offsets.shape[0]
    return pl.pallas_call(
        _pos_kernel,
        grid=(t // tp,),
        in_specs=[pl.BlockSpec((MOE_TOP_E, tp), lambda i: (0, i)),
                  pl.BlockSpec((n_e, 1), lambda i: (0, 0))],
        out_specs=pl.BlockSpec((1, MOE_TOP_E, tp), lambda i: (i, 0, 0)),
        out_shape=jax.ShapeDtypeStruct((t // tp, MOE_TOP_E, tp), jnp.int32),
        scratch_shapes=[pltpu.VMEM((n_e, 1), F32)],
        compiler_params=pltpu.CompilerParams(dimension_semantics=("arbitrary",)),
    )(ids, offsets)


def _row_copy(src_ref, src_row, dst_ref, dst_row, sem):
    return pltpu.make_async_copy(src_ref.at[pl.ds(src_row, 1)], dst_ref.at[pl.ds(dst_row, 1)], sem)


def _dispatch_kernel(pos_ref, h_ref, xs_ref, sem):
    i = pl.program_id(0)
    tp = pos_ref.shape[2]

    def issue(r, carry):
        for k in range(MOE_TOP_E):
            _row_copy(h_ref, i * tp + r, xs_ref, pos_ref[0, k, r], sem).start()
        return carry

    lax.fori_loop(0, tp, issue, 0)
    for k in range(MOE_TOP_E):
        pltpu.make_async_copy(h_ref.at[pl.ds(0, tp)], xs_ref.at[pl.ds(0, tp)], sem).wait()


def _dispatch(pos, h2, n_rows):
    nt, _, tp = pos.shape
    t, d = h2.shape
    return pl.pallas_call(
        _dispatch_kernel,
        grid=(nt,),
        in_specs=[pl.BlockSpec((1, MOE_TOP_E, tp), lambda i: (i, 0, 0), memory_space=pltpu.SMEM),
                  pl.BlockSpec(memory_space=pl.ANY)],
        out_specs=pl.BlockSpec(memory_space=pl.ANY),
        out_shape=jax.ShapeDtypeStruct((n_rows, d), F32),
        scratch_shapes=[pltpu.SemaphoreType.DMA(())],
        compiler_params=pltpu.CompilerParams(dimension_semantics=("arbitrary",), has_side_effects=True),
    )(pos, h2)


def _gmm_kernel(it_ref, ie_ref, lo_ref, hi_ref, n_ref, xs_ref, wi_ref, wo_ref, ys_ref, wi_s, wo_s):
    w = pl.program_id(0)
    f = wo_ref.shape[1]
    tm = xs_ref.shape[0]

    @pl.when(w < n_ref[0])
    def _():
        prev = jnp.maximum(w - 1, 0)

        @pl.when(jnp.logical_or(w == 0, ie_ref[w] != ie_ref[prev]))
        def _():
            wi_s[...] = wi_ref[0].astype(BF16)
            wo_s[...] = wo_ref[0].astype(BF16)

        gu = _dot(xs_ref[...].astype(BF16), wi_s[...])
        gate, up = gu[:, :f], gu[:, f:]
        a = gate * (1.0 / (1.0 + jnp.exp(-gate))) * up
        y = _dot(a.astype(BF16), wo_s[...])
        row = lax.broadcasted_iota(jnp.int32, (tm, 1), 0)
        mine = jnp.logical_and(row >= lo_ref[w], row < hi_ref[w])
        first_visit = jnp.logical_or(w == 0, it_ref[w] != it_ref[prev])

        @pl.when(first_visit)
        def _():
            ys_ref[...] = jnp.where(mine, y, 0.0)

        @pl.when(jnp.logical_not(first_visit))
        def _():
            ys_ref[...] = jnp.where(mine, y, ys_ref[...])


def _grouped_mlp(plan, xs, w_in, w_out):
    item_tile, item_expert, item_lo, item_hi, n_items = plan
    p, d = xs.shape
    n_e, _, f2 = w_in.shape
    f = f2 // 2
    tm = GMM_TILE
    return pl.pallas_call(
        _gmm_kernel,
        grid_spec=pltpu.PrefetchScalarGridSpec(
            num_scalar_prefetch=5,
            grid=(item_tile.shape[0],),
            in_specs=[
                pl.BlockSpec((tm, d), lambda w, it, ie, lo, hi, n: (it[w], 0)),
                pl.BlockSpec((1, d, f2), lambda w, it, ie, lo, hi, n: (ie[w], 0, 0)),
                pl.BlockSpec((1, f, d), lambda w, it, ie, lo, hi, n: (ie[w], 0, 0)),
            ],
            out_specs=pl.BlockSpec((tm, d), lambda w, it, ie, lo, hi, n: (it[w], 0)),
            scratch_shapes=[pltpu.VMEM((d, f2), BF16), pltpu.VMEM((f, d), BF16)],
        ),
        out_shape=jax.ShapeDtypeStruct((p, d), F32),
        compiler_params=pltpu.CompilerParams(
            dimension_semantics=("arbitrary",), vmem_limit_bytes=VMEM_LIMIT),
    )(item_tile, item_expert, item_lo, item_hi, n_items, xs, w_in, w_out)


def _combine_kernel(pos_ref, posn_ref, x1_ref, wt_ref, mod_ref, fg_ref, ys_ref, o_ref, buf, sems, *, final):
    i = pl.program_id(0)
    n = pl.num_programs(0)
    tp = pos_ref.shape[2]

    def issue(idx_ref, slot):
        def body(r, carry):
            for k in range(MOE_TOP_E):
                _row_copy(ys_ref, idx_ref[0, k, r], buf.at[slot, k], r, sems.at[slot]).start()
            return carry
        lax.fori_loop(0, tp, body, 0)

    slot = lax.rem(i, 2)

    @pl.when(i == 0)
    def _():
        issue(pos_ref, 0)

    @pl.when(i + 1 < n)
    def _():
        issue(posn_ref, 1 - slot)

    for k in range(MOE_TOP_E):
        pltpu.make_async_copy(ys_ref.at[pl.ds(0, tp)], buf.at[slot, k], sems.at[slot]).wait()
    wt = wt_ref[...]
    y = wt[:, 0:1] * buf[slot, 0] + wt[:, 1:2] * buf[slot, 1]
    x = x1_ref[...] + mod_ref[0, 5:6] * y
    if final:
        x = x * lax.rsqrt(jnp.mean(x * x, axis=-1, keepdims=True) + NORM_EPS) * fg_ref[...]
    o_ref[...] = x


def _combine(pos, x1, wts_t, mod, final_g, ys, seq_len, final):
    nt, _, tp = pos.shape
    t, d = x1.shape
    per_seq = seq_len // tp
    return pl.pallas_call(
        functools.partial(_combine_kernel, final=final),
        grid=(nt,),
        in_specs=[
            pl.BlockSpec((1, MOE_TOP_E, tp), lambda i: (i, 0, 0), memory_space=pltpu.SMEM),
            pl.BlockSpec((1, MOE_TOP_E, tp), lambda i: (jnp.minimum(i + 1, nt - 1), 0, 0),
                         memory_space=pltpu.SMEM),
            pl.BlockSpec((tp, d), lambda i: (i, 0)),
            pl.BlockSpec((tp, MOE_TOP_E), lambda i: (i, 0)),
            pl.BlockSpec((1, 6, d), lambda i: (i // per_seq, 0, 0)),
            pl.BlockSpec((1, d), lambda i: (0, 0)),
            pl.BlockSpec(memory_space=pl.ANY),
        ],
        out_specs=pl.BlockSpec((tp, d), lambda i: (i, 0)),
        out_shape=jax.ShapeDtypeStruct((t, d), F32),
        scratch_shapes=[pltpu.VMEM((2, MOE_TOP_E, tp, d), F32), pltpu.SemaphoreType.DMA((2,))],
        compiler_params=pltpu.CompilerParams(
            dimension_semantics=("arbitrary",), vmem_limit_bytes=VMEM_LIMIT),
    )(pos, pos, x1, wts_t, mod, final_g, ys)


def _router_weights(w_group, b_group, w_expert, b_expert):
    d, g = w_group.shape
    e = w_expert.shape[1]
    w = jnp.concatenate([w_group, w_expert], axis=1).T
    w = jnp.pad(w, ((0, ROUTER_ROWS - g - e), (0, 0)))
    hi, lo = _split_bf16(w)
    bias = jnp.pad(jnp.concatenate([b_group, b_expert]), (0, ROUTER_ROWS - g - e)).reshape(ROUTER_ROWS, 1)
    return hi, lo, bias


def _tile_plan(hist, n_rows):
    n_e = hist.shape[0]
    tm = GMM_TILE
    counts = jnp.sum(hist, axis=1).astype(jnp.int32)
    ends = jnp.cumsum(counts)
    starts = ends - counts
    first_tile = starts // tm
    items_per = jnp.where(counts > 0, (ends - 1) // tm - first_tile + 1, 0)
    item_ends = jnp.cumsum(items_per)
    n_items = item_ends[-1:]
    max_items = n_rows // tm + n_e - 1
    w = jnp.minimum(jnp.arange(max_items, dtype=jnp.int32), n_items[0] - 1)
    item_expert = jnp.sum(w[:, None] >= item_ends[None, :], axis=1).astype(jnp.int32)
    item_tile = first_tile[item_expert] + w - (item_ends - items_per)[item_expert]
    item_lo = jnp.maximum(starts[item_expert] - item_tile * tm, 0)
    item_hi = jnp.minimum(ends[item_expert] - item_tile * tm, tm)
    offsets = starts.astype(F32).reshape(n_e, 1)
    return offsets, (item_tile.astype(jnp.int32), item_expert, item_lo.astype(jnp.int32),
                     item_hi.astype(jnp.int32), n_items.astype(jnp.int32))


def kernel(x, c, norm_gain, w_mod, b_mod, pool_w, pool_b, pool_scale, gla_w_in, gla_w_gate, gla_b_gate, gla_norm_g, gla_w_out, moe_w_group, moe_b_group, moe_w_expert, moe_b_expert, moe_w_in, moe_w_out, final_norm_g):
    b, s, d = x.shape
    depth = w_mod.shape[0]
    t = b * s
    n_groups = moe_w_group.shape[-1]
    n_experts = moe_w_expert.shape[-1]
    n_rows = MOE_TOP_E * t
    mod_all = _modulation(c, w_mod, b_mod).reshape(depth, b, 6, d)
    fg = final_norm_g.reshape(1, d)
    for i in range(depth):
        mod = mod_all[i]
        wr_hi, wr_lo, br = _router_weights(moe_w_group[i], moe_b_group[i], moe_w_expert[i], moe_b_expert[i])
        j = i // 2
        if i % 2 == 0:
            x1, h2, ids, wts, hist = _pool_layer(x, mod, norm_gain[i], pool_w[j], pool_b[j], pool_scale[j],
                                                 wr_hi, wr_lo, br, n_groups, n_experts)
        else:
            x1, h2, ids, wts, hist = _gla_layer(x, mod, norm_gain[i], gla_w_in[j], gla_w_gate[j], gla_b_gate[j],
                                                gla_norm_g[j], gla_w_out[j], wr_hi, wr_lo, br, n_groups, n_experts)
        offsets, plan = _tile_plan(hist, n_rows)
        pos = _positions(ids, offsets)
        xs = _dispatch(pos, h2.reshape(t, d), n_rows)
        ys = _grouped_mlp(plan, xs, moe_w_in[i], moe_w_out[i])
        x = _combine(pos, x1.reshape(t, d), wts.T, mod, fg, ys, s, final=(i == depth - 1)).reshape(b, s, d)
    return x
```

```python
import functools

import jax
import jax.numpy as jnp
from jax import lax
from jax.experimental import pallas as pl
from jax.experimental.pallas import tpu as pltpu

F32 = jnp.float32
BF16 = jnp.bfloat16

NORM_EPS = 1e-6
POOL_WINDOWS = (2, 4, 8, 16)
POOL_HALO = 16
GLA_CHUNK = 64
GLA_GATE_TEMP = 16.0
MOE_TOP_E = 2
LANES = 128
ROUTER_ROWS = 128
VMEM_LIMIT = 56 * 1024 * 1024

SEQ_TILE_POOL = 512
SEQ_TILE_GLA = 256
GMM_TILE = 256
ROW_TILE = 512


def _dot(a, b):
    return jnp.dot(a, b, preferred_element_type=F32)


def _dot_nt(a, b):
    return lax.dot_general(a, b, (((1,), (1,)), ((), ())), preferred_element_type=F32)


def _dot_tn(a, b):
    return lax.dot_general(a, b, (((0,), (0,)), ((), ())), preferred_element_type=F32)


def _split_bf16(x):
    hi = x.astype(BF16)
    lo = (x - hi.astype(F32)).astype(BF16)
    return hi, lo


def _norm_mod(x, gain, scale, shift):
    ms = jnp.mean(x * x, axis=-1, keepdims=True)
    return x * lax.rsqrt(ms + NORM_EPS) * gain * (1.0 + scale) + shift


def _mod_kernel(c_ref, w_ref, b_ref, o_ref):
    c = c_ref[...]
    sc = (c * (1.0 / (1.0 + jnp.exp(-c)))).astype(BF16)
    o_ref[0] = _dot(sc, w_ref[0].astype(BF16)) + b_ref[0]


def _modulation(c, w_mod, b_mod):
    depth, d, n = w_mod.shape
    b = c.shape[0]
    tn = n // 4
    return pl.pallas_call(
        _mod_kernel,
        grid=(depth, n // tn),
        in_specs=[
            pl.BlockSpec((b, d), lambda i, j: (0, 0)),
            pl.BlockSpec((1, d, tn), lambda i, j: (i, 0, j)),
            pl.BlockSpec((1, 1, tn), lambda i, j: (i, 0, j)),
        ],
        out_specs=pl.BlockSpec((1, b, tn), lambda i, j: (i, 0, j)),
        out_shape=jax.ShapeDtypeStruct((depth, b, n), F32),
        compiler_params=pltpu.CompilerParams(vmem_limit_bytes=VMEM_LIMIT),
    )(c, w_mod, b_mod.reshape(depth, 1, n))


def _router_epilogue(x1, mod, gain2, wr_hi_ref, wr_lo_ref, br_ref, n_groups, n_experts,
                     h2_ref, ids_ref, wts_ref, hist_ref, first_step):
    h2 = _norm_mod(x1, gain2, mod[4:5], mod[3:4])
    h2_ref[0] = h2
    hh, hl = _split_bf16(h2)
    wh = wr_hi_ref[...]
    lt = _dot_nt(wh, hh) + _dot_nt(wh, hl) + _dot_nt(wr_lo_ref[...], hh) + br_ref[...]
    ts = lt.shape[1]
    per = n_experts // n_groups
    lg = lt[0:n_groups]
    mg = jnp.max(lg, axis=0, keepdims=True)
    p_g = 1.0 / jnp.sum(jnp.exp(lg - mg), axis=0, keepdims=True)
    gi = lax.broadcasted_iota(jnp.int32, lg.shape, 0)
    g_idx = jnp.min(jnp.where(lg == mg, gi, n_groups), axis=0, keepdims=True)
    sel = jnp.zeros((per, ts), F32)
    for g in range(n_groups):
        sel = jnp.where(g_idx == g, lt[n_groups + g * per:n_groups + (g + 1) * per], sel)
    ei = lax.broadcasted_iota(jnp.int32, sel.shape, 0)
    m1 = jnp.max(sel, axis=0, keepdims=True)
    i1 = jnp.min(jnp.where(sel == m1, ei, per), axis=0, keepdims=True)
    rest = jnp.where(ei == i1, -jnp.inf, sel)
    m2 = jnp.max(rest, axis=0, keepdims=True)
    i2 = jnp.min(jnp.where(rest == m2, ei, per), axis=0, keepdims=True)
    e21 = jnp.exp(m2 - m1)
    w1 = p_g / (1.0 + e21)
    w2 = p_g * e21 / (1.0 + e21)
    id1 = g_idx * per + i1
    id2 = g_idx * per + i2
    ids_ref[...] = jnp.concatenate([id1, id2], axis=0)
    wts_ref[...] = jnp.concatenate([w1, w2], axis=0)
    xi = lax.broadcasted_iota(jnp.int32, (n_experts, ts), 0)
    cnt = (xi == id1).astype(F32) + (xi == id2).astype(F32)

    @pl.when(first_step)
    def _():
        hist_ref[...] = cnt

    @pl.when(jnp.logical_not(first_step))
    def _():
        hist_ref[...] += cnt


def _router_specs(d, ts, n_seq_tiles, n_experts):
    in_specs = [
        pl.BlockSpec((ROUTER_ROWS, d), lambda b, s: (0, 0)),
        pl.BlockSpec((ROUTER_ROWS, d), lambda b, s: (0, 0)),
        pl.BlockSpec((ROUTER_ROWS, 1), lambda b, s: (0, 0)),
    ]
    out_specs = [
        pl.BlockSpec((1, ts, d), lambda b, s: (b, s, 0)),
        pl.BlockSpec((1, ts, d), lambda b, s: (b, s, 0)),
        pl.BlockSpec((MOE_TOP_E, ts), lambda b, s: (0, b * n_seq_tiles + s)),
        pl.BlockSpec((MOE_TOP_E, ts), lambda b, s: (0, b * n_seq_tiles + s)),
        pl.BlockSpec((n_experts, ts), lambda b, s: (0, 0)),
    ]
    return in_specs, out_specs


def _router_out_shapes(b, s, d, ts, n_experts):
    return [
        jax.ShapeDtypeStruct((b, s, d), F32),
        jax.ShapeDtypeStruct((b, s, d), F32),
        jax.ShapeDtypeStruct((MOE_TOP_E, b * s), jnp.int32),
        jax.ShapeDtypeStruct((MOE_TOP_E, b * s), F32),
        jax.ShapeDtypeStruct((n_experts, ts), F32),
    ]


def _pool_kernel(x_ref, mod_ref, gain_ref, pw_ref, pb_ref, ps_ref, wr_hi_ref, wr_lo_ref, br_ref,
                 x1_ref, h2_ref, ids_ref, wts_ref, hist_ref, hbuf_ref, *, n_groups, n_experts):
    b, s = pl.program_id(0), pl.program_id(1)
    x = x_ref[0]
    ts, d = x.shape
    mod = mod_ref[0]
    h = _norm_mod(x, gain_ref[0:1], mod[1:2], mod[0:1])

    @pl.when(s == 0)
    def _():
        hbuf_ref[0:POOL_HALO] = jnp.zeros((POOL_HALO, d), F32)

    hbuf_ref[POOL_HALO:POOL_HALO + ts] = h
    pos = (s * ts + 1 + lax.broadcasted_iota(jnp.int32, (ts, 1), 0)).astype(F32)
    cg = d // len(POOL_WINDOWS)
    ys = []
    for gi, win in enumerate(POOL_WINDOWS):
        cols = slice(gi * cg, (gi + 1) * cg)
        acc = h[:, cols]
        for j in range(1, win):
            acc = acc + hbuf_ref[POOL_HALO - j:POOL_HALO - j + ts, cols]
        dgi = acc / jnp.minimum(pos, float(win)) - h[:, cols]
        ys.append(_dot(dgi.astype(BF16), pw_ref[gi]))
    y = (jnp.concatenate(ys, axis=-1) + pb_ref[...]) * ps_ref[...]
    hbuf_ref[0:POOL_HALO] = hbuf_ref[ts:ts + POOL_HALO]
    x1 = x + mod[2:3] * y
    x1_ref[0] = x1
    _router_epilogue(x1, mod, gain_ref[1:2], wr_hi_ref, wr_lo_ref, br_ref, n_groups, n_experts,
                     h2_ref, ids_ref, wts_ref, hist_ref, jnp.logical_and(b == 0, s == 0))


def _pool_layer(x, mod, gain, pw, pb, ps, wr_hi, wr_lo, br, n_groups, n_experts):
    b, s, d = x.shape
    ts = min(SEQ_TILE_POOL, s)
    nst = s // ts
    g, cg = pw.shape[0], pw.shape[1]
    r_in, r_out = _router_specs(d, ts, nst, n_experts)
    return pl.pallas_call(
        functools.partial(_pool_kernel, n_groups=n_groups, n_experts=n_experts),
        grid=(b, nst),
        in_specs=[
            pl.BlockSpec((1, ts, d), lambda bb, ss: (bb, ss, 0)),
            pl.BlockSpec((1, 6, d), lambda bb, ss: (bb, 0, 0)),
            pl.BlockSpec((2, d), lambda bb, ss: (0, 0)),
            pl.BlockSpec((g, cg, cg), lambda bb, ss: (0, 0, 0)),
            pl.BlockSpec((1, d), lambda bb, ss: (0, 0)),
            pl.BlockSpec((1, d), lambda bb, ss: (0, 0)),
        ] + r_in,
        out_specs=r_out,
        out_shape=_router_out_shapes(b, s, d, ts, n_experts),
        scratch_shapes=[pltpu.VMEM((POOL_HALO + ts, d), F32)],
        compiler_params=pltpu.CompilerParams(
            dimension_semantics=("arbitrary", "arbitrary"), vmem_limit_bytes=VMEM_LIMIT),
    )(x, mod, gain, pw.astype(BF16), pb.reshape(1, d), ps.reshape(1, d), wr_hi, wr_lo, br)


def _gla_kernel(x_ref, mod_ref, gain_ref, wp_ref, wz_ref, wg_ref, bg_ref, ng_ref, wo_ref,
                wr_hi_ref, wr_lo_ref, br_ref,
                x1_ref, h2_ref, ids_ref, wts_ref, hist_ref,
                state_ref, q_s, k_s, g_s, v_s, r_s, o_s, *, n_heads, n_groups, n_experts):
    b, s = pl.program_id(0), pl.program_id(1)
    x = x_ref[0]
    ts, d = x.shape
    dk_all = q_s.shape[1]
    dk = dk_all // n_heads
    dv = d // n_heads
    mod = mod_ref[0]
    h = _norm_mod(x, gain_ref[0:1], mod[1:2], mod[0:1]).astype(BF16)

    @pl.when(s == 0)
    def _():
        state_ref[...] = jnp.zeros(state_ref.shape, F32)

    q_s[...] = _dot(h, wp_ref[:, 0:dk_all]) * (dk ** -0.5)
    k_s[...] = _dot(h, wp_ref[:, dk_all:2 * dk_all])
    v_s[...] = _dot(h, wp_ref[:, 2 * dk_all:2 * dk_all + d]).astype(BF16)
    r = _dot(h, wp_ref[:, 2 * dk_all + d:2 * dk_all + 2 * d])
    r_s[...] = r * (1.0 / (1.0 + jnp.exp(-r)))
    z = _dot(h, wz_ref[...]).astype(BF16)
    u = _dot(z, wg_ref[...]) + bg_ref[...]
    g_s[...] = (jnp.minimum(u, 0.0) - jnp.log(1.0 + jnp.exp(-jnp.abs(u)))) * (1.0 / GLA_GATE_TEMP)

    c = GLA_CHUNK
    ri = lax.broadcasted_iota(jnp.int32, (c, c), 0)
    ci = lax.broadcasted_iota(jnp.int32, (c, c), 1)
    causal = ri >= ci
    tril = jnp.where(causal, 1.0, 0.0).astype(BF16)
    ng = ng_ref[...]

    def chunk(n, carry):
        rows = pl.ds(pl.multiple_of(n * c, c), c)
        g_hi, g_lo = _split_bf16(g_s[rows, :])
        bc = _dot(tril, g_hi) + _dot(tril, g_lo)
        bl = bc[c - 1:c, :]
        q_e = (q_s[rows, :] * jnp.exp(bc)).astype(BF16)
        kk = k_s[rows, :]
        k_e = (kk * jnp.exp(-bc)).astype(BF16)
        k_d = (kk * jnp.exp(bl - bc)).astype(BF16)
        e_l = jnp.exp(bl)
        vv = v_s[rows, :]
        for hd in range(n_heads):
            ks = slice(hd * dk, (hd + 1) * dk)
            vs = slice(hd * dv, (hd + 1) * dv)
            sc = jnp.where(causal, _dot_nt(q_e[:, ks], k_e[:, ks]), 0.0).astype(BF16)
            st = state_ref[hd]
            o = _dot(sc, vv[:, vs]) + _dot_nt(q_e[:, ks], st.astype(BF16))
            state_ref[hd] = st * e_l[:, ks] + _dot_tn(vv[:, vs], k_d[:, ks])
            o = o * lax.rsqrt(jnp.mean(o * o, axis=-1, keepdims=True) + NORM_EPS) * ng
            o_s[rows, vs] = (o * r_s[rows, vs]).astype(BF16)
        return carry

    lax.fori_loop(0, ts // c, chunk, 0)
    x1 = x + mod[2:3] * _dot(o_s[...], wo_ref[...])
    x1_ref[0] = x1
    _router_epilogue(x1, mod, gain_ref[1:2], wr_hi_ref, wr_lo_ref, br_ref, n_groups, n_experts,
                     h2_ref, ids_ref, wts_ref, hist_ref, jnp.logical_and(b == 0, s == 0))


def _gla_layer(x, mod, gain, w_in, w_gate, b_gate, norm_g, w_out, wr_hi, wr_lo, br, n_groups, n_experts):
    b, s, d = x.shape
    ts = min(SEQ_TILE_GLA, s)
    nst = s // ts
    rank, dk_all = w_gate.shape
    dv = norm_g.shape[0]
    n_heads = d // dv
    n_proj = 2 * dk_all + 2 * d
    wp = w_in[:, :n_proj].astype(BF16)
    wz = jnp.pad(w_in[:, n_proj:], ((0, 0), (0, LANES - rank))).astype(BF16)
    wg = jnp.pad(w_gate, ((0, LANES - rank), (0, 0))).astype(BF16)
    r_in, r_out = _router_specs(d, ts, nst, n_experts)
    const = lambda bb, ss: (0, 0)
    return pl.pallas_call(
        functools.partial(_gla_kernel, n_heads=n_heads, n_groups=n_groups, n_experts=n_experts),
        grid=(b, nst),
        in_specs=[
            pl.BlockSpec((1, ts, d), lambda bb, ss: (bb, ss, 0)),
            pl.BlockSpec((1, 6, d), lambda bb, ss: (bb, 0, 0)),
            pl.BlockSpec((2, d), const),
            pl.BlockSpec((d, n_proj), const),
            pl.BlockSpec((d, LANES), const),
            pl.BlockSpec((LANES, dk_all), const),
            pl.BlockSpec((1, dk_all), const),
            pl.BlockSpec((1, dv), const),
            pl.BlockSpec((d, d), const),
        ] + r_in,
        out_specs=r_out,
        out_shape=_router_out_shapes(b, s, d, ts, n_experts),
        scratch_shapes=[
            pltpu.VMEM((n_heads, dv, dk_all // n_heads), F32),
            pltpu.VMEM((ts, dk_all), F32),
            pltpu.VMEM((ts, dk_all), F32),
            pltpu.VMEM((ts, dk_all), F32),
            pltpu.VMEM((ts, d), BF16),
            pltpu.VMEM((ts, d), F32),
            pltpu.VMEM((ts, d), BF16),
        ],
        compiler_params=pltpu.CompilerParams(
            dimension_semantics=("arbitrary", "arbitrary"), vmem_limit_bytes=VMEM_LIMIT),
    )(x, mod, gain, wp, wz, wg, b_gate.reshape(1, dk_all), norm_g.reshape(1, dv), w_out.astype(BF16),
      wr_hi, wr_lo, br)


def _pos_kernel(ids_ref, off_ref, pos_ref, run_ref):
    i = pl.program_id(0)
    ids = ids_ref[...]
    tp = ids.shape[1]
    n_e = off_ref.shape[0]

    @pl.when(i == 0)
    def _():
        run_ref[...] = jnp.zeros(run_ref.shape, F32)

    ei = lax.broadcasted_iota(jnp.int32, (n_e, tp), 0)
    oh0 = ei == ids[0:1]
    oh1 = ei == ids[1:2]
    tri = jnp.where(lax.broadcasted_iota(jnp.int32, (tp, tp), 0) <= lax.broadcasted_iota(jnp.int32, (tp, tp), 1),
                    1.0, 0.0).astype(BF16)
    inc0 = _dot(jnp.where(oh0, 1.0, 0.0).astype(BF16), tri)
    inc1 = _dot(jnp.where(oh1, 1.0, 0.0).astype(BF16), tri)
    tot0 = inc0[:, tp - 1:tp]
    tot1 = inc1[:, tp - 1:tp]
    base = off_ref[...] + run_ref[...]
    p0 = jnp.sum(jnp.where(oh0, base + inc0 - 1.0, 0.0), axis=0, keepdims=True)
    p1 = jnp.sum(jnp.where(oh1, base + tot0 + inc1 - 1.0, 0.0), axis=0, keepdims=True)
    pos_ref[0] = jnp.concatenate([p0, p1], axis=0).astype(jnp.int32)
    run_ref[...] += tot0 + tot1


def _positions(ids, offsets):
    t = ids.shape[1]
    tp = min(ROW_TILE, t)
    n_e = offsets.shape[0]
    return pl.pallas_call(
        _pos_kernel,
        grid=(t // tp,),
        in_specs=[pl.BlockSpec((MOE_TOP_E, tp), lambda i: (0, i)),
                  pl.BlockSpec((n_e, 1), lambda i: (0, 0))],
        out_specs=pl.BlockSpec((1, MOE_TOP_E, tp), lambda i: (i, 0, 0)),
        out_shape=jax.ShapeDtypeStruct((t // tp, MOE_TOP_E, tp), jnp.int32),
        scratch_shapes=[pltpu.VMEM((n_e, 1), F32)],
        compiler_params=pltpu.CompilerParams(dimension_semantics=("arbitrary",)),
    )(ids, offsets)


def _row_copy(src_ref, src_row, dst_ref, dst_row, sem):
    return pltpu.make_async_copy(src_ref.at[pl.ds(src_row, 1)], dst_ref.at[pl.ds(dst_row, 1)], sem)


def _dispatch_kernel(pos_ref, h_ref, xs_ref, sem):
    tp = pos_ref.shape[2]

    def issue(r, carry):
        for k in range(MOE_TOP_E):
            _row_copy(h_ref, r, xs_ref, pos_ref[0, k, r], sem).start()
        return carry

    lax.fori_loop(0, tp, issue, 0, unroll=8)
    for k in range(MOE_TOP_E):
        pltpu.make_async_copy(h_ref, xs_ref.at[pl.ds(0, tp)], sem).wait()


def _dispatch(pos, h2, n_rows):
    nt, _, tp = pos.shape
    t, d = h2.shape
    return pl.pallas_call(
        _dispatch_kernel,
        grid=(nt,),
        in_specs=[pl.BlockSpec((1, MOE_TOP_E, tp), lambda i: (i, 0, 0), memory_space=pltpu.SMEM),
                  pl.BlockSpec((tp, d), lambda i: (i, 0))],
        out_specs=pl.BlockSpec(memory_space=pl.ANY),
        out_shape=jax.ShapeDtypeStruct((n_rows, d), F32),
        scratch_shapes=[pltpu.SemaphoreType.DMA(())],
        compiler_params=pltpu.CompilerParams(dimension_semantics=("arbitrary",), has_side_effects=True),
    )(pos, h2)


def _gmm_kernel(it_ref, ie_ref, lo_ref, hi_ref, n_ref, xs_ref, wi_ref, wo_ref, ys_ref, wi_s, wo_s):
    w = pl.program_id(0)
    f = wo_ref.shape[2]
    tm = xs_ref.shape[0]

    @pl.when(w < n_ref[0])
    def _():
        prev = jnp.maximum(w - 1, 0)

        @pl.when(jnp.logical_or(w == 0, ie_ref[w] != ie_ref[prev]))
        def _():
            wi_s[...] = wi_ref[0, 0].astype(BF16)
            wo_s[...] = wo_ref[0, 0].astype(BF16)

        gu = _dot(xs_ref[...].astype(BF16), wi_s[...])
        gate, up = gu[:, :f], gu[:, f:]
        a = gate * (1.0 / (1.0 + jnp.exp(-gate))) * up
        y = _dot(a.astype(BF16), wo_s[...])
        row = lax.broadcasted_iota(jnp.int32, (tm, 1), 0)
        mine = jnp.logical_and(row >= lo_ref[w], row < hi_ref[w])
        first_visit = jnp.logical_or(w == 0, it_ref[w] != it_ref[prev])

        @pl.when(first_visit)
        def _():
            ys_ref[...] = jnp.where(mine, y, 0.0)

        @pl.when(jnp.logical_not(first_visit))
        def _():
            ys_ref[...] = jnp.where(mine, y, ys_ref[...])


def _grouped_mlp(plan, xs, w_in, w_out, layer):
    item_tile, item_expert, item_lo, item_hi, n_items = plan
    p, d = xs.shape
    f2 = w_in.shape[-1]
    f = f2 // 2
    tm = GMM_TILE
    return pl.pallas_call(
        _gmm_kernel,
        grid_spec=pltpu.PrefetchScalarGridSpec(
            num_scalar_prefetch=5,
            grid=(item_tile.shape[0],),
            in_specs=[
                pl.BlockSpec((tm, d), lambda w, it, ie, lo, hi, n: (it[w], 0)),
                pl.BlockSpec((1, 1, d, f2), lambda w, it, ie, lo, hi, n: (layer, ie[w], 0, 0)),
                pl.BlockSpec((1, 1, f, d), lambda w, it, ie, lo, hi, n: (layer, ie[w], 0, 0)),
            ],
            out_specs=pl.BlockSpec((tm, d), lambda w, it, ie, lo, hi, n: (it[w], 0)),
            scratch_shapes=[pltpu.VMEM((d, f2), BF16), pltpu.VMEM((f, d), BF16)],
        ),
        out_shape=jax.ShapeDtypeStruct((p, d), F32),
        compiler_params=pltpu.CompilerParams(
            dimension_semantics=("arbitrary",), vmem_limit_bytes=VMEM_LIMIT),
    )(item_tile, item_expert, item_lo, item_hi, n_items, xs, w_in, w_out)


def _combine_kernel(pos_ref, posn_ref, x1_ref, wt_ref, mod_ref, fg_ref, ys_ref, o_ref, buf, sems, *, final):
    i = pl.program_id(0)
    n = pl.num_programs(0)
    tp = pos_ref.shape[2]

    def issue(idx_ref, slot):
        def body(r, carry):
            for k in range(MOE_TOP_E):
                _row_copy(ys_ref, idx_ref[0, k, r], buf.at[slot, k], r, sems.at[slot]).start()
            return carry
        lax.fori_loop(0, tp, body, 0)

    slot = lax.rem(i, 2)

    @pl.when(i == 0)
    def _():
        issue(pos_ref, 0)

    @pl.when(i + 1 < n)
    def _():
        issue(posn_ref, 1 - slot)

    for k in range(MOE_TOP_E):
        pltpu.make_async_copy(ys_ref.at[pl.ds(0, tp)], buf.at[slot, k], sems.at[slot]).wait()
    wt = wt_ref[...]
    y = wt[:, 0:1] * buf[slot, 0] + wt[:, 1:2] * buf[slot, 1]
    x = x1_ref[...] + mod_ref[0, 5:6] * y
    if final:
        x = x * lax.rsqrt(jnp.mean(x * x, axis=-1, keepdims=True) + NORM_EPS) * fg_ref[...]
    o_ref[...] = x


def _combine(pos, x1, wts_t, mod, final_g, ys, seq_len, final):
    nt, _, tp = pos.shape
    t, d = x1.shape
    per_seq = seq_len // tp
    return pl.pallas_call(
        functools.partial(_combine_kernel, final=final),
        grid=(nt,),
        in_specs=[
            pl.BlockSpec((1, MOE_TOP_E, tp), lambda i: (i, 0, 0), memory_space=pltpu.SMEM),
            pl.BlockSpec((1, MOE_TOP_E, tp), lambda i: (jnp.minimum(i + 1, nt - 1), 0, 0),
                         memory_space=pltpu.SMEM),
            pl.BlockSpec((tp, d), lambda i: (i, 0)),
            pl.BlockSpec((tp, MOE_TOP_E), lambda i: (i, 0)),
            pl.BlockSpec((1, 6, d), lambda i: (i // per_seq, 0, 0)),
            pl.BlockSpec((1, d), lambda i: (0, 0)),
            pl.BlockSpec(memory_space=pl.ANY),
        ],
        out_specs=pl.BlockSpec((tp, d), lambda i: (i, 0)),
        out_shape=jax.ShapeDtypeStruct((t, d), F32),
        scratch_shapes=[pltpu.VMEM((2, MOE_TOP_E, tp, d), F32), pltpu.SemaphoreType.DMA((2,))],
        compiler_params=pltpu.CompilerParams(
            dimension_semantics=("arbitrary",), vmem_limit_bytes=VMEM_LIMIT),
    )(pos, pos, x1, wts_t, mod, final_g, ys)


def _router_weights(w_group, b_group, w_expert, b_expert):
    d, g = w_group.shape
    e = w_expert.shape[1]
    w = jnp.concatenate([w_group, w_expert], axis=1).T
    w = jnp.pad(w, ((0, ROUTER_ROWS - g - e), (0, 0)))
    hi, lo = _split_bf16(w)
    bias = jnp.pad(jnp.concatenate([b_group, b_expert]), (0, ROUTER_ROWS - g - e)).reshape(ROUTER_ROWS, 1)
    return hi, lo, bias


def _tile_plan(hist, n_rows):
    n_e = hist.shape[0]
    tm = GMM_TILE
    counts = jnp.sum(hist, axis=1).astype(jnp.int32)
    ends = jnp.cumsum(counts)
    starts = ends - counts
    first_tile = starts // tm
    items_per = jnp.where(counts > 0, (ends - 1) // tm - first_tile + 1, 0)
    item_ends = jnp.cumsum(items_per)
    n_items = item_ends[-1:]
    max_items = n_rows // tm + n_e - 1
    w = jnp.minimum(jnp.arange(max_items, dtype=jnp.int32), n_items[0] - 1)
    item_expert = jnp.sum(w[:, None] >= item_ends[None, :], axis=1).astype(jnp.int32)
    item_tile = first_tile[item_expert] + w - (item_ends - items_per)[item_expert]
    item_lo = jnp.maximum(starts[item_expert] - item_tile * tm, 0)
    item_hi = jnp.minimum(ends[item_expert] - item_tile * tm, tm)
    offsets = starts.astype(F32).reshape(n_e, 1)
    return offsets, (item_tile.astype(jnp.int32), item_expert, item_lo.astype(jnp.int32),
                     item_hi.astype(jnp.int32), n_items.astype(jnp.int32))


def kernel(x, c, norm_gain, w_mod, b_mod, pool_w, pool_b, pool_scale, gla_w_in, gla_w_gate, gla_b_gate, gla_norm_g, gla_w_out, moe_w_group, moe_b_group, moe_w_expert, moe_b_expert, moe_w_in, moe_w_out, final_norm_g):
    b, s, d = x.shape
    depth = w_mod.shape[0]
    t = b * s
    n_groups = moe_w_group.shape[-1]
    n_experts = moe_w_expert.shape[-1]
    n_rows = MOE_TOP_E * t
    mod_all = _modulation(c, w_mod, b_mod).reshape(depth, b, 6, d)
    fg = final_norm_g.reshape(1, d)
    for i in range(depth):
        mod = mod_all[i]
        wr_hi, wr_lo, br = _router_weights(moe_w_group[i], moe_b_group[i], moe_w_expert[i], moe_b_expert[i])
        j = i // 2
        if i % 2 == 0:
            x1, h2, ids, wts, hist = _pool_layer(x, mod, norm_gain[i], pool_w[j], pool_b[j], pool_scale[j],
                                                 wr_hi, wr_lo, br, n_groups, n_experts)
        else:
            x1, h2, ids, wts, hist = _gla_layer(x, mod, norm_gain[i], gla_w_in[j], gla_w_gate[j], gla_b_gate[j],
                                                gla_norm_g[j], gla_w_out[j], wr_hi, wr_lo, br, n_groups, n_experts)
        offsets, plan = _tile_plan(hist, n_rows)
        pos = _positions(ids, offsets)
        xs = _dispatch(pos, h2.reshape(t, d), n_rows)
        ys = _grouped_mlp(plan, xs, moe_w_in, moe_w_out, i)
        x = _combine(pos, x1.reshape(t, d), wts.T, mod, fg, ys, s, final=(i == depth - 1)).reshape(b, s, d)
    return x
```

```python
import functools

import jax
import jax.numpy as jnp
from jax import lax
from jax.experimental import pallas as pl
from jax.experimental.pallas import tpu as pltpu

F32 = jnp.float32
BF16 = jnp.bfloat16

NORM_EPS = 1e-6
POOL_WINDOWS = (2, 4, 8, 16)
POOL_HALO = 16
GLA_CHUNK = 64
GLA_GATE_TEMP = 16.0
MOE_TOP_E = 2
LANES = 128
SUBLANES = 8
ROUTER_ROWS = 128
VMEM_LIMIT = 56 * 1024 * 1024

SEQ_TILE_POOL = 512
SEQ_TILE_GLA = 256
GMM_TILE = 256
ROW_TILE = 512


def _dot(a, b):
    return jnp.dot(a, b, preferred_element_type=F32)


def _dot_nt(a, b):
    return lax.dot_general(a, b, (((1,), (1,)), ((), ())), preferred_element_type=F32)


def _dot_tn(a, b):
    return lax.dot_general(a, b, (((0,), (0,)), ((), ())), preferred_element_type=F32)


def _split_bf16(x):
    hi = x.astype(BF16)
    lo = (x - hi.astype(F32)).astype(BF16)
    return hi, lo


def _store_token_tiles(ref, x):
    n = x.shape[0]
    for s in range(SUBLANES):
        ref[pl.ds(s, n, stride=SUBLANES), :] = x[:, s * LANES:(s + 1) * LANES]


def _load_token_tiles(ref, n, s):
    return ref[pl.ds(s, n, stride=SUBLANES), :]


def _norm_mod(x, gain, scale, shift):
    ms = jnp.mean(x * x, axis=-1, keepdims=True)
    return x * lax.rsqrt(ms + NORM_EPS) * gain * (1.0 + scale) + shift


def _mod_kernel(c_ref, w_ref, b_ref, o_ref):
    c = c_ref[...]
    sc = (c * (1.0 / (1.0 + jnp.exp(-c)))).astype(BF16)
    o_ref[0] = _dot(sc, w_ref[0].astype(BF16)) + b_ref[0]


def _modulation(c, w_mod, b_mod):
    depth, d, n = w_mod.shape
    b = c.shape[0]
    tn = n // 4
    return pl.pallas_call(
        _mod_kernel,
        grid=(depth, n // tn),
        in_specs=[
            pl.BlockSpec((b, d), lambda i, j: (0, 0)),
            pl.BlockSpec((1, d, tn), lambda i, j: (i, 0, j)),
            pl.BlockSpec((1, 1, tn), lambda i, j: (i, 0, j)),
        ],
        out_specs=pl.BlockSpec((1, b, tn), lambda i, j: (i, 0, j)),
        out_shape=jax.ShapeDtypeStruct((depth, b, n), F32),
        compiler_params=pltpu.CompilerParams(vmem_limit_bytes=VMEM_LIMIT),
    )(c, w_mod, b_mod.reshape(depth, 1, n))


def _router_epilogue(x1, mod, gain2, wr_hi_ref, wr_lo_ref, br_ref, n_groups, n_experts,
                     h2_ref, ids_ref, wts_ref, hist_ref, first_step):
    h2 = _norm_mod(x1, gain2, mod[4:5], mod[3:4])
    _store_token_tiles(h2_ref, h2)
    hh, hl = _split_bf16(h2)
    wh = wr_hi_ref[...]
    lt = _dot_nt(wh, hh) + _dot_nt(wh, hl) + _dot_nt(wr_lo_ref[...], hh) + br_ref[...]
    ts = lt.shape[1]
    per = n_experts // n_groups
    lg = lt[0:n_groups]
    mg = jnp.max(lg, axis=0, keepdims=True)
    p_g = 1.0 / jnp.sum(jnp.exp(lg - mg), axis=0, keepdims=True)
    gi = lax.broadcasted_iota(jnp.int32, lg.shape, 0)
    g_idx = jnp.min(jnp.where(lg == mg, gi, n_groups), axis=0, keepdims=True)
    sel = jnp.zeros((per, ts), F32)
    for g in range(n_groups):
        sel = jnp.where(g_idx == g, lt[n_groups + g * per:n_groups + (g + 1) * per], sel)
    ei = lax.broadcasted_iota(jnp.int32, sel.shape, 0)
    m1 = jnp.max(sel, axis=0, keepdims=True)
    i1 = jnp.min(jnp.where(sel == m1, ei, per), axis=0, keepdims=True)
    rest = jnp.where(ei == i1, -jnp.inf, sel)
    m2 = jnp.max(rest, axis=0, keepdims=True)
    i2 = jnp.min(jnp.where(rest == m2, ei, per), axis=0, keepdims=True)
    e21 = jnp.exp(m2 - m1)
    w1 = p_g / (1.0 + e21)
    w2 = p_g * e21 / (1.0 + e21)
    id1 = g_idx * per + i1
    id2 = g_idx * per + i2
    ids_ref[...] = jnp.concatenate([id1, id2], axis=0)
    wts_ref[...] = jnp.concatenate([w1, w2], axis=0)
    xi = lax.broadcasted_iota(jnp.int32, (n_experts, ts), 0)
    cnt = (xi == id1).astype(F32) + (xi == id2).astype(F32)

    @pl.when(first_step)
    def _():
        hist_ref[...] = cnt

    @pl.when(jnp.logical_not(first_step))
    def _():
        hist_ref[...] += cnt


def _router_specs(d, ts, n_seq_tiles, n_experts):
    in_specs = [
        pl.BlockSpec((ROUTER_ROWS, d), lambda b, s: (0, 0)),
        pl.BlockSpec((ROUTER_ROWS, d), lambda b, s: (0, 0)),
        pl.BlockSpec((ROUTER_ROWS, 1), lambda b, s: (0, 0)),
    ]
    out_specs = [
        pl.BlockSpec((1, ts, d), lambda b, s: (b, s, 0)),
        pl.BlockSpec((ts * SUBLANES, LANES), lambda b, s: (b * n_seq_tiles + s, 0)),
        pl.BlockSpec((MOE_TOP_E, ts), lambda b, s: (0, b * n_seq_tiles + s)),
        pl.BlockSpec((MOE_TOP_E, ts), lambda b, s: (0, b * n_seq_tiles + s)),
        pl.BlockSpec((n_experts, ts), lambda b, s: (0, 0)),
    ]
    return in_specs, out_specs


def _router_out_shapes(b, s, d, ts, n_experts):
    assert d == SUBLANES * LANES, "token-per-tile row layout needs d_model == one (8,128) tile"
    return [
        jax.ShapeDtypeStruct((b, s, d), F32),
        jax.ShapeDtypeStruct((b * s * SUBLANES, LANES), F32),
        jax.ShapeDtypeStruct((MOE_TOP_E, b * s), jnp.int32),
        jax.ShapeDtypeStruct((MOE_TOP_E, b * s), F32),
        jax.ShapeDtypeStruct((n_experts, ts), F32),
    ]


def _pool_kernel(x_ref, mod_ref, gain_ref, pw_ref, pb_ref, ps_ref, wr_hi_ref, wr_lo_ref, br_ref,
                 x1_ref, h2_ref, ids_ref, wts_ref, hist_ref, hbuf_ref, *, n_groups, n_experts):
    b, s = pl.program_id(0), pl.program_id(1)
    x = x_ref[0]
    ts, d = x.shape
    mod = mod_ref[0]
    h = _norm_mod(x, gain_ref[0:1], mod[1:2], mod[0:1])

    @pl.when(s == 0)
    def _():
        hbuf_ref[0:POOL_HALO] = jnp.zeros((POOL_HALO, d), F32)

    hbuf_ref[POOL_HALO:POOL_HALO + ts] = h
    pos = (s * ts + 1 + lax.broadcasted_iota(jnp.int32, (ts, 1), 0)).astype(F32)
    cg = d // len(POOL_WINDOWS)
    ys = []
    for gi, win in enumerate(POOL_WINDOWS):
        cols = slice(gi * cg, (gi + 1) * cg)
        acc = h[:, cols]
        for j in range(1, win):
            acc = acc + hbuf_ref[POOL_HALO - j:POOL_HALO - j + ts, cols]
        dgi = acc / jnp.minimum(pos, float(win)) - h[:, cols]
        ys.append(_dot(dgi.astype(BF16), pw_ref[gi]))
    y = (jnp.concatenate(ys, axis=-1) + pb_ref[...]) * ps_ref[...]
    hbuf_ref[0:POOL_HALO] = hbuf_ref[ts:ts + POOL_HALO]
    x1 = x + mod[2:3] * y
    x1_ref[0] = x1
    _router_epilogue(x1, mod, gain_ref[1:2], wr_hi_ref, wr_lo_ref, br_ref, n_groups, n_experts,
                     h2_ref, ids_ref, wts_ref, hist_ref, jnp.logical_and(b == 0, s == 0))


def _pool_layer(x, mod, gain, pw, pb, ps, wr_hi, wr_lo, br, n_groups, n_experts):
    b, s, d = x.shape
    ts = min(SEQ_TILE_POOL, s)
    nst = s // ts
    g, cg = pw.shape[0], pw.shape[1]
    r_in, r_out = _router_specs(d, ts, nst, n_experts)
    return pl.pallas_call(
        functools.partial(_pool_kernel, n_groups=n_groups, n_experts=n_experts),
        grid=(b, nst),
        in_specs=[
            pl.BlockSpec((1, ts, d), lambda bb, ss: (bb, ss, 0)),
            pl.BlockSpec((1, 6, d), lambda bb, ss: (bb, 0, 0)),
            pl.BlockSpec((2, d), lambda bb, ss: (0, 0)),
            pl.BlockSpec((g, cg, cg), lambda bb, ss: (0, 0, 0)),
            pl.BlockSpec((1, d), lambda bb, ss: (0, 0)),
            pl.BlockSpec((1, d), lambda bb, ss: (0, 0)),
        ] + r_in,
        out_specs=r_out,
        out_shape=_router_out_shapes(b, s, d, ts, n_experts),
        scratch_shapes=[pltpu.VMEM((POOL_HALO + ts, d), F32)],
        compiler_params=pltpu.CompilerParams(
            dimension_semantics=("arbitrary", "arbitrary"), vmem_limit_bytes=VMEM_LIMIT),
    )(x, mod, gain, pw.astype(BF16), pb.reshape(1, d), ps.reshape(1, d), wr_hi, wr_lo, br)


def _gla_kernel(x_ref, mod_ref, gain_ref, wp_ref, wz_ref, wg_ref, bg_ref, ng_ref, wo_ref,
                wr_hi_ref, wr_lo_ref, br_ref,
                x1_ref, h2_ref, ids_ref, wts_ref, hist_ref,
                state_ref, q_s, k_s, g_s, v_s, r_s, o_s, *, n_heads, n_groups, n_experts):
    b, s = pl.program_id(0), pl.program_id(1)
    x = x_ref[0]
    ts, d = x.shape
    dk_all = q_s.shape[1]
    dk = dk_all // n_heads
    dv = d // n_heads
    mod = mod_ref[0]
    h = _norm_mod(x, gain_ref[0:1], mod[1:2], mod[0:1]).astype(BF16)

    @pl.when(s == 0)
    def _():
        state_ref[...] = jnp.zeros(state_ref.shape, F32)

    q_s[...] = _dot(h, wp_ref[:, 0:dk_all]) * (dk ** -0.5)
    k_s[...] = _dot(h, wp_ref[:, dk_all:2 * dk_all])
    v_s[...] = _dot(h, wp_ref[:, 2 * dk_all:2 * dk_all + d]).astype(BF16)
    r = _dot(h, wp_ref[:, 2 * dk_all + d:2 * dk_all + 2 * d])
    r_s[...] = r * (1.0 / (1.0 + jnp.exp(-r)))
    z = _dot(h, wz_ref[...]).astype(BF16)
    u = _dot(z, wg_ref[...]) + bg_ref[...]
    g_s[...] = (jnp.minimum(u, 0.0) - jnp.log(1.0 + jnp.exp(-jnp.abs(u)))) * (1.0 / GLA_GATE_TEMP)

    c = GLA_CHUNK
    ri = lax.broadcasted_iota(jnp.int32, (c, c), 0)
    ci = lax.broadcasted_iota(jnp.int32, (c, c), 1)
    causal = ri >= ci
    tril = jnp.where(causal, 1.0, 0.0).astype(BF16)
    ng = ng_ref[...]

    def chunk(n, carry):
        rows = pl.ds(pl.multiple_of(n * c, c), c)
        g_hi, g_lo = _split_bf16(g_s[rows, :])
        bc = _dot(tril, g_hi) + _dot(tril, g_lo)
        bl = bc[c - 1:c, :]
        q_e = (q_s[rows, :] * jnp.exp(bc)).astype(BF16)
        kk = k_s[rows, :]
        k_e = (kk * jnp.exp(-bc)).astype(BF16)
        k_d = (kk * jnp.exp(bl - bc)).astype(BF16)
        e_l = jnp.exp(bl)
        vv = v_s[rows, :]
        for hd in range(n_heads):
            ks = slice(hd * dk, (hd + 1) * dk)
            vs = slice(hd * dv, (hd + 1) * dv)
            sc = jnp.where(causal, _dot_nt(q_e[:, ks], k_e[:, ks]), 0.0).astype(BF16)
            st = state_ref[hd]
            o = _dot(sc, vv[:, vs]) + _dot_nt(q_e[:, ks], st.astype(BF16))
            state_ref[hd] = st * e_l[:, ks] + _dot_tn(vv[:, vs], k_d[:, ks])
            o = o * lax.rsqrt(jnp.mean(o * o, axis=-1, keepdims=True) + NORM_EPS) * ng
            o_s[rows, vs] = (o * r_s[rows, vs]).astype(BF16)
        return carry

    lax.fori_loop(0, ts // c, chunk, 0)
    x1 = x + mod[2:3] * _dot(o_s[...], wo_ref[...])
    x1_ref[0] = x1
    _router_epilogue(x1, mod, gain_ref[1:2], wr_hi_ref, wr_lo_ref, br_ref, n_groups, n_experts,
                     h2_ref, ids_ref, wts_ref, hist_ref, jnp.logical_and(b == 0, s == 0))


def _gla_layer(x, mod, gain, w_in, w_gate, b_gate, norm_g, w_out, wr_hi, wr_lo, br, n_groups, n_experts):
    b, s, d = x.shape
    ts = min(SEQ_TILE_GLA, s)
    nst = s // ts
    rank, dk_all = w_gate.shape
    dv = norm_g.shape[0]
    n_heads = d // dv
    n_proj = 2 * dk_all + 2 * d
    wp = w_in[:, :n_proj].astype(BF16)
    wz = jnp.pad(w_in[:, n_proj:], ((0, 0), (0, LANES - rank))).astype(BF16)
    wg = jnp.pad(w_gate, ((0, LANES - rank), (0, 0))).astype(BF16)
    r_in, r_out = _router_specs(d, ts, nst, n_experts)
    const = lambda bb, ss: (0, 0)
    return pl.pallas_call(
        functools.partial(_gla_kernel, n_heads=n_heads, n_groups=n_groups, n_experts=n_experts),
        grid=(b, nst),
        in_specs=[
            pl.BlockSpec((1, ts, d), lambda bb, ss: (bb, ss, 0)),
            pl.BlockSpec((1, 6, d), lambda bb, ss: (bb, 0, 0)),
            pl.BlockSpec((2, d), const),
            pl.BlockSpec((d, n_proj), const),
            pl.BlockSpec((d, LANES), const),
            pl.BlockSpec((LANES, dk_all), const),
            pl.BlockSpec((1, dk_all), const),
            pl.BlockSpec((1, dv), const),
            pl.BlockSpec((d, d), const),
        ] + r_in,
        out_specs=r_out,
        out_shape=_router_out_shapes(b, s, d, ts, n_experts),
        scratch_shapes=[
            pltpu.VMEM((n_heads, dv, dk_all // n_heads), F32),
            pltpu.VMEM((ts, dk_all), F32),
            pltpu.VMEM((ts, dk_all), F32),
            pltpu.VMEM((ts, dk_all), F32),
            pltpu.VMEM((ts, d), BF16),
            pltpu.VMEM((ts, d), F32),
            pltpu.VMEM((ts, d), BF16),
        ],
        compiler_params=pltpu.CompilerParams(
            dimension_semantics=("arbitrary", "arbitrary"), vmem_limit_bytes=VMEM_LIMIT),
    )(x, mod, gain, wp, wz, wg, b_gate.reshape(1, dk_all), norm_g.reshape(1, dv), w_out.astype(BF16),
      wr_hi, wr_lo, br)


def _pos_kernel(ids_ref, off_ref, pos_ref, run_ref):
    i = pl.program_id(0)
    ids = ids_ref[...]
    tp = ids.shape[1]
    n_e = off_ref.shape[0]

    @pl.when(i == 0)
    def _():
        run_ref[...] = jnp.zeros(run_ref.shape, F32)

    ei = lax.broadcasted_iota(jnp.int32, (n_e, tp), 0)
    oh0 = ei == ids[0:1]
    oh1 = ei == ids[1:2]
    tri = jnp.where(lax.broadcasted_iota(jnp.int32, (tp, tp), 0) <= lax.broadcasted_iota(jnp.int32, (tp, tp), 1),
                    1.0, 0.0).astype(BF16)
    inc0 = _dot(jnp.where(oh0, 1.0, 0.0).astype(BF16), tri)
    inc1 = _dot(jnp.where(oh1, 1.0, 0.0).astype(BF16), tri)
    tot0 = inc0[:, tp - 1:tp]
    tot1 = inc1[:, tp - 1:tp]
    base = off_ref[...] + run_ref[...]
    p0 = jnp.sum(jnp.where(oh0, base + inc0 - 1.0, 0.0), axis=0, keepdims=True)
    p1 = jnp.sum(jnp.where(oh1, base + tot0 + inc1 - 1.0, 0.0), axis=0, keepdims=True)
    pos_ref[0] = jnp.concatenate([p0, p1], axis=0).astype(jnp.int32)
    run_ref[...] += tot0 + tot1


def _positions(ids, offsets):
    t = ids.shape[1]
    tp = min(ROW_TILE, t)
    n_e = offsets.shape[0]
    return pl.pallas_call(
        _pos_kernel,
        grid=(t // tp,),
        in_specs=[pl.BlockSpec((MOE_TOP_E, tp), lambda i: (0, i)),
                  pl.BlockSpec((n_e, 1), lambda i: (0, 0))],
        out_specs=pl.BlockSpec((1, MOE_TOP_E, tp), lambda i: (i, 0, 0)),
        out_shape=jax.ShapeDtypeStruct((t // tp, MOE_TOP_E, tp), jnp.int32),
        scratch_shapes=[pltpu.VMEM((n_e, 1), F32)],
        compiler_params=pltpu.CompilerParams(dimension_semantics=("arbitrary",)),
    )(ids, offsets)


def _tile_rows(row):
    return pl.ds(pl.multiple_of(row * SUBLANES, SUBLANES), SUBLANES)


def _for_each_token(tp, fn):
    def group(j, carry):
        r0 = pl.multiple_of(j * SUBLANES, SUBLANES)
        for u in range(SUBLANES):
            fn(r0 + u)
        return carry
    lax.fori_loop(0, tp // SUBLANES, group, 0)


def _dispatch_kernel(pos_ref, h_ref, xs_ref, sem):
    tp = pos_ref.shape[2]

    def issue(r):
        for k in range(MOE_TOP_E):
            pltpu.make_async_copy(h_ref.at[_tile_rows(r)], xs_ref.at[_tile_rows(pos_ref[0, k, r])], sem).start()

    _for_each_token(tp, issue)
    for k in range(MOE_TOP_E):
        pltpu.make_async_copy(h_ref, xs_ref.at[pl.ds(0, tp * SUBLANES)], sem).wait()


def _dispatch(pos, h2, n_rows):
    nt, _, tp = pos.shape
    return pl.pallas_call(
        _dispatch_kernel,
        grid=(nt,),
        in_specs=[pl.BlockSpec((1, MOE_TOP_E, tp), lambda i: (i, 0, 0), memory_space=pltpu.SMEM),
                  pl.BlockSpec((tp * SUBLANES, LANES), lambda i: (i, 0))],
        out_specs=pl.BlockSpec(memory_space=pl.ANY),
        out_shape=jax.ShapeDtypeStruct((n_rows * SUBLANES, LANES), F32),
        scratch_shapes=[pltpu.SemaphoreType.DMA(())],
        compiler_params=pltpu.CompilerParams(dimension_semantics=("arbitrary",), has_side_effects=True),
    )(pos, h2)


def _gmm_kernel(it_ref, ie_ref, lo_ref, hi_ref, n_ref, xs_ref, wi_ref, wo_ref, ys_ref, wi_s, wo_s):
    w = pl.program_id(0)
    f = wo_ref.shape[2]
    tm = xs_ref.shape[0] // SUBLANES

    @pl.when(w < n_ref[0])
    def _():
        prev = jnp.maximum(w - 1, 0)

        @pl.when(jnp.logical_or(w == 0, ie_ref[w] != ie_ref[prev]))
        def _():
            wi_s[...] = wi_ref[0, 0].astype(BF16)
            wo_s[...] = wo_ref[0, 0].astype(BF16)

        x = jnp.concatenate([_load_token_tiles(xs_ref, tm, s).astype(BF16) for s in range(SUBLANES)], axis=-1)
        gu = _dot(x, wi_s[...])
        gate, up = gu[:, :f], gu[:, f:]
        a = gate * (1.0 / (1.0 + jnp.exp(-gate))) * up
        y = _dot(a.astype(BF16), wo_s[...])
        row = lax.broadcasted_iota(jnp.int32, (tm, 1), 0)
        mine = jnp.logical_and(row >= lo_ref[w], row < hi_ref[w])
        first_visit = jnp.logical_or(w == 0, it_ref[w] != it_ref[prev])

        @pl.when(first_visit)
        def _():
            _store_token_tiles(ys_ref, jnp.where(mine, y, 0.0))

        @pl.when(jnp.logical_not(first_visit))
        def _():
            for s in range(SUBLANES):
                old = _load_token_tiles(ys_ref, tm, s)
                ys_ref[pl.ds(s, tm, stride=SUBLANES), :] = jnp.where(mine, y[:, s * LANES:(s + 1) * LANES], old)


def _grouped_mlp(plan, xs, w_in, w_out, layer):
    item_tile, item_expert, item_lo, item_hi, n_items = plan
    d, f2 = w_in.shape[-2:]
    f = f2 // 2
    tm = GMM_TILE
    rows = tm * SUBLANES
    return pl.pallas_call(
        _gmm_kernel,
        grid_spec=pltpu.PrefetchScalarGridSpec(
            num_scalar_prefetch=5,
            grid=(item_tile.shape[0],),
            in_specs=[
                pl.BlockSpec((rows, LANES), lambda w, it, ie, lo, hi, n: (it[w], 0)),
                pl.BlockSpec((1, 1, d, f2), lambda w, it, ie, lo, hi, n: (layer, ie[w], 0, 0)),
                pl.BlockSpec((1, 1, f, d), lambda w, it, ie, lo, hi, n: (layer, ie[w], 0, 0)),
            ],
            out_specs=pl.BlockSpec((rows, LANES), lambda w, it, ie, lo, hi, n: (it[w], 0)),
            scratch_shapes=[pltpu.VMEM((d, f2), BF16), pltpu.VMEM((f, d), BF16)],
        ),
        out_shape=jax.ShapeDtypeStruct(xs.shape, F32),
        compiler_params=pltpu.CompilerParams(
            dimension_semantics=("arbitrary",), vmem_limit_bytes=VMEM_LIMIT),
    )(item_tile, item_expert, item_lo, item_hi, n_items, xs, w_in, w_out)


def _combine_kernel(pos_ref, posn_ref, x1_ref, wt_ref, mod_ref, fg_ref, ys_ref, o_ref, buf, sems, *, final):
    i = pl.program_id(0)
    n = pl.num_programs(0)
    tp = pos_ref.shape[2]

    def issue(idx_ref, slot):
        def one(r):
            for k in range(MOE_TOP_E):
                pltpu.make_async_copy(ys_ref.at[_tile_rows(idx_ref[0, k, r])], buf.at[slot, k, _tile_rows(r)],
                                      sems.at[slot]).start()
        _for_each_token(tp, one)

    slot = lax.rem(i, 2)

    @pl.when(i == 0)
    def _():
        issue(pos_ref, 0)

    @pl.when(i + 1 < n)
    def _():
        issue(posn_ref, 1 - slot)

    for k in range(MOE_TOP_E):
        pltpu.make_async_copy(ys_ref.at[pl.ds(0, tp * SUBLANES)], buf.at[slot, k], sems.at[slot]).wait()
    wt = wt_ref[...]
    gate2 = mod_ref[0, 5:6]
    chunks = []
    for s in range(SUBLANES):
        rows = pl.ds(s, tp, stride=SUBLANES)
        cols = slice(s * LANES, (s + 1) * LANES)
        y = wt[:, 0:1] * buf[slot, 0, rows, :] + wt[:, 1:2] * buf[slot, 1, rows, :]
        chunks.append(x1_ref[:, cols] + gate2[:, cols] * y)
    x = jnp.concatenate(chunks, axis=-1)
    if final:
        x = x * lax.rsqrt(jnp.mean(x * x, axis=-1, keepdims=True) + NORM_EPS) * fg_ref[...]
    o_ref[...] = x


def _combine(pos, x1, wts_t, mod, final_g, ys, seq_len, final):
    nt, _, tp = pos.shape
    t, d = x1.shape
    per_seq = seq_len // tp
    return pl.pallas_call(
        functools.partial(_combine_kernel, final=final),
        grid=(nt,),
        in_specs=[
            pl.BlockSpec((1, MOE_TOP_E, tp), lambda i: (i, 0, 0), memory_space=pltpu.SMEM),
            pl.BlockSpec((1, MOE_TOP_E, tp), lambda i: (jnp.minimum(i + 1, nt - 1), 0, 0),
                         memory_space=pltpu.SMEM),
            pl.BlockSpec((tp, d), lambda i: (i, 0)),
            pl.BlockSpec((tp, MOE_TOP_E), lambda i: (i, 0)),
            pl.BlockSpec((1, 6, d), lambda i: (i // per_seq, 0, 0)),
            pl.BlockSpec((1, d), lambda i: (0, 0)),
            pl.BlockSpec(memory_space=pl.ANY),
        ],
        out_specs=pl.BlockSpec((tp, d), lambda i: (i, 0)),
        out_shape=jax.ShapeDtypeStruct((t, d), F32),
        scratch_shapes=[pltpu.VMEM((2, MOE_TOP_E, tp * SUBLANES, LANES), F32), pltpu.SemaphoreType.DMA((2,))],
        compiler_params=pltpu.CompilerParams(
            dimension_semantics=("arbitrary",), vmem_limit_bytes=VMEM_LIMIT),
    )(pos, pos, x1, wts_t, mod, final_g, ys)


def _router_weights(w_group, b_group, w_expert, b_expert):
    d, g = w_group.shape
    e = w_expert.shape[1]
    w = jnp.concatenate([w_group, w_expert], axis=1).T
    w = jnp.pad(w, ((0, ROUTER_ROWS - g - e), (0, 0)))
    hi, lo = _split_bf16(w)
    bias = jnp.pad(jnp.concatenate([b_group, b_expert]), (0, ROUTER_ROWS - g - e)).reshape(ROUTER_ROWS, 1)
    return hi, lo, bias


def _tile_plan(hist, n_rows):
    n_e = hist.shape[0]
    tm = GMM_TILE
    counts = jnp.sum(hist, axis=1).astype(jnp.int32)
    ends = jnp.cumsum(counts)
    starts = ends - counts
    first_tile = starts // tm
    items_per = jnp.where(counts > 0, (ends - 1) // tm - first_tile + 1, 0)
    item_ends = jnp.cumsum(items_per)
    n_items = item_ends[-1:]
    max_items = n_rows // tm + n_e - 1
    w = jnp.minimum(jnp.arange(max_items, dtype=jnp.int32), n_items[0] - 1)
    item_expert = jnp.sum(w[:, None] >= item_ends[None, :], axis=1).astype(jnp.int32)
    item_tile = first_tile[item_expert] + w - (item_ends - items_per)[item_expert]
    item_lo = jnp.maximum(starts[item_expert] - item_tile * tm, 0)
    item_hi = jnp.minimum(ends[item_expert] - item_tile * tm, tm)
    offsets = starts.astype(F32).reshape(n_e, 1)
    return offsets, (item_tile.astype(jnp.int32), item_expert, item_lo.astype(jnp.int32),
                     item_hi.astype(jnp.int32), n_items.astype(jnp.int32))


def kernel(x, c, norm_gain, w_mod, b_mod, pool_w, pool_b, pool_scale, gla_w_in, gla_w_gate, gla_b_gate, gla_norm_g, gla_w_out, moe_w_group, moe_b_group, moe_w_expert, moe_b_expert, moe_w_in, moe_w_out, final_norm_g):
    b, s, d = x.shape
    depth = w_mod.shape[0]
    t = b * s
    n_groups = moe_w_group.shape[-1]
    n_experts = moe_w_expert.shape[-1]
    n_rows = MOE_TOP_E * t
    mod_all = _modulation(c, w_mod, b_mod).reshape(depth, b, 6, d)
    fg = final_norm_g.reshape(1, d)
    for i in range(depth):
        mod = mod_all[i]
        wr_hi, wr_lo, br = _router_weights(moe_w_group[i], moe_b_group[i], moe_w_expert[i], moe_b_expert[i])
        j = i // 2
        if i % 2 == 0:
            x1, h2, ids, wts, hist = _pool_layer(x, mod, norm_gain[i], pool_w[j], pool_b[j], pool_scale[j],
                                                 wr_hi, wr_lo, br, n_groups, n_experts)
        else:
            x1, h2, ids, wts, hist = _gla_layer(x, mod, norm_gain[i], gla_w_in[j], gla_w_gate[j], gla_b_gate[j],
                                                gla_norm_g[j], gla_w_out[j], wr_hi, wr_lo, br, n_groups, n_experts)
        offsets, plan = _tile_plan(hist, n_rows)
        pos = _positions(ids, offsets)
        xs = _dispatch(pos, h2, n_rows)
        ys = _grouped_mlp(plan, xs, moe_w_in, moe_w_out, i)
        x = _combine(pos, x1.reshape(t, d), wts.T, mod, fg, ys, s, final=(i == depth - 1)).reshape(b, s, d)
    return x
```

```python
import functools

import jax
import jax.numpy as jnp
from jax import lax
from jax.experimental import pallas as pl
from jax.experimental.pallas import tpu as pltpu

F32 = jnp.float32
BF16 = jnp.bfloat16

NORM_EPS = 1e-6
POOL_WINDOWS = (2, 4, 8, 16)
POOL_HALO = 16
GLA_CHUNK = 64
GLA_GATE_TEMP = 16.0
MOE_TOP_E = 2
LANES = 128
SUBLANES = 8
ROUTER_ROWS = 128
VMEM_LIMIT = 56 * 1024 * 1024

SEQ_TILE_POOL = 512
SEQ_TILE_GLA = 512
GMM_TILE = 256
ROW_TILE = 512


def _dot(a, b):
    return jnp.dot(a, b, preferred_element_type=F32)


def _dot_nt(a, b):
    return lax.dot_general(a, b, (((1,), (1,)), ((), ())), preferred_element_type=F32)


def _dot_tn(a, b):
    return lax.dot_general(a, b, (((0,), (0,)), ((), ())), preferred_element_type=F32)


def _split_bf16(x):
    hi = x.astype(BF16)
    lo = (x - hi.astype(F32)).astype(BF16)
    return hi, lo


def _store_token_tiles(ref, x):
    n = x.shape[0]
    for s in range(SUBLANES):
        ref[pl.ds(s, n, stride=SUBLANES), :] = x[:, s * LANES:(s + 1) * LANES]


def _load_token_tiles(ref, n, s):
    return ref[pl.ds(s, n, stride=SUBLANES), :]


def _norm_mod(x, gain, scale, shift):
    ms = jnp.mean(x * x, axis=-1, keepdims=True)
    return x * lax.rsqrt(ms + NORM_EPS) * gain * (1.0 + scale) + shift


def _mod_kernel(c_ref, w_ref, b_ref, o_ref):
    c = c_ref[...]
    sc = (c * (1.0 / (1.0 + jnp.exp(-c)))).astype(BF16)
    o_ref[0] = _dot(sc, w_ref[0].astype(BF16)) + b_ref[0]


def _modulation(c, w_mod, b_mod):
    depth, d, n = w_mod.shape
    b = c.shape[0]
    tn = n // 4
    return pl.pallas_call(
        _mod_kernel,
        grid=(depth, n // tn),
        in_specs=[
            pl.BlockSpec((b, d), lambda i, j: (0, 0)),
            pl.BlockSpec((1, d, tn), lambda i, j: (i, 0, j)),
            pl.BlockSpec((1, 1, tn), lambda i, j: (i, 0, j)),
        ],
        out_specs=pl.BlockSpec((1, b, tn), lambda i, j: (i, 0, j)),
        out_shape=jax.ShapeDtypeStruct((depth, b, n), F32),
        compiler_params=pltpu.CompilerParams(vmem_limit_bytes=VMEM_LIMIT),
    )(c, w_mod, b_mod.reshape(depth, 1, n))


def _router_epilogue(x1, mod, gain2, wr_hi_ref, wr_lo_ref, br_ref, n_groups, n_experts,
                     h2_ref, ids_ref, wts_ref, hist_ref, first_step):
    h2 = _norm_mod(x1, gain2, mod[4:5], mod[3:4])
    _store_token_tiles(h2_ref, h2)
    hh, hl = _split_bf16(h2)
    wh = wr_hi_ref[...]
    lt = _dot_nt(wh, hh) + _dot_nt(wh, hl) + _dot_nt(wr_lo_ref[...], hh) + br_ref[...]
    ts = lt.shape[1]
    per = n_experts // n_groups
    lg = lt[0:n_groups]
    mg = jnp.max(lg, axis=0, keepdims=True)
    p_g = 1.0 / jnp.sum(jnp.exp(lg - mg), axis=0, keepdims=True)
    gi = lax.broadcasted_iota(jnp.int32, lg.shape, 0)
    g_idx = jnp.min(jnp.where(lg == mg, gi, n_groups), axis=0, keepdims=True)
    sel = jnp.zeros((per, ts), F32)
    for g in range(n_groups):
        sel = jnp.where(g_idx == g, lt[n_groups + g * per:n_groups + (g + 1) * per], sel)
    ei = lax.broadcasted_iota(jnp.int32, sel.shape, 0)
    m1 = jnp.max(sel, axis=0, keepdims=True)
    i1 = jnp.min(jnp.where(sel == m1, ei, per), axis=0, keepdims=True)
    rest = jnp.where(ei == i1, -jnp.inf, sel)
    m2 = jnp.max(rest, axis=0, keepdims=True)
    i2 = jnp.min(jnp.where(rest == m2, ei, per), axis=0, keepdims=True)
    e21 = jnp.exp(m2 - m1)
    w1 = p_g / (1.0 + e21)
    w2 = p_g * e21 / (1.0 + e21)
    id1 = g_idx * per + i1
    id2 = g_idx * per + i2
    ids_ref[...] = jnp.concatenate([id1, id2], axis=0)
    wts_ref[...] = jnp.concatenate([w1, w2], axis=0)
    xi = lax.broadcasted_iota(jnp.int32, (n_experts, ts), 0)
    cnt = (xi == id1).astype(F32) + (xi == id2).astype(F32)

    @pl.when(first_step)
    def _():
        hist_ref[...] = cnt

    @pl.when(jnp.logical_not(first_step))
    def _():
        hist_ref[...] += cnt


def _router_specs(d, ts, n_seq_tiles, n_experts):
    in_specs = [
        pl.BlockSpec((ROUTER_ROWS, d), lambda b, s: (0, 0)),
        pl.BlockSpec((ROUTER_ROWS, d), lambda b, s: (0, 0)),
        pl.BlockSpec((ROUTER_ROWS, 1), lambda b, s: (0, 0)),
    ]
    out_specs = [
        pl.BlockSpec((1, ts, d), lambda b, s: (b, s, 0)),
        pl.BlockSpec((ts * SUBLANES, LANES), lambda b, s: (b * n_seq_tiles + s, 0)),
        pl.BlockSpec((MOE_TOP_E, ts), lambda b, s: (0, b * n_seq_tiles + s)),
        pl.BlockSpec((MOE_TOP_E, ts), lambda b, s: (0, b * n_seq_tiles + s)),
        pl.BlockSpec((n_experts, ts), lambda b, s: (0, 0)),
    ]
    return in_specs, out_specs


def _router_out_shapes(b, s, d, ts, n_experts):
    assert d == SUBLANES * LANES, "token-per-tile row layout needs d_model == one (8,128) tile"
    return [
        jax.ShapeDtypeStruct((b, s, d), F32),
        jax.ShapeDtypeStruct((b * s * SUBLANES, LANES), F32),
        jax.ShapeDtypeStruct((MOE_TOP_E, b * s), jnp.int32),
        jax.ShapeDtypeStruct((MOE_TOP_E, b * s), F32),
        jax.ShapeDtypeStruct((n_experts, ts), F32),
    ]


def _pool_kernel(x_ref, mod_ref, gain_ref, pw_ref, pb_ref, ps_ref, wr_hi_ref, wr_lo_ref, br_ref,
                 x1_ref, h2_ref, ids_ref, wts_ref, hist_ref, hbuf_ref, *, n_groups, n_experts):
    b, s = pl.program_id(0), pl.program_id(1)
    x = x_ref[0]
    ts, d = x.shape
    mod = mod_ref[0]
    h = _norm_mod(x, gain_ref[0:1], mod[1:2], mod[0:1])

    @pl.when(s == 0)
    def _():
        hbuf_ref[0:POOL_HALO] = jnp.zeros((POOL_HALO, d), F32)

    hbuf_ref[POOL_HALO:POOL_HALO + ts] = h
    pos = (s * ts + 1 + lax.broadcasted_iota(jnp.int32, (ts, 1), 0)).astype(F32)
    cg = d // len(POOL_WINDOWS)
    ys = []
    for gi, win in enumerate(POOL_WINDOWS):
        cols = slice(gi * cg, (gi + 1) * cg)
        acc = h[:, cols]
        for j in range(1, win):
            acc = acc + hbuf_ref[POOL_HALO - j:POOL_HALO - j + ts, cols]
        dgi = acc / jnp.minimum(pos, float(win)) - h[:, cols]
        ys.append(_dot(dgi.astype(BF16), pw_ref[gi]))
    y = (jnp.concatenate(ys, axis=-1) + pb_ref[...]) * ps_ref[...]
    hbuf_ref[0:POOL_HALO] = hbuf_ref[ts:ts + POOL_HALO]
    x1 = x + mod[2:3] * y
    x1_ref[0] = x1
    _router_epilogue(x1, mod, gain_ref[1:2], wr_hi_ref, wr_lo_ref, br_ref, n_groups, n_experts,
                     h2_ref, ids_ref, wts_ref, hist_ref, jnp.logical_and(b == 0, s == 0))


def _pool_layer(x, mod, gain, pw, pb, ps, wr_hi, wr_lo, br, n_groups, n_experts):
    b, s, d = x.shape
    ts = min(SEQ_TILE_POOL, s)
    nst = s // ts
    g, cg = pw.shape[0], pw.shape[1]
    r_in, r_out = _router_specs(d, ts, nst, n_experts)
    return pl.pallas_call(
        functools.partial(_pool_kernel, n_groups=n_groups, n_experts=n_experts),
        grid=(b, nst),
        in_specs=[
            pl.BlockSpec((1, ts, d), lambda bb, ss: (bb, ss, 0)),
            pl.BlockSpec((1, 6, d), lambda bb, ss: (bb, 0, 0)),
            pl.BlockSpec((2, d), lambda bb, ss: (0, 0)),
            pl.BlockSpec((g, cg, cg), lambda bb, ss: (0, 0, 0)),
            pl.BlockSpec((1, d), lambda bb, ss: (0, 0)),
            pl.BlockSpec((1, d), lambda bb, ss: (0, 0)),
        ] + r_in,
        out_specs=r_out,
        out_shape=_router_out_shapes(b, s, d, ts, n_experts),
        scratch_shapes=[pltpu.VMEM((POOL_HALO + ts, d), F32)],
        compiler_params=pltpu.CompilerParams(
            dimension_semantics=("arbitrary", "arbitrary"), vmem_limit_bytes=VMEM_LIMIT),
    )(x, mod, gain, pw.astype(BF16), pb.reshape(1, d), ps.reshape(1, d), wr_hi, wr_lo, br)


def _gla_kernel(x_ref, mod_ref, gain_ref, wp_ref, wz_ref, wg_ref, bg_ref, ng_ref, wo_ref,
                wr_hi_ref, wr_lo_ref, br_ref,
                x1_ref, h2_ref, ids_ref, wts_ref, hist_ref,
                state_ref, q_s, k_s, g_s, v_s, r_s, o_s, *, n_heads, n_groups, n_experts):
    b, s = pl.program_id(0), pl.program_id(1)
    x = x_ref[0]
    ts, d = x.shape
    dk_all = q_s.shape[1]
    dk = dk_all // n_heads
    dv = d // n_heads
    mod = mod_ref[0]
    h = _norm_mod(x, gain_ref[0:1], mod[1:2], mod[0:1]).astype(BF16)

    @pl.when(s == 0)
    def _():
        state_ref[...] = jnp.zeros(state_ref.shape, F32)

    q_s[...] = _dot(h, wp_ref[:, 0:dk_all]) * (dk ** -0.5)
    k_s[...] = _dot(h, wp_ref[:, dk_all:2 * dk_all])
    v_s[...] = _dot(h, wp_ref[:, 2 * dk_all:2 * dk_all + d]).astype(BF16)
    r = _dot(h, wp_ref[:, 2 * dk_all + d:2 * dk_all + 2 * d])
    r_s[...] = r * (1.0 / (1.0 + jnp.exp(-r)))
    z = _dot(h, wz_ref[...]).astype(BF16)
    u = _dot(z, wg_ref[...]) + bg_ref[...]
    g_s[...] = (jnp.minimum(u, 0.0) - jnp.log(1.0 + jnp.exp(-jnp.abs(u)))) * (1.0 / GLA_GATE_TEMP)

    c = GLA_CHUNK
    ri = lax.broadcasted_iota(jnp.int32, (c, c), 0)
    ci = lax.broadcasted_iota(jnp.int32, (c, c), 1)
    causal = ri >= ci
    tril = jnp.where(causal, 1.0, 0.0).astype(BF16)
    ng = ng_ref[...]

    def chunk(n, carry):
        rows = pl.ds(pl.multiple_of(n * c, c), c)
        g_hi, g_lo = _split_bf16(g_s[rows, :])
        bc = _dot(tril, g_hi) + _dot(tril, g_lo)
        bl = bc[c - 1:c, :]
        q_e = (q_s[rows, :] * jnp.exp(bc)).astype(BF16)
        kk = k_s[rows, :]
        k_e = (kk * jnp.exp(-bc)).astype(BF16)
        k_d = (kk * jnp.exp(bl - bc)).astype(BF16)
        e_l = jnp.exp(bl)
        vv = v_s[rows, :]
        for hd in range(n_heads):
            ks = slice(hd * dk, (hd + 1) * dk)
            vs = slice(hd * dv, (hd + 1) * dv)
            sc = jnp.where(causal, _dot_nt(q_e[:, ks], k_e[:, ks]), 0.0).astype(BF16)
            st = state_ref[hd]
            o = _dot(sc, vv[:, vs]) + _dot_nt(q_e[:, ks], st.astype(BF16))
            state_ref[hd] = st * e_l[:, ks] + _dot_tn(vv[:, vs], k_d[:, ks])
            o = o * lax.rsqrt(jnp.mean(o * o, axis=-1, keepdims=True) + NORM_EPS) * ng
            o_s[rows, vs] = (o * r_s[rows, vs]).astype(BF16)
        return carry

    lax.fori_loop(0, ts // c, chunk, 0, unroll=4)
    x1 = x + mod[2:3] * _dot(o_s[...], wo_ref[...])
    x1_ref[0] = x1
    _router_epilogue(x1, mod, gain_ref[1:2], wr_hi_ref, wr_lo_ref, br_ref, n_groups, n_experts,
                     h2_ref, ids_ref, wts_ref, hist_ref, jnp.logical_and(b == 0, s == 0))


def _gla_layer(x, mod, gain, w_in, w_gate, b_gate, norm_g, w_out, wr_hi, wr_lo, br, n_groups, n_experts):
    b, s, d = x.shape
    ts = min(SEQ_TILE_GLA, s)
    nst = s // ts
    rank, dk_all = w_gate.shape
    dv = norm_g.shape[0]
    n_heads = d // dv
    n_proj = 2 * dk_all + 2 * d
    wp = w_in[:, :n_proj].astype(BF16)
    wz = jnp.pad(w_in[:, n_proj:], ((0, 0), (0, LANES - rank))).astype(BF16)
    wg = jnp.pad(w_gate, ((0, LANES - rank), (0, 0))).astype(BF16)
    r_in, r_out = _router_specs(d, ts, nst, n_experts)
    const = lambda bb, ss: (0, 0)
    return pl.pallas_call(
        functools.partial(_gla_kernel, n_heads=n_heads, n_groups=n_groups, n_experts=n_experts),
        grid=(b, nst),
        in_specs=[
            pl.BlockSpec((1, ts, d), lambda bb, ss: (bb, ss, 0)),
            pl.BlockSpec((1, 6, d), lambda bb, ss: (bb, 0, 0)),
            pl.BlockSpec((2, d), const),
            pl.BlockSpec((d, n_proj), const),
            pl.BlockSpec((d, LANES), const),
            pl.BlockSpec((LANES, dk_all), const),
            pl.BlockSpec((1, dk_all), const),
            pl.BlockSpec((1, dv), const),
            pl.BlockSpec((d, d), const),
        ] + r_in,
        out_specs=r_out,
        out_shape=_router_out_shapes(b, s, d, ts, n_experts),
        scratch_shapes=[
            pltpu.VMEM((n_heads, dv, dk_all // n_heads), F32),
            pltpu.VMEM((ts, dk_all), F32),
            pltpu.VMEM((ts, dk_all), F32),
            pltpu.VMEM((ts, dk_all), F32),
            pltpu.VMEM((ts, d), BF16),
            pltpu.VMEM((ts, d), F32),
            pltpu.VMEM((ts, d), BF16),
        ],
        compiler_params=pltpu.CompilerParams(
            dimension_semantics=("arbitrary", "arbitrary"), vmem_limit_bytes=VMEM_LIMIT),
    )(x, mod, gain, wp, wz, wg, b_gate.reshape(1, dk_all), norm_g.reshape(1, dv), w_out.astype(BF16),
      wr_hi, wr_lo, br)


def _pos_kernel(ids_ref, off_ref, pos_ref, run_ref):
    i = pl.program_id(0)
    ids = ids_ref[...]
    tp = ids.shape[1]
    n_e = off_ref.shape[0]

    @pl.when(i == 0)
    def _():
        run_ref[...] = jnp.zeros(run_ref.shape, F32)

    ei = lax.broadcasted_iota(jnp.int32, (n_e, tp), 0)
    oh0 = ei == ids[0:1]
    oh1 = ei == ids[1:2]
    tri = jnp.where(lax.broadcasted_iota(jnp.int32, (tp, tp), 0) <= lax.broadcasted_iota(jnp.int32, (tp, tp), 1),
                    1.0, 0.0).astype(BF16)
    inc0 = _dot(jnp.where(oh0, 1.0, 0.0).astype(BF16), tri)
    inc1 = _dot(jnp.where(oh1, 1.0, 0.0).astype(BF16), tri)
    tot0 = inc0[:, tp - 1:tp]
    tot1 = inc1[:, tp - 1:tp]
    base = off_ref[...] + run_ref[...]
    p0 = jnp.sum(jnp.where(oh0, base + inc0 - 1.0, 0.0), axis=0, keepdims=True)
    p1 = jnp.sum(jnp.where(oh1, base + tot0 + inc1 - 1.0, 0.0), axis=0, keepdims=True)
    pos_ref[0] = jnp.concatenate([p0, p1], axis=0).astype(jnp.int32)
    run_ref[...] += tot0 + tot1


def _positions(ids, offsets):
    t = ids.shape[1]
    tp = min(ROW_TILE, t)
    n_e = offsets.shape[0]
    return pl.pallas_call(
        _pos_kernel,
        grid=(t // tp,),
        in_specs=[pl.BlockSpec((MOE_TOP_E, tp), lambda i: (0, i)),
                  pl.BlockSpec((n_e, 1), lambda i: (0, 0))],
        out_specs=pl.BlockSpec((1, MOE_TOP_E, tp), lambda i: (i, 0, 0)),
        out_shape=jax.ShapeDtypeStruct((t // tp, MOE_TOP_E, tp), jnp.int32),
        scratch_shapes=[pltpu.VMEM((n_e, 1), F32)],
        compiler_params=pltpu.CompilerParams(dimension_semantics=("arbitrary",)),
    )(ids, offsets)


def _tile_rows(row):
    return pl.ds(pl.multiple_of(row * SUBLANES, SUBLANES), SUBLANES)


def _for_each_token(tp, fn):
    def group(j, carry):
        r0 = pl.multiple_of(j * SUBLANES, SUBLANES)
        for u in range(SUBLANES):
            fn(r0 + u)
        return carry
    lax.fori_loop(0, tp // SUBLANES, group, 0)


def _dispatch_kernel(pos_ref, h_ref, xs_ref, sem):
    tp = pos_ref.shape[2]

    def issue(r):
        for k in range(MOE_TOP_E):
            pltpu.make_async_copy(h_ref.at[_tile_rows(r)], xs_ref.at[_tile_rows(pos_ref[0, k, r])],
                                  sem).start(priority=k)

    _for_each_token(tp, issue)
    for k in range(MOE_TOP_E):
        pltpu.make_async_copy(h_ref, xs_ref.at[pl.ds(0, tp * SUBLANES)], sem).wait()


def _dispatch(pos, h2, n_rows):
    nt, _, tp = pos.shape
    return pl.pallas_call(
        _dispatch_kernel,
        grid=(nt,),
        in_specs=[pl.BlockSpec((1, MOE_TOP_E, tp), lambda i: (i, 0, 0), memory_space=pltpu.SMEM),
                  pl.BlockSpec((tp * SUBLANES, LANES), lambda i: (i, 0))],
        out_specs=pl.BlockSpec(memory_space=pl.ANY),
        out_shape=jax.ShapeDtypeStruct((n_rows * SUBLANES, LANES), F32),
        scratch_shapes=[pltpu.SemaphoreType.DMA(())],
        compiler_params=pltpu.CompilerParams(dimension_semantics=("arbitrary",), has_side_effects=True),
    )(pos, h2)


def _gmm_kernel(it_ref, ie_ref, lo_ref, hi_ref, n_ref, xs_ref, wi_ref, wo_ref, ys_ref, wi_s, wo_s):
    w = pl.program_id(0)
    f = wo_ref.shape[2]
    tm = xs_ref.shape[0] // SUBLANES

    @pl.when(w < n_ref[0])
    def _():
        prev = jnp.maximum(w - 1, 0)

        @pl.when(jnp.logical_or(w == 0, ie_ref[w] != ie_ref[prev]))
        def _():
            wi_s[...] = wi_ref[0, 0].astype(BF16)
            wo_s[...] = wo_ref[0, 0].astype(BF16)

        x = jnp.concatenate([_load_token_tiles(xs_ref, tm, s).astype(BF16) for s in range(SUBLANES)], axis=-1)
        gu = _dot(x, wi_s[...])
        gate, up = gu[:, :f], gu[:, f:]
        a = gate * (1.0 / (1.0 + jnp.exp(-gate))) * up
        y = _dot(a.astype(BF16), wo_s[...])
        row = lax.broadcasted_iota(jnp.int32, (tm, 1), 0)
        mine = jnp.logical_and(row >= lo_ref[w], row < hi_ref[w])
        first_visit = jnp.logical_or(w == 0, it_ref[w] != it_ref[prev])

        @pl.when(first_visit)
        def _():
            _store_token_tiles(ys_ref, jnp.where(mine, y, 0.0))

        @pl.when(jnp.logical_not(first_visit))
        def _():
            for s in range(SUBLANES):
                old = _load_token_tiles(ys_ref, tm, s)
                ys_ref[pl.ds(s, tm, stride=SUBLANES), :] = jnp.where(mine, y[:, s * LANES:(s + 1) * LANES], old)


def _grouped_mlp(plan, xs, w_in, w_out, layer):
    item_tile, item_expert, item_lo, item_hi, n_items = plan
    d, f2 = w_in.shape[-2:]
    f = f2 // 2
    tm = GMM_TILE
    rows = tm * SUBLANES
    return pl.pallas_call(
        _gmm_kernel,
        grid_spec=pltpu.PrefetchScalarGridSpec(
            num_scalar_prefetch=5,
            grid=(item_tile.shape[0],),
            in_specs=[
                pl.BlockSpec((rows, LANES), lambda w, it, ie, lo, hi, n: (it[w], 0)),
                pl.BlockSpec((1, 1, d, f2), lambda w, it, ie, lo, hi, n: (layer, ie[w], 0, 0)),
                pl.BlockSpec((1, 1, f, d), lambda w, it, ie, lo, hi, n: (layer, ie[w], 0, 0)),
            ],
            out_specs=pl.BlockSpec((rows, LANES), lambda w, it, ie, lo, hi, n: (it[w], 0)),
            scratch_shapes=[pltpu.VMEM((d, f2), BF16), pltpu.VMEM((f, d), BF16)],
        ),
        out_shape=jax.ShapeDtypeStruct(xs.shape, F32),
        compiler_params=pltpu.CompilerParams(
            dimension_semantics=("arbitrary",), vmem_limit_bytes=VMEM_LIMIT),
    )(item_tile, item_expert, item_lo, item_hi, n_items, xs, w_in, w_out)


def _combine_kernel(pos_ref, posn_ref, x1_ref, wt_ref, mod_ref, fg_ref, ys_ref, o_ref, buf, sems, *, final):
    i = pl.program_id(0)
    n = pl.num_programs(0)
    tp = pos_ref.shape[2]

    def issue(idx_ref, slot):
        def one(r):
            for k in range(MOE_TOP_E):
                pltpu.make_async_copy(ys_ref.at[_tile_rows(idx_ref[0, k, r])], buf.at[slot, k, _tile_rows(r)],
                                      sems.at[slot]).start(priority=k)
        _for_each_token(tp, one)

    slot = lax.rem(i, 2)

    @pl.when(i == 0)
    def _():
        issue(pos_ref, 0)

    @pl.when(i + 1 < n)
    def _():
        issue(posn_ref, 1 - slot)

    for k in range(MOE_TOP_E):
        pltpu.make_async_copy(ys_ref.at[pl.ds(0, tp * SUBLANES)], buf.at[slot, k], sems.at[slot]).wait()
    wt = wt_ref[...]
    gate2 = mod_ref[0, 5:6]
    chunks = []
    for s in range(SUBLANES):
        rows = pl.ds(s, tp, stride=SUBLANES)
        cols = slice(s * LANES, (s + 1) * LANES)
        y = wt[:, 0:1] * buf[slot, 0, rows, :] + wt[:, 1:2] * buf[slot, 1, rows, :]
        chunks.append(x1_ref[:, cols] + gate2[:, cols] * y)
    x = jnp.concatenate(chunks, axis=-1)
    if final:
        x = x * lax.rsqrt(jnp.mean(x * x, axis=-1, keepdims=True) + NORM_EPS) * fg_ref[...]
    o_ref[...] = x


def _combine(pos, x1, wts_t, mod, final_g, ys, seq_len, final):
    nt, _, tp = pos.shape
    t, d = x1.shape
    per_seq = seq_len // tp
    return pl.pallas_call(
        functools.partial(_combine_kernel, final=final),
        grid=(nt,),
        in_specs=[
            pl.BlockSpec((1, MOE_TOP_E, tp), lambda i: (i, 0, 0), memory_space=pltpu.SMEM),
            pl.BlockSpec((1, MOE_TOP_E, tp), lambda i: (jnp.minimum(i + 1, nt - 1), 0, 0),
                         memory_space=pltpu.SMEM),
            pl.BlockSpec((tp, d), lambda i: (i, 0)),
            pl.BlockSpec((tp, MOE_TOP_E), lambda i: (i, 0)),
            pl.BlockSpec((1, 6, d), lambda i: (i // per_seq, 0, 0)),
            pl.BlockSpec((1, d), lambda i: (0, 0)),
            pl.BlockSpec(memory_space=pl.ANY),
        ],
        out_specs=pl.BlockSpec((tp, d), lambda i: (i, 0)),
        out_shape=jax.ShapeDtypeStruct((t, d), F32),
        scratch_shapes=[pltpu.VMEM((2, MOE_TOP_E, tp * SUBLANES, LANES), F32), pltpu.SemaphoreType.DMA((2,))],
        compiler_params=pltpu.CompilerParams(
            dimension_semantics=("arbitrary",), vmem_limit_bytes=VMEM_LIMIT),
    )(pos, pos, x1, wts_t, mod, final_g, ys)


def _router_weights(w_group, b_group, w_expert, b_expert):
    d, g = w_group.shape
    e = w_expert.shape[1]
    w = jnp.concatenate([w_group, w_expert], axis=1).T
    w = jnp.pad(w, ((0, ROUTER_ROWS - g - e), (0, 0)))
    hi, lo = _split_bf16(w)
    bias = jnp.pad(jnp.concatenate([b_group, b_expert]), (0, ROUTER_ROWS - g - e)).reshape(ROUTER_ROWS, 1)
    return hi, lo, bias


def _tile_plan(hist, n_rows):
    n_e = hist.shape[0]
    tm = GMM_TILE
    counts = jnp.sum(hist, axis=1).astype(jnp.int32)
    ends = jnp.cumsum(counts)
    starts = ends - counts
    first_tile = starts // tm
    items_per = jnp.where(counts > 0, (ends - 1) // tm - first_tile + 1, 0)
    item_ends = jnp.cumsum(items_per)
    n_items = item_ends[-1:]
    max_items = n_rows // tm + n_e - 1
    w = jnp.minimum(jnp.arange(max_items, dtype=jnp.int32), n_items[0] - 1)
    item_expert = jnp.sum(w[:, None] >= item_ends[None, :], axis=1).astype(jnp.int32)
    item_tile = first_tile[item_expert] + w - (item_ends - items_per)[item_expert]
    item_lo = jnp.maximum(starts[item_expert] - item_tile * tm, 0)
    item_hi = jnp.minimum(ends[item_expert] - item_tile * tm, tm)
    offsets = starts.astype(F32).reshape(n_e, 1)
    return offsets, (item_tile.astype(jnp.int32), item_expert, item_lo.astype(jnp.int32),
                     item_hi.astype(jnp.int32), n_items.astype(jnp.int32))


def kernel(x, c, norm_gain, w_mod, b_mod, pool_w, pool_b, pool_scale, gla_w_in, gla_w_gate, gla_b_gate, gla_norm_g, gla_w_out, moe_w_group, moe_b_group, moe_w_expert, moe_b_expert, moe_w_in, moe_w_out, final_norm_g):
    b, s, d = x.shape
    depth = w_mod.shape[0]
    t = b * s
    n_groups = moe_w_group.shape[-1]
    n_experts = moe_w_expert.shape[-1]
    n_rows = MOE_TOP_E * t
    mod_all = _modulation(c, w_mod, b_mod).reshape(depth, b, 6, d)
    fg = final_norm_g.reshape(1, d)
    for i in range(depth):
        mod = mod_all[i]
        wr_hi, wr_lo, br = _router_weights(moe_w_group[i], moe_b_group[i], moe_w_expert[i], moe_b_expert[i])
        j = i // 2
        if i % 2 == 0:
            x1, h2, ids, wts, hist = _pool_layer(x, mod, norm_gain[i], pool_w[j], pool_b[j], pool_scale[j],
                                                 wr_hi, wr_lo, br, n_groups, n_experts)
        else:
            x1, h2, ids, wts, hist = _gla_layer(x, mod, norm_gain[i], gla_w_in[j], gla_w_gate[j], gla_b_gate[j],
                                                gla_norm_g[j], gla_w_out[j], wr_hi, wr_lo, br, n_groups, n_experts)
        offsets, plan = _tile_plan(hist, n_rows)
        pos = _positions(ids, offsets)
        xs = _dispatch(pos, h2, n_rows)
        ys = _grouped_mlp(plan, xs, moe_w_in, moe_w_out, i)
        x = _combine(pos, x1.reshape(t, d), wts.T, mod, fg, ys, s, final=(i == depth - 1)).reshape(b, s, d)
    return x
```

```python
import functools

import jax
import jax.numpy as jnp
from jax import lax
from jax.experimental import pallas as pl
from jax.experimental.pallas import tpu as pltpu

F32 = jnp.float32
BF16 = jnp.bfloat16

NORM_EPS = 1e-6
POOL_WINDOWS = (2, 4, 8, 16)
POOL_HALO = 16
GLA_CHUNK = 64
GLA_GATE_TEMP = 16.0
MOE_TOP_E = 2
LANES = 128
SUBLANES = 8
ROUTER_ROWS = 128
VMEM_LIMIT = 56 * 1024 * 1024

SEQ_TILE_POOL = 512
SEQ_TILE_GLA = 512
GMM_TILE = 256
ROW_TILE = 512


def _dot(a, b):
    return jnp.dot(a, b, preferred_element_type=F32)


def _dot_nt(a, b):
    return lax.dot_general(a, b, (((1,), (1,)), ((), ())), preferred_element_type=F32)


def _dot_tn(a, b):
    return lax.dot_general(a, b, (((0,), (0,)), ((), ())), preferred_element_type=F32)


def _split_bf16(x):
    hi = x.astype(BF16)
    lo = (x - hi.astype(F32)).astype(BF16)
    return hi, lo


def _store_token_tiles(ref, x):
    n = x.shape[0]
    for s in range(SUBLANES):
        ref[pl.ds(s, n, stride=SUBLANES), :] = x[:, s * LANES:(s + 1) * LANES]


def _load_token_tiles(ref, n, s):
    return ref[pl.ds(s, n, stride=SUBLANES), :]


def _norm_mod(x, gain, scale, shift):
    ms = jnp.mean(x * x, axis=-1, keepdims=True)
    return x * lax.rsqrt(ms + NORM_EPS) * gain * (1.0 + scale) + shift


def _mod_kernel(c_ref, w_ref, b_ref, o_ref):
    c = c_ref[...]
    sc = (c * (1.0 / (1.0 + jnp.exp(-c)))).astype(BF16)
    o_ref[0] = _dot(sc, w_ref[0].astype(BF16)) + b_ref[0]


def _modulation(c, w_mod, b_mod):
    depth, d, n = w_mod.shape
    b = c.shape[0]
    tn = n // 4
    return pl.pallas_call(
        _mod_kernel,
        grid=(depth, n // tn),
        in_specs=[
            pl.BlockSpec((b, d), lambda i, j: (0, 0)),
            pl.BlockSpec((1, d, tn), lambda i, j: (i, 0, j)),
            pl.BlockSpec((1, 1, tn), lambda i, j: (i, 0, j)),
        ],
        out_specs=pl.BlockSpec((1, b, tn), lambda i, j: (i, 0, j)),
        out_shape=jax.ShapeDtypeStruct((depth, b, n), F32),
        compiler_params=pltpu.CompilerParams(vmem_limit_bytes=VMEM_LIMIT),
    )(c, w_mod, b_mod.reshape(depth, 1, n))


def _router_epilogue(x1, mod, gain2, wr_hi_ref, wr_lo_ref, br_ref, n_groups, n_experts,
                     h2_ref, ids_ref, wts_ref, hist_ref, first_step):
    h2 = _norm_mod(x1, gain2, mod[4:5], mod[3:4])
    _store_token_tiles(h2_ref, h2)
    hh, hl = _split_bf16(h2)
    wh = wr_hi_ref[...]
    lt = _dot_nt(wh, hh) + _dot_nt(wh, hl) + _dot_nt(wr_lo_ref[...], hh) + br_ref[...]
    ts = lt.shape[1]
    per = n_experts // n_groups
    lg = lt[0:n_groups]
    mg = jnp.max(lg, axis=0, keepdims=True)
    p_g = 1.0 / jnp.sum(jnp.exp(lg - mg), axis=0, keepdims=True)
    gi = lax.broadcasted_iota(jnp.int32, lg.shape, 0)
    g_idx = jnp.min(jnp.where(lg == mg, gi, n_groups), axis=0, keepdims=True)
    sel = jnp.zeros((per, ts), F32)
    for g in range(n_groups):
        sel = jnp.where(g_idx == g, lt[n_groups + g * per:n_groups + (g + 1) * per], sel)
    ei = lax.broadcasted_iota(jnp.int32, sel.shape, 0)
    m1 = jnp.max(sel, axis=0, keepdims=True)
    i1 = jnp.min(jnp.where(sel == m1, ei, per), axis=0, keepdims=True)
    rest = jnp.where(ei == i1, -jnp.inf, sel)
    m2 = jnp.max(rest, axis=0, keepdims=True)
    i2 = jnp.min(jnp.where(rest == m2, ei, per), axis=0, keepdims=True)
    e21 = jnp.exp(m2 - m1)
    w1 = p_g / (1.0 + e21)
    w2 = p_g * e21 / (1.0 + e21)
    id1 = g_idx * per + i1
    id2 = g_idx * per + i2
    ids_ref[...] = jnp.concatenate([id1, id2], axis=0)
    wts_ref[...] = jnp.concatenate([w1, w2], axis=0)
    xi = lax.broadcasted_iota(jnp.int32, (n_experts, ts), 0)
    cnt = (xi == id1).astype(F32) + (xi == id2).astype(F32)

    @pl.when(first_step)
    def _():
        hist_ref[...] = cnt

    @pl.when(jnp.logical_not(first_step))
    def _():
        hist_ref[...] += cnt


def _router_specs(d, ts, n_seq_tiles, n_experts):
    in_specs = [
        pl.BlockSpec((ROUTER_ROWS, d), lambda b, s: (0, 0)),
        pl.BlockSpec((ROUTER_ROWS, d), lambda b, s: (0, 0)),
        pl.BlockSpec((ROUTER_ROWS, 1), lambda b, s: (0, 0)),
    ]
    out_specs = [
        pl.BlockSpec((1, ts, d), lambda b, s: (b, s, 0)),
        pl.BlockSpec((ts * SUBLANES, LANES), lambda b, s: (b * n_seq_tiles + s, 0)),
        pl.BlockSpec((MOE_TOP_E, ts), lambda b, s: (0, b * n_seq_tiles + s)),
        pl.BlockSpec((MOE_TOP_E, ts), lambda b, s: (0, b * n_seq_tiles + s)),
        pl.BlockSpec((n_experts, ts), lambda b, s: (0, 0)),
    ]
    return in_specs, out_specs


def _router_out_shapes(b, s, d, ts, n_experts):
    assert d == SUBLANES * LANES, "token-per-tile row layout needs d_model == one (8,128) tile"
    return [
        jax.ShapeDtypeStruct((b, s, d), F32),
        jax.ShapeDtypeStruct((b * s * SUBLANES, LANES), F32),
        jax.ShapeDtypeStruct((MOE_TOP_E, b * s), jnp.int32),
        jax.ShapeDtypeStruct((MOE_TOP_E, b * s), F32),
        jax.ShapeDtypeStruct((n_experts, ts), F32),
    ]


def _pool_kernel(x_ref, mod_ref, gain_ref, pw_ref, pb_ref, ps_ref, wr_hi_ref, wr_lo_ref, br_ref,
                 x1_ref, h2_ref, ids_ref, wts_ref, hist_ref, hbuf_ref, *, n_groups, n_experts):
    b, s = pl.program_id(0), pl.program_id(1)
    x = x_ref[0]
    ts, d = x.shape
    mod = mod_ref[0]
    h = _norm_mod(x, gain_ref[0:1], mod[1:2], mod[0:1])

    @pl.when(s == 0)
    def _():
        hbuf_ref[0:POOL_HALO] = jnp.zeros((POOL_HALO, d), F32)

    hbuf_ref[POOL_HALO:POOL_HALO + ts] = h
    pos = (s * ts + 1 + lax.broadcasted_iota(jnp.int32, (ts, 1), 0)).astype(F32)
    cg = d // len(POOL_WINDOWS)
    ys = []
    for gi, win in enumerate(POOL_WINDOWS):
        cols = slice(gi * cg, (gi + 1) * cg)
        acc = h[:, cols]
        for j in range(1, win):
            acc = acc + hbuf_ref[POOL_HALO - j:POOL_HALO - j + ts, cols]
        dgi = acc / jnp.minimum(pos, float(win)) - h[:, cols]
        ys.append(_dot(dgi.astype(BF16), pw_ref[gi]))
    y = (jnp.concatenate(ys, axis=-1) + pb_ref[...]) * ps_ref[...]
    hbuf_ref[0:POOL_HALO] = hbuf_ref[ts:ts + POOL_HALO]
    x1 = x + mod[2:3] * y
    x1_ref[0] = x1
    _router_epilogue(x1, mod, gain_ref[1:2], wr_hi_ref, wr_lo_ref, br_ref, n_groups, n_experts,
                     h2_ref, ids_ref, wts_ref, hist_ref, jnp.logical_and(b == 0, s == 0))


def _pool_layer(x, mod, gain, pw, pb, ps, wr_hi, wr_lo, br, n_groups, n_experts):
    b, s, d = x.shape
    ts = min(SEQ_TILE_POOL, s)
    nst = s // ts
    g, cg = pw.shape[0], pw.shape[1]
    r_in, r_out = _router_specs(d, ts, nst, n_experts)
    return pl.pallas_call(
        functools.partial(_pool_kernel, n_groups=n_groups, n_experts=n_experts),
        grid=(b, nst),
        in_specs=[
            pl.BlockSpec((1, ts, d), lambda bb, ss: (bb, ss, 0)),
            pl.BlockSpec((1, 6, d), lambda bb, ss: (bb, 0, 0)),
            pl.BlockSpec((2, d), lambda bb, ss: (0, 0)),
            pl.BlockSpec((g, cg, cg), lambda bb, ss: (0, 0, 0)),
            pl.BlockSpec((1, d), lambda bb, ss: (0, 0)),
            pl.BlockSpec((1, d), lambda bb, ss: (0, 0)),
        ] + r_in,
        out_specs=r_out,
        out_shape=_router_out_shapes(b, s, d, ts, n_experts),
        scratch_shapes=[pltpu.VMEM((POOL_HALO + ts, d), F32)],
        compiler_params=pltpu.CompilerParams(
            dimension_semantics=("arbitrary", "arbitrary"), vmem_limit_bytes=VMEM_LIMIT),
    )(x, mod, gain, pw.astype(BF16), pb.reshape(1, d), ps.reshape(1, d), wr_hi, wr_lo, br)


def _gla_kernel(x_ref, mod_ref, gain_ref, wp_ref, wz_ref, wg_ref, bg_ref, ng_ref, wo_ref,
                wr_hi_ref, wr_lo_ref, br_ref,
                x1_ref, h2_ref, ids_ref, wts_ref, hist_ref,
                state_ref, q_s, k_s, g_s, v_s, r_s, o_s, *, n_heads, n_groups, n_experts):
    b, s = pl.program_id(0), pl.program_id(1)
    x = x_ref[0]
    ts, d = x.shape
    dk_all = q_s.shape[1]
    dk = dk_all // n_heads
    dv = d // n_heads
    mod = mod_ref[0]
    h = _norm_mod(x, gain_ref[0:1], mod[1:2], mod[0:1]).astype(BF16)

    @pl.when(s == 0)
    def _():
        state_ref[...] = jnp.zeros(state_ref.shape, F32)

    q_s[...] = _dot(h, wp_ref[:, 0:dk_all]) * (dk ** -0.5)
    k_s[...] = _dot(h, wp_ref[:, dk_all:2 * dk_all])
    v_s[...] = _dot(h, wp_ref[:, 2 * dk_all:2 * dk_all + d]).astype(BF16)
    r = _dot(h, wp_ref[:, 2 * dk_all + d:2 * dk_all + 2 * d])
    r_s[...] = r * (1.0 / (1.0 + jnp.exp(-r)))
    z = _dot(h, wz_ref[...]).astype(BF16)
    u = _dot(z, wg_ref[...]) + bg_ref[...]
    g_s[...] = (jnp.minimum(u, 0.0) - jnp.log(1.0 + jnp.exp(-jnp.abs(u)))) * (1.0 / GLA_GATE_TEMP)

    c = GLA_CHUNK
    ri = lax.broadcasted_iota(jnp.int32, (c, c), 0)
    ci = lax.broadcasted_iota(jnp.int32, (c, c), 1)
    causal = ri >= ci
    tril = jnp.where(causal, 1.0, 0.0).astype(BF16)
    ng = ng_ref[...]

    def chunk(n, carry):
        rows = pl.ds(pl.multiple_of(n * c, c), c)
        g_hi, g_lo = _split_bf16(g_s[rows, :])
        bc = _dot(tril, g_hi) + _dot(tril, g_lo)
        bl = bc[c - 1:c, :]
        q_e = (q_s[rows, :] * jnp.exp(bc)).astype(BF16)
        kk = k_s[rows, :]
        k_e = (kk * jnp.exp(-bc)).astype(BF16)
        k_d = (kk * jnp.exp(bl - bc)).astype(BF16)
        e_l = jnp.exp(bl)
        vv = v_s[rows, :]
        for hd in range(n_heads):
            ks = slice(hd * dk, (hd + 1) * dk)
            vs = slice(hd * dv, (hd + 1) * dv)
            sc = jnp.where(causal, _dot_nt(q_e[:, ks], k_e[:, ks]), 0.0).astype(BF16)
            st = state_ref[hd]
            o = _dot(sc, vv[:, vs]) + _dot_nt(q_e[:, ks], st.astype(BF16))
            state_ref[hd] = st * e_l[:, ks] + _dot_tn(vv[:, vs], k_d[:, ks])
            o = o * lax.rsqrt(jnp.mean(o * o, axis=-1, keepdims=True) + NORM_EPS) * ng
            o_s[rows, vs] = (o * r_s[rows, vs]).astype(BF16)
        return carry

    lax.fori_loop(0, ts // c, chunk, 0, unroll=4)
    x1 = x + mod[2:3] * _dot(o_s[...], wo_ref[...])
    x1_ref[0] = x1
    _router_epilogue(x1, mod, gain_ref[1:2], wr_hi_ref, wr_lo_ref, br_ref, n_groups, n_experts,
                     h2_ref, ids_ref, wts_ref, hist_ref, jnp.logical_and(b == 0, s == 0))


def _gla_layer(x, mod, gain, w_in, w_gate, b_gate, norm_g, w_out, wr_hi, wr_lo, br, n_groups, n_experts):
    b, s, d = x.shape
    ts = min(SEQ_TILE_GLA, s)
    nst = s // ts
    rank, dk_all = w_gate.shape
    dv = norm_g.shape[0]
    n_heads = d // dv
    n_proj = 2 * dk_all + 2 * d
    wp = w_in[:, :n_proj].astype(BF16)
    wz = jnp.pad(w_in[:, n_proj:], ((0, 0), (0, LANES - rank))).astype(BF16)
    wg = jnp.pad(w_gate, ((0, LANES - rank), (0, 0))).astype(BF16)
    r_in, r_out = _router_specs(d, ts, nst, n_experts)
    const = lambda bb, ss: (0, 0)
    return pl.pallas_call(
        functools.partial(_gla_kernel, n_heads=n_heads, n_groups=n_groups, n_experts=n_experts),
        grid=(b, nst),
        in_specs=[
            pl.BlockSpec((1, ts, d), lambda bb, ss: (bb, ss, 0)),
            pl.BlockSpec((1, 6, d), lambda bb, ss: (bb, 0, 0)),
            pl.BlockSpec((2, d), const),
            pl.BlockSpec((d, n_proj), const),
            pl.BlockSpec((d, LANES), const),
            pl.BlockSpec((LANES, dk_all), const),
            pl.BlockSpec((1, dk_all), const),
            pl.BlockSpec((1, dv), const),
            pl.BlockSpec((d, d), const),
        ] + r_in,
        out_specs=r_out,
        out_shape=_router_out_shapes(b, s, d, ts, n_experts),
        scratch_shapes=[
            pltpu.VMEM((n_heads, dv, dk_all // n_heads), F32),
            pltpu.VMEM((ts, dk_all), F32),
            pltpu.VMEM((ts, dk_all), F32),
            pltpu.VMEM((ts, dk_all), F32),
            pltpu.VMEM((ts, d), BF16),
            pltpu.VMEM((ts, d), F32),
            pltpu.VMEM((ts, d), BF16),
        ],
        compiler_params=pltpu.CompilerParams(
            dimension_semantics=("arbitrary", "arbitrary"), vmem_limit_bytes=VMEM_LIMIT),
    )(x, mod, gain, wp, wz, wg, b_gate.reshape(1, dk_all), norm_g.reshape(1, dv), w_out.astype(BF16),
      wr_hi, wr_lo, br)


def _pos_kernel(ids_ref, off_ref, pos_ref, run_ref):
    i = pl.program_id(0)
    ids = ids_ref[...]
    tp = ids.shape[1]
    n_e = off_ref.shape[0]

    @pl.when(i == 0)
    def _():
        run_ref[...] = jnp.zeros(run_ref.shape, F32)

    ei = lax.broadcasted_iota(jnp.int32, (n_e, tp), 0)
    oh0 = ei == ids[0:1]
    oh1 = ei == ids[1:2]
    tri = jnp.where(lax.broadcasted_iota(jnp.int32, (tp, tp), 0) <= lax.broadcasted_iota(jnp.int32, (tp, tp), 1),
                    1.0, 0.0).astype(BF16)
    inc0 = _dot(jnp.where(oh0, 1.0, 0.0).astype(BF16), tri)
    inc1 = _dot(jnp.where(oh1, 1.0, 0.0).astype(BF16), tri)
    tot0 = inc0[:, tp - 1:tp]
    tot1 = inc1[:, tp - 1:tp]
    base = off_ref[...] + run_ref[...]
    p0 = jnp.sum(jnp.where(oh0, base + inc0 - 1.0, 0.0), axis=0, keepdims=True)
    p1 = jnp.sum(jnp.where(oh1, base + tot0 + inc1 - 1.0, 0.0), axis=0, keepdims=True)
    pos_ref[0] = jnp.concatenate([p0, p1], axis=0).astype(jnp.int32)
    run_ref[...] += tot0 + tot1


def _positions(ids, offsets):
    t = ids.shape[1]
    tp = min(ROW_TILE, t)
    n_e = offsets.shape[0]
    return pl.pallas_call(
        _pos_kernel,
        grid=(t // tp,),
        in_specs=[pl.BlockSpec((MOE_TOP_E, tp), lambda i: (0, i)),
                  pl.BlockSpec((n_e, 1), lambda i: (0, 0))],
        out_specs=pl.BlockSpec((1, MOE_TOP_E, tp), lambda i: (i, 0, 0)),
        out_shape=jax.ShapeDtypeStruct((t // tp, MOE_TOP_E, tp), jnp.int32),
        scratch_shapes=[pltpu.VMEM((n_e, 1), F32)],
        compiler_params=pltpu.CompilerParams(dimension_semantics=("arbitrary",)),
    )(ids, offsets)


def _tile_rows(row):
    return pl.ds(pl.multiple_of(row * SUBLANES, SUBLANES), SUBLANES)


def _for_each_token(tp, fn):
    def group(j, carry):
        r0 = pl.multiple_of(j * SUBLANES, SUBLANES)
        for u in range(SUBLANES):
            fn(r0 + u)
        return carry
    lax.fori_loop(0, tp // SUBLANES, group, 0)


def _dispatch_kernel(pos_ref, h_ref, xs_ref, sem):
    tp = pos_ref.shape[2]

    def issue(r):
        for k in range(MOE_TOP_E):
            pltpu.make_async_copy(h_ref.at[_tile_rows(r)], xs_ref.at[_tile_rows(pos_ref[0, k, r])],
                                  sem).start(priority=k)

    _for_each_token(tp, issue)
    for k in range(MOE_TOP_E):
        pltpu.make_async_copy(h_ref, xs_ref.at[pl.ds(0, tp * SUBLANES)], sem).wait()


def _dispatch(pos, h2, n_rows):
    nt, _, tp = pos.shape
    return pl.pallas_call(
        _dispatch_kernel,
        grid=(nt,),
        in_specs=[pl.BlockSpec((1, MOE_TOP_E, tp), lambda i: (i, 0, 0), memory_space=pltpu.SMEM),
                  pl.BlockSpec((tp * SUBLANES, LANES), lambda i: (i, 0))],
        out_specs=pl.BlockSpec(memory_space=pl.ANY),
        out_shape=jax.ShapeDtypeStruct((n_rows * SUBLANES, LANES), F32),
        scratch_shapes=[pltpu.SemaphoreType.DMA(())],
        compiler_params=pltpu.CompilerParams(dimension_semantics=("arbitrary",), has_side_effects=True),
    )(pos, h2)


def _gmm_kernel(it_ref, ie_ref, lo_ref, hi_ref, n_ref, xs_ref, wi_ref, wo_ref, ys_ref, wi_s, wo_s, a_s, acc_s):
    w = pl.program_id(0)
    f = wo_ref.shape[2]
    tm = xs_ref.shape[0] // SUBLANES

    n = n_ref[0]

    @pl.when(w <= n)
    def _():
        ja = jnp.minimum(w, n - 1)
        jb = jnp.maximum(w - 1, 0)
        slot_a = lax.rem(w, 2)

        @pl.when(jnp.logical_or(w == 0, ie_ref[ja] != ie_ref[jb]))
        def _():
            wi_s[...] = wi_ref[0, 0].astype(BF16)

        @pl.when(jnp.logical_or(w <= 1, ie_ref[jb] != ie_ref[jnp.maximum(w - 2, 0)]))
        def _():
            wo_s[...] = wo_ref[0, 0].astype(BF16)

        @pl.when(w == 0)
        def _():
            a_s[1] = jnp.zeros(a_s.shape[1:], BF16)
            acc_s[...] = jnp.zeros(acc_s.shape, F32)

        y = _dot(a_s[1 - slot_a], wo_s[...])
        row = lax.broadcasted_iota(jnp.int32, (tm, 1), 0)
        mine = jnp.logical_and(jnp.logical_and(row >= lo_ref[jb], row < hi_ref[jb]), w >= 1)
        merged = jnp.where(mine, y, acc_s[...])
        acc_s[...] = merged
        _store_token_tiles(ys_ref, merged)

        x = jnp.concatenate([_load_token_tiles(xs_ref, tm, s).astype(BF16) for s in range(SUBLANES)], axis=-1)
        gu = _dot(x, wi_s[...])
        gate, up = gu[:, :f], gu[:, f:]
        a_s[slot_a] = (gate * (1.0 / (1.0 + jnp.exp(-gate))) * up).astype(BF16)


def _grouped_mlp(plan, xs, w_in, w_out, layer):
    item_tile, item_expert, item_lo, item_hi, n_items = plan
    d, f2 = w_in.shape[-2:]
    f = f2 // 2
    tm = GMM_TILE
    rows = tm * SUBLANES

    def first(w, n):
        return jnp.minimum(w, n[0] - 1)

    def second(w, n):
        return jnp.minimum(jnp.maximum(w - 1, 0), n[0] - 1)

    return pl.pallas_call(
        _gmm_kernel,
        grid_spec=pltpu.PrefetchScalarGridSpec(
            num_scalar_prefetch=5,
            grid=(item_tile.shape[0] + 1,),
            in_specs=[
                pl.BlockSpec((rows, LANES), lambda w, it, ie, lo, hi, n: (it[first(w, n)], 0)),
                pl.BlockSpec((1, 1, d, f2), lambda w, it, ie, lo, hi, n: (layer, ie[first(w, n)], 0, 0)),
                pl.BlockSpec((1, 1, f, d), lambda w, it, ie, lo, hi, n: (layer, ie[second(w, n)], 0, 0)),
            ],
            out_specs=pl.BlockSpec((rows, LANES), lambda w, it, ie, lo, hi, n: (it[second(w, n)], 0)),
            scratch_shapes=[pltpu.VMEM((d, f2), BF16), pltpu.VMEM((f, d), BF16), pltpu.VMEM((2, tm, f), BF16),
                            pltpu.VMEM((tm, d), F32)],
        ),
        out_shape=jax.ShapeDtypeStruct(xs.shape, F32),
        compiler_params=pltpu.CompilerParams(
            dimension_semantics=("arbitrary",), vmem_limit_bytes=VMEM_LIMIT),
    )(item_tile, item_expert, item_lo, item_hi, n_items, xs, w_in, w_out)


def _combine_kernel(pos_ref, posn_ref, x1_ref, wt_ref, mod_ref, fg_ref, ys_ref, o_ref, buf, sems, *, final):
    i = pl.program_id(0)
    n = pl.num_programs(0)
    tp = pos_ref.shape[2]

    def issue(idx_ref, slot):
        def one(r):
            for k in range(MOE_TOP_E):
                pltpu.make_async_copy(ys_ref.at[_tile_rows(idx_ref[0, k, r])], buf.at[slot, k, _tile_rows(r)],
                                      sems.at[slot]).start(priority=k)
        _for_each_token(tp, one)

    slot = lax.rem(i, 2)

    @pl.when(i == 0)
    def _():
        issue(pos_ref, 0)

    @pl.when(i + 1 < n)
    def _():
        issue(posn_ref, 1 - slot)

    for k in range(MOE_TOP_E):
        pltpu.make_async_copy(ys_ref.at[pl.ds(0, tp * SUBLANES)], buf.at[slot, k], sems.at[slot]).wait()
    wt = wt_ref[...]
    gate2 = mod_ref[0, 5:6]
    chunks = []
    for s in range(SUBLANES):
        rows = pl.ds(s, tp, stride=SUBLANES)
        cols = slice(s * LANES, (s + 1) * LANES)
        y = wt[:, 0:1] * buf[slot, 0, rows, :] + wt[:, 1:2] * buf[slot, 1, rows, :]
        chunks.append(x1_ref[:, cols] + gate2[:, cols] * y)
    x = jnp.concatenate(chunks, axis=-1)
    if final:
        x = x * lax.rsqrt(jnp.mean(x * x, axis=-1, keepdims=True) + NORM_EPS) * fg_ref[...]
    o_ref[...] = x


def _combine(pos, x1, wts_t, mod, final_g, ys, seq_len, final):
    nt, _, tp = pos.shape
    t, d = x1.shape
    per_seq = seq_len // tp
    return pl.pallas_call(
        functools.partial(_combine_kernel, final=final),
        grid=(nt,),
        in_specs=[
            pl.BlockSpec((1, MOE_TOP_E, tp), lambda i: (i, 0, 0), memory_space=pltpu.SMEM),
            pl.BlockSpec((1, MOE_TOP_E, tp), lambda i: (jnp.minimum(i + 1, nt - 1), 0, 0),
                         memory_space=pltpu.SMEM),
            pl.BlockSpec((tp, d), lambda i: (i, 0)),
            pl.BlockSpec((tp, MOE_TOP_E), lambda i: (i, 0)),
            pl.BlockSpec((1, 6, d), lambda i: (i // per_seq, 0, 0)),
            pl.BlockSpec((1, d), lambda i: (0, 0)),
            pl.BlockSpec(memory_space=pl.ANY),
        ],
        out_specs=pl.BlockSpec((tp, d), lambda i: (i, 0)),
        out_shape=jax.ShapeDtypeStruct((t, d), F32),
        scratch_shapes=[pltpu.VMEM((2, MOE_TOP_E, tp * SUBLANES, LANES), F32), pltpu.SemaphoreType.DMA((2,))],
        compiler_params=pltpu.CompilerParams(
            dimension_semantics=("arbitrary",), vmem_limit_bytes=VMEM_LIMIT),
    )(pos, pos, x1, wts_t, mod, final_g, ys)


def _router_weights(w_group, b_group, w_expert, b_expert):
    d, g = w_group.shape
    e = w_expert.shape[1]
    w = jnp.concatenate([w_group, w_expert], axis=1).T
    w = jnp.pad(w, ((0, ROUTER_ROWS - g - e), (0, 0)))
    hi, lo = _split_bf16(w)
    bias = jnp.pad(jnp.concatenate([b_group, b_expert]), (0, ROUTER_ROWS - g - e)).reshape(ROUTER_ROWS, 1)
    return hi, lo, bias


def _tile_plan(hist, n_rows):
    n_e = hist.shape[0]
    tm = GMM_TILE
    counts = jnp.sum(hist, axis=1).astype(jnp.int32)
    ends = jnp.cumsum(counts)
    starts = ends - counts
    first_tile = starts // tm
    items_per = jnp.where(counts > 0, (ends - 1) // tm - first_tile + 1, 0)
    item_ends = jnp.cumsum(items_per)
    n_items = item_ends[-1:]
    max_items = n_rows // tm + n_e - 1
    w = jnp.minimum(jnp.arange(max_items, dtype=jnp.int32), n_items[0] - 1)
    item_expert = jnp.sum(w[:, None] >= item_ends[None, :], axis=1).astype(jnp.int32)
    item_tile = first_tile[item_expert] + w - (item_ends - items_per)[item_expert]
    item_lo = jnp.maximum(starts[item_expert] - item_tile * tm, 0)
    item_hi = jnp.minimum(ends[item_expert] - item_tile * tm, tm)
    offsets = starts.astype(F32).reshape(n_e, 1)
    return offsets, (item_tile.astype(jnp.int32), item_expert, item_lo.astype(jnp.int32),
                     item_hi.astype(jnp.int32), n_items.astype(jnp.int32))


def kernel(x, c, norm_gain, w_mod, b_mod, pool_w, pool_b, pool_scale, gla_w_in, gla_w_gate, gla_b_gate, gla_norm_g, gla_w_out, moe_w_group, moe_b_group, moe_w_expert, moe_b_expert, moe_w_in, moe_w_out, final_norm_g):
    b, s, d = x.shape
    depth = w_mod.shape[0]
    t = b * s
    n_groups = moe_w_group.shape[-1]
    n_experts = moe_w_expert.shape[-1]
    n_rows = MOE_TOP_E * t
    mod_all = _modulation(c, w_mod, b_mod).reshape(depth, b, 6, d)
    fg = final_norm_g.reshape(1, d)
    for i in range(depth):
        mod = mod_all[i]
        wr_hi, wr_lo, br = _router_weights(moe_w_group[i], moe_b_group[i], moe_w_expert[i], moe_b_expert[i])
        j = i // 2
        if i % 2 == 0:
            x1, h2, ids, wts, hist = _pool_layer(x, mod, norm_gain[i], pool_w[j], pool_b[j], pool_scale[j],
                                                 wr_hi, wr_lo, br, n_groups, n_experts)
        else:
            x1, h2, ids, wts, hist = _gla_layer(x, mod, norm_gain[i], gla_w_in[j], gla_w_gate[j], gla_b_gate[j],
                                                gla_norm_g[j], gla_w_out[j], wr_hi, wr_lo, br, n_groups, n_experts)
        offsets, plan = _tile_plan(hist, n_rows)
        pos = _positions(ids, offsets)
        xs = _dispatch(pos, h2, n_rows)
        ys = _grouped_mlp(plan, xs, moe_w_in, moe_w_out, i)
        x = _combine(pos, x1.reshape(t, d), wts.T, mod, fg, ys, s, final=(i == depth - 1)).reshape(b, s, d)
    return x
```

```python
import functools

import jax
import jax.numpy as jnp
from jax import lax
from jax.experimental import pallas as pl
from jax.experimental.pallas import tpu as pltpu

F32 = jnp.float32
BF16 = jnp.bfloat16

NORM_EPS = 1e-6
POOL_WINDOWS = (2, 4, 8, 16)
POOL_HALO = 16
GLA_CHUNK = 64
GLA_GATE_TEMP = 16.0
MOE_TOP_E = 2
LANES = 128
SUBLANES = 8
ROUTER_ROWS = 128
VMEM_LIMIT = 56 * 1024 * 1024

SEQ_TILE_POOL = 512
SEQ_TILE_GLA = 512
GMM_TILE = 256
ROW_TILE = 512


def _dot(a, b):
    return jnp.dot(a, b, preferred_element_type=F32)


def _dot_nt(a, b):
    return lax.dot_general(a, b, (((1,), (1,)), ((), ())), preferred_element_type=F32)


def _dot_tn(a, b):
    return lax.dot_general(a, b, (((0,), (0,)), ((), ())), preferred_element_type=F32)


def _split_bf16(x):
    hi = x.astype(BF16)
    lo = (x - hi.astype(F32)).astype(BF16)
    return hi, lo


def _store_token_tiles(ref, x):
    n = x.shape[0]
    for s in range(SUBLANES):
        ref[pl.ds(s, n, stride=SUBLANES), :] = x[:, s * LANES:(s + 1) * LANES]


def _load_token_tiles(ref, n, s):
    return ref[pl.ds(s, n, stride=SUBLANES), :]


def _norm_mod(x, gain, scale, shift):
    ms = jnp.mean(x * x, axis=-1, keepdims=True)
    return x * lax.rsqrt(ms + NORM_EPS) * gain * (1.0 + scale) + shift


def _mod_kernel(c_ref, w_ref, b_ref, o_ref):
    c = c_ref[...]
    sc = (c * (1.0 / (1.0 + jnp.exp(-c)))).astype(BF16)
    o_ref[0] = _dot(sc, w_ref[0].astype(BF16)) + b_ref[0]


def _modulation(c, w_mod, b_mod):
    depth, d, n = w_mod.shape
    b = c.shape[0]
    tn = n // 4
    return pl.pallas_call(
        _mod_kernel,
        grid=(depth, n // tn),
        in_specs=[
            pl.BlockSpec((b, d), lambda i, j: (0, 0)),
            pl.BlockSpec((1, d, tn), lambda i, j: (i, 0, j)),
            pl.BlockSpec((1, 1, tn), lambda i, j: (i, 0, j)),
        ],
        out_specs=pl.BlockSpec((1, b, tn), lambda i, j: (i, 0, j)),
        out_shape=jax.ShapeDtypeStruct((depth, b, n), F32),
        compiler_params=pltpu.CompilerParams(vmem_limit_bytes=VMEM_LIMIT),
    )(c, w_mod, b_mod.reshape(depth, 1, n))


def _router_epilogue(x1, mod, gain2, wr_hi_ref, wr_lo_ref, br_ref, n_groups, n_experts,
                     h2_ref, ids_ref, wts_ref, hist_ref, first_step):
    h2 = _norm_mod(x1, gain2, mod[4:5], mod[3:4])
    _store_token_tiles(h2_ref, h2)
    hh, hl = _split_bf16(h2)
    wh = wr_hi_ref[...]
    lt = _dot_nt(wh, hh) + _dot_nt(wh, hl) + _dot_nt(wr_lo_ref[...], hh) + br_ref[...]
    ts = lt.shape[1]
    per = n_experts // n_groups
    lg = lt[0:n_groups]
    mg = jnp.max(lg, axis=0, keepdims=True)
    p_g = 1.0 / jnp.sum(jnp.exp(lg - mg), axis=0, keepdims=True)
    gi = lax.broadcasted_iota(jnp.int32, lg.shape, 0)
    g_idx = jnp.min(jnp.where(lg == mg, gi, n_groups), axis=0, keepdims=True)
    sel = jnp.zeros((per, ts), F32)
    for g in range(n_groups):
        sel = jnp.where(g_idx == g, lt[n_groups + g * per:n_groups + (g + 1) * per], sel)
    ei = lax.broadcasted_iota(jnp.int32, sel.shape, 0)
    m1 = jnp.max(sel, axis=0, keepdims=True)
    i1 = jnp.min(jnp.where(sel == m1, ei, per), axis=0, keepdims=True)
    rest = jnp.where(ei == i1, -jnp.inf, sel)
    m2 = jnp.max(rest, axis=0, keepdims=True)
    i2 = jnp.min(jnp.where(rest == m2, ei, per), axis=0, keepdims=True)
    e21 = jnp.exp(m2 - m1)
    w1 = p_g / (1.0 + e21)
    w2 = p_g * e21 / (1.0 + e21)
    id1 = g_idx * per + i1
    id2 = g_idx * per + i2
    ids_ref[...] = jnp.concatenate([id1, id2], axis=0)
    wts_ref[...] = jnp.concatenate([w1, w2], axis=0)
    xi = lax.broadcasted_iota(jnp.int32, (n_experts, ts), 0)
    cnt = (xi == id1).astype(F32) + (xi == id2).astype(F32)

    @pl.when(first_step)
    def _():
        hist_ref[...] = cnt

    @pl.when(jnp.logical_not(first_step))
    def _():
        hist_ref[...] += cnt


def _router_specs(d, ts, n_seq_tiles, n_experts):
    in_specs = [
        pl.BlockSpec((ROUTER_ROWS, d), lambda b, s: (0, 0)),
        pl.BlockSpec((ROUTER_ROWS, d), lambda b, s: (0, 0)),
        pl.BlockSpec((ROUTER_ROWS, 1), lambda b, s: (0, 0)),
    ]
    out_specs = [
        pl.BlockSpec((1, ts, d), lambda b, s: (b, s, 0)),
        pl.BlockSpec((ts * SUBLANES, LANES), lambda b, s: (b * n_seq_tiles + s, 0)),
        pl.BlockSpec((MOE_TOP_E, ts), lambda b, s: (0, b * n_seq_tiles + s)),
        pl.BlockSpec((MOE_TOP_E, ts), lambda b, s: (0, b * n_seq_tiles + s)),
        pl.BlockSpec((n_experts, ts), lambda b, s: (0, 0)),
    ]
    return in_specs, out_specs


def _router_out_shapes(b, s, d, ts, n_experts):
    assert d == SUBLANES * LANES, "token-per-tile row layout needs d_model == one (8,128) tile"
    return [
        jax.ShapeDtypeStruct((b, s, d), F32),
        jax.ShapeDtypeStruct((b * s * SUBLANES, LANES), F32),
        jax.ShapeDtypeStruct((MOE_TOP_E, b * s), jnp.int32),
        jax.ShapeDtypeStruct((MOE_TOP_E, b * s), F32),
        jax.ShapeDtypeStruct((n_experts, ts), F32),
    ]


def _pool_kernel(x_ref, mod_ref, gain_ref, pw_ref, pb_ref, ps_ref, wr_hi_ref, wr_lo_ref, br_ref,
                 x1_ref, h2_ref, ids_ref, wts_ref, hist_ref, hbuf_ref, *, n_groups, n_experts):
    b, s = pl.program_id(0), pl.program_id(1)
    x = x_ref[0]
    ts, d = x.shape
    mod = mod_ref[0]
    h = _norm_mod(x, gain_ref[0:1], mod[1:2], mod[0:1])

    @pl.when(s == 0)
    def _():
        hbuf_ref[0:POOL_HALO] = jnp.zeros((POOL_HALO, d), F32)

    hbuf_ref[POOL_HALO:POOL_HALO + ts] = h
    pos = (s * ts + 1 + lax.broadcasted_iota(jnp.int32, (ts, 1), 0)).astype(F32)
    cg = d // len(POOL_WINDOWS)
    ys = []
    for gi, win in enumerate(POOL_WINDOWS):
        cols = slice(gi * cg, (gi + 1) * cg)
        acc = h[:, cols]
        for j in range(1, win):
            acc = acc + hbuf_ref[POOL_HALO - j:POOL_HALO - j + ts, cols]
        dgi = acc / jnp.minimum(pos, float(win)) - h[:, cols]
        ys.append(_dot(dgi.astype(BF16), pw_ref[gi]))
    y = (jnp.concatenate(ys, axis=-1) + pb_ref[...]) * ps_ref[...]
    hbuf_ref[0:POOL_HALO] = hbuf_ref[ts:ts + POOL_HALO]
    x1 = x + mod[2:3] * y
    x1_ref[0] = x1
    _router_epilogue(x1, mod, gain_ref[1:2], wr_hi_ref, wr_lo_ref, br_ref, n_groups, n_experts,
                     h2_ref, ids_ref, wts_ref, hist_ref, jnp.logical_and(b == 0, s == 0))


def _pool_layer(x, mod, gain, pw, pb, ps, wr_hi, wr_lo, br, n_groups, n_experts):
    b, s, d = x.shape
    ts = min(SEQ_TILE_POOL, s)
    nst = s // ts
    g, cg = pw.shape[0], pw.shape[1]
    r_in, r_out = _router_specs(d, ts, nst, n_experts)
    return pl.pallas_call(
        functools.partial(_pool_kernel, n_groups=n_groups, n_experts=n_experts),
        grid=(b, nst),
        in_specs=[
            pl.BlockSpec((1, ts, d), lambda bb, ss: (bb, ss, 0)),
            pl.BlockSpec((1, 6, d), lambda bb, ss: (bb, 0, 0)),
            pl.BlockSpec((2, d), lambda bb, ss: (0, 0)),
            pl.BlockSpec((g, cg, cg), lambda bb, ss: (0, 0, 0)),
            pl.BlockSpec((1, d), lambda bb, ss: (0, 0)),
            pl.BlockSpec((1, d), lambda bb, ss: (0, 0)),
        ] + r_in,
        out_specs=r_out,
        out_shape=_router_out_shapes(b, s, d, ts, n_experts),
        scratch_shapes=[pltpu.VMEM((POOL_HALO + ts, d), F32)],
        compiler_params=pltpu.CompilerParams(
            dimension_semantics=("arbitrary", "arbitrary"), vmem_limit_bytes=VMEM_LIMIT),
    )(x, mod, gain, pw.astype(BF16), pb.reshape(1, d), ps.reshape(1, d), wr_hi, wr_lo, br)


def _gla_kernel(x_ref, mod_ref, gain_ref, wp_ref, wz_ref, wg_ref, bg_ref, ng_ref, wo_ref,
                wr_hi_ref, wr_lo_ref, br_ref,
                x1_ref, h2_ref, ids_ref, wts_ref, hist_ref,
                state_ref, q_s, k_s, g_s, v_s, r_s, o_s, *, n_heads, n_groups, n_experts):
    b, s = pl.program_id(0), pl.program_id(1)
    x = x_ref[0]
    ts, d = x.shape
    dk_all = q_s.shape[1]
    dk = dk_all // n_heads
    dv = d // n_heads
    mod = mod_ref[0]
    h = _norm_mod(x, gain_ref[0:1], mod[1:2], mod[0:1]).astype(BF16)

    @pl.when(s == 0)
    def _():
        state_ref[...] = jnp.zeros(state_ref.shape, F32)

    q_s[...] = _dot(h, wp_ref[:, 0:dk_all]) * (dk ** -0.5)
    k_s[...] = _dot(h, wp_ref[:, dk_all:2 * dk_all])
    v_s[...] = _dot(h, wp_ref[:, 2 * dk_all:2 * dk_all + d]).astype(BF16)
    r = _dot(h, wp_ref[:, 2 * dk_all + d:2 * dk_all + 2 * d])
    r_s[...] = r * (1.0 / (1.0 + jnp.exp(-r)))
    z = _dot(h, wz_ref[...]).astype(BF16)
    u = _dot(z, wg_ref[...]) + bg_ref[...]
    g_s[...] = (jnp.minimum(u, 0.0) - jnp.log(1.0 + jnp.exp(-jnp.abs(u)))) * (1.0 / GLA_GATE_TEMP)

    c = GLA_CHUNK
    ri = lax.broadcasted_iota(jnp.int32, (c, c), 0)
    ci = lax.broadcasted_iota(jnp.int32, (c, c), 1)
    causal = ri >= ci
    tril = jnp.where(causal, 1.0, 0.0).astype(BF16)
    ng = ng_ref[...]

    def chunk(n, carry):
        rows = pl.ds(pl.multiple_of(n * c, c), c)
        g_hi, g_lo = _split_bf16(g_s[rows, :])
        bc = _dot(tril, g_hi) + _dot(tril, g_lo)
        bl = bc[c - 1:c, :]
        q_e = (q_s[rows, :] * jnp.exp(bc)).astype(BF16)
        kk = k_s[rows, :]
        k_e = (kk * jnp.exp(-bc)).astype(BF16)
        k_d = (kk * jnp.exp(bl - bc)).astype(BF16)
        e_l = jnp.exp(bl)
        vv = v_s[rows, :]
        for hd in range(n_heads):
            ks = slice(hd * dk, (hd + 1) * dk)
            vs = slice(hd * dv, (hd + 1) * dv)
            sc = jnp.where(causal, _dot_nt(q_e[:, ks], k_e[:, ks]), 0.0).astype(BF16)
            st = state_ref[hd]
            o = _dot(sc, vv[:, vs]) + _dot_nt(q_e[:, ks], st.astype(BF16))
            state_ref[hd] = st * e_l[:, ks] + _dot_tn(vv[:, vs], k_d[:, ks])
            o = o * lax.rsqrt(jnp.mean(o * o, axis=-1, keepdims=True) + NORM_EPS) * ng
            o_s[rows, vs] = (o * r_s[rows, vs]).astype(BF16)
        return carry

    lax.fori_loop(0, ts // c, chunk, 0, unroll=4)
    x1 = x + mod[2:3] * _dot(o_s[...], wo_ref[...])
    x1_ref[0] = x1
    _router_epilogue(x1, mod, gain_ref[1:2], wr_hi_ref, wr_lo_ref, br_ref, n_groups, n_experts,
                     h2_ref, ids_ref, wts_ref, hist_ref, jnp.logical_and(b == 0, s == 0))


def _gla_layer(x, mod, gain, w_in, w_gate, b_gate, norm_g, w_out, wr_hi, wr_lo, br, n_groups, n_experts):
    b, s, d = x.shape
    ts = min(SEQ_TILE_GLA, s)
    nst = s // ts
    rank, dk_all = w_gate.shape
    dv = norm_g.shape[0]
    n_heads = d // dv
    n_proj = 2 * dk_all + 2 * d
    wp = w_in[:, :n_proj].astype(BF16)
    wz = jnp.pad(w_in[:, n_proj:], ((0, 0), (0, LANES - rank))).astype(BF16)
    wg = jnp.pad(w_gate, ((0, LANES - rank), (0, 0))).astype(BF16)
    r_in, r_out = _router_specs(d, ts, nst, n_experts)
    const = lambda bb, ss: (0, 0)
    return pl.pallas_call(
        functools.partial(_gla_kernel, n_heads=n_heads, n_groups=n_groups, n_experts=n_experts),
        grid=(b, nst),
        in_specs=[
            pl.BlockSpec((1, ts, d), lambda bb, ss: (bb, ss, 0)),
            pl.BlockSpec((1, 6, d), lambda bb, ss: (bb, 0, 0)),
            pl.BlockSpec((2, d), const),
            pl.BlockSpec((d, n_proj), const),
            pl.BlockSpec((d, LANES), const),
            pl.BlockSpec((LANES, dk_all), const),
            pl.BlockSpec((1, dk_all), const),
            pl.BlockSpec((1, dv), const),
            pl.BlockSpec((d, d), const),
        ] + r_in,
        out_specs=r_out,
        out_shape=_router_out_shapes(b, s, d, ts, n_experts),
        scratch_shapes=[
            pltpu.VMEM((n_heads, dv, dk_all // n_heads), F32),
            pltpu.VMEM((ts, dk_all), F32),
            pltpu.VMEM((ts, dk_all), F32),
            pltpu.VMEM((ts, dk_all), F32),
            pltpu.VMEM((ts, d), BF16),
            pltpu.VMEM((ts, d), F32),
            pltpu.VMEM((ts, d), BF16),
        ],
        compiler_params=pltpu.CompilerParams(
            dimension_semantics=("arbitrary", "arbitrary"), vmem_limit_bytes=VMEM_LIMIT),
    )(x, mod, gain, wp, wz, wg, b_gate.reshape(1, dk_all), norm_g.reshape(1, dv), w_out.astype(BF16),
      wr_hi, wr_lo, br)


def _pos_kernel(ids_ref, off_ref, pos_ref, run_ref):
    i = pl.program_id(0)
    ids = ids_ref[...]
    tp = ids.shape[1]
    n_e = off_ref.shape[0]

    @pl.when(i == 0)
    def _():
        run_ref[...] = jnp.zeros(run_ref.shape, F32)

    ei = lax.broadcasted_iota(jnp.int32, (n_e, tp), 0)
    oh0 = ei == ids[0:1]
    oh1 = ei == ids[1:2]
    tri = jnp.where(lax.broadcasted_iota(jnp.int32, (tp, tp), 0) <= lax.broadcasted_iota(jnp.int32, (tp, tp), 1),
                    1.0, 0.0).astype(BF16)
    inc0 = _dot(jnp.where(oh0, 1.0, 0.0).astype(BF16), tri)
    inc1 = _dot(jnp.where(oh1, 1.0, 0.0).astype(BF16), tri)
    tot0 = inc0[:, tp - 1:tp]
    tot1 = inc1[:, tp - 1:tp]
    base = off_ref[...] + run_ref[...]
    p0 = jnp.sum(jnp.where(oh0, base + inc0 - 1.0, 0.0), axis=0, keepdims=True)
    p1 = jnp.sum(jnp.where(oh1, base + tot0 + inc1 - 1.0, 0.0), axis=0, keepdims=True)
    pos_ref[0] = jnp.concatenate([p0, p1], axis=0).astype(jnp.int32)
    run_ref[...] += tot0 + tot1


def _positions(ids, offsets):
    t = ids.shape[1]
    tp = min(ROW_TILE, t)
    n_e = offsets.shape[0]
    return pl.pallas_call(
        _pos_kernel,
        grid=(t // tp,),
        in_specs=[pl.BlockSpec((MOE_TOP_E, tp), lambda i: (0, i)),
                  pl.BlockSpec((n_e, 1), lambda i: (0, 0))],
        out_specs=pl.BlockSpec((1, MOE_TOP_E, tp), lambda i: (i, 0, 0)),
        out_shape=jax.ShapeDtypeStruct((t // tp, MOE_TOP_E, tp), jnp.int32),
        scratch_shapes=[pltpu.VMEM((n_e, 1), F32)],
        compiler_params=pltpu.CompilerParams(dimension_semantics=("arbitrary",)),
    )(ids, offsets)


def _tile_rows(row):
    return pl.ds(pl.multiple_of(row * SUBLANES, SUBLANES), SUBLANES)


def _for_each_token(tp, fn):
    def group(j, carry):
        r0 = pl.multiple_of(j * SUBLANES, SUBLANES)
        for u in range(SUBLANES):
            fn(r0 + u)
        return carry
    lax.fori_loop(0, tp // SUBLANES, group, 0)


def _dispatch_kernel(pos_ref, h_ref, xs_ref, sem):
    tp = pos_ref.shape[2]

    def issue(r):
        for k in range(MOE_TOP_E):
            pltpu.make_async_copy(h_ref.at[_tile_rows(r)], xs_ref.at[_tile_rows(pos_ref[0, k, r])],
                                  sem).start(priority=k)

    _for_each_token(tp, issue)
    for k in range(MOE_TOP_E):
        pltpu.make_async_copy(h_ref, xs_ref.at[pl.ds(0, tp * SUBLANES)], sem).wait()


def _dispatch(pos, h2, n_rows):
    nt, _, tp = pos.shape
    return pl.pallas_call(
        _dispatch_kernel,
        grid=(nt,),
        in_specs=[pl.BlockSpec((1, MOE_TOP_E, tp), lambda i: (i, 0, 0), memory_space=pltpu.SMEM),
                  pl.BlockSpec((tp * SUBLANES, LANES), lambda i: (i, 0))],
        out_specs=pl.BlockSpec(memory_space=pl.ANY),
        out_shape=jax.ShapeDtypeStruct((n_rows * SUBLANES, LANES), F32),
        scratch_shapes=[pltpu.SemaphoreType.DMA(())],
        compiler_params=pltpu.CompilerParams(dimension_semantics=("arbitrary",), has_side_effects=True),
    )(pos, h2)


def _gmm_kernel(it_ref, iq_ref, lo_ref, hi_ref, qe_ref, n_ref, nq_ref, xs_ref, wi_hbm, wo_hbm, ys_ref,
                wi_s, wo_s, a_s, acc_s, wi_f, wo_f, sem_i, sem_o, *, layer):
    w = pl.program_id(0)
    f = wo_s.shape[0]
    tm = xs_ref.shape[0] // SUBLANES
    n = n_ref[0]

    def weight_copies(q, slot):
        e = qe_ref[q]
        return (pltpu.make_async_copy(wi_hbm.at[layer, e], wi_f.at[slot], sem_i.at[slot]),
                pltpu.make_async_copy(wo_hbm.at[layer, e], wo_f.at[slot], sem_o.at[slot]))

    @pl.when(w <= n)
    def _():
        ja = jnp.minimum(w, n - 1)
        jb = jnp.maximum(w - 1, 0)
        slot_a = lax.rem(w, 2)
        qa = iq_ref[ja]
        qb = iq_ref[jb]

        @pl.when(w == 0)
        def _():
            for c in weight_copies(0, 0):
                c.start()
            a_s[1] = jnp.zeros(a_s.shape[1:], BF16)
            acc_s[...] = jnp.zeros(acc_s.shape, F32)
            wo_s[...] = jnp.zeros(wo_s.shape, BF16)

        @pl.when(jnp.logical_or(w == 1, jnp.logical_and(w >= 2, qb != iq_ref[jnp.maximum(w - 2, 0)])))
        def _():
            slot = lax.rem(qb, 2)
            weight_copies(qb, slot)[1].wait()
            wo_s[...] = wo_f[slot].astype(BF16)

        @pl.when(jnp.logical_or(w == 0, qa != qb))
        def _():
            @pl.when(qa + 1 < nq_ref[0])
            def _():
                for c in weight_copies(qa + 1, lax.rem(qa + 1, 2)):
                    c.start()

            slot = lax.rem(qa, 2)
            weight_copies(qa, slot)[0].wait()
            wi_s[...] = wi_f[slot].astype(BF16)

        y = _dot(a_s[1 - slot_a], wo_s[...])
        row = lax.broadcasted_iota(jnp.int32, (tm, 1), 0)
        mine = jnp.logical_and(jnp.logical_and(row >= lo_ref[jb], row < hi_ref[jb]), w >= 1)
        merged = jnp.where(mine, y, acc_s[...])
        acc_s[...] = merged
        _store_token_tiles(ys_ref, merged)

        x = jnp.concatenate([_load_token_tiles(xs_ref, tm, s).astype(BF16) for s in range(SUBLANES)], axis=-1)
        gu = _dot(x, wi_s[...])
        gate, up = gu[:, :f], gu[:, f:]
        a_s[slot_a] = (gate * (1.0 / (1.0 + jnp.exp(-gate))) * up).astype(BF16)


def _grouped_mlp(plan, xs, w_in, w_out, layer):
    item_tile, item_seq, item_lo, item_hi, seq_expert, n_items, n_seq = plan
    d, f2 = w_in.shape[-2:]
    f = f2 // 2
    tm = GMM_TILE
    rows = tm * SUBLANES

    def first(w, n):
        return jnp.minimum(w, n[0] - 1)

    def second(w, n):
        return jnp.minimum(jnp.maximum(w - 1, 0), n[0] - 1)

    return pl.pallas_call(
        functools.partial(_gmm_kernel, layer=layer),
        grid_spec=pltpu.PrefetchScalarGridSpec(
            num_scalar_prefetch=7,
            grid=(item_tile.shape[0] + 1,),
            in_specs=[
                pl.BlockSpec((rows, LANES), lambda w, it, iq, lo, hi, qe, n, nq: (it[first(w, n)], 0)),
                pl.BlockSpec(memory_space=pl.ANY),
                pl.BlockSpec(memory_space=pl.ANY),
            ],
            out_specs=pl.BlockSpec((rows, LANES), lambda w, it, iq, lo, hi, qe, n, nq: (it[second(w, n)], 0)),
            scratch_shapes=[pltpu.VMEM((d, f2), BF16), pltpu.VMEM((f, d), BF16), pltpu.VMEM((2, tm, f), BF16),
                            pltpu.VMEM((tm, d), F32), pltpu.VMEM((2, d, f2), F32), pltpu.VMEM((2, f, d), F32),
                            pltpu.SemaphoreType.DMA((2,)), pltpu.SemaphoreType.DMA((2,))],
        ),
        out_shape=jax.ShapeDtypeStruct(xs.shape, F32),
        compiler_params=pltpu.CompilerParams(
            dimension_semantics=("arbitrary",), vmem_limit_bytes=VMEM_LIMIT),
    )(item_tile, item_seq, item_lo, item_hi, seq_expert, n_items, n_seq, xs, w_in, w_out)


def _combine_kernel(pos_ref, posn_ref, x1_ref, wt_ref, mod_ref, fg_ref, ys_ref, o_ref, buf, sems, *, final):
    i = pl.program_id(0)
    n = pl.num_programs(0)
    tp = pos_ref.shape[2]

    def issue(idx_ref, slot):
        def one(r):
            for k in range(MOE_TOP_E):
                pltpu.make_async_copy(ys_ref.at[_tile_rows(idx_ref[0, k, r])], buf.at[slot, k, _tile_rows(r)],
                                      sems.at[slot]).start(priority=k)
        _for_each_token(tp, one)

    slot = lax.rem(i, 2)

    @pl.when(i == 0)
    def _():
        issue(pos_ref, 0)

    @pl.when(i + 1 < n)
    def _():
        issue(posn_ref, 1 - slot)

    for k in range(MOE_TOP_E):
        pltpu.make_async_copy(ys_ref.at[pl.ds(0, tp * SUBLANES)], buf.at[slot, k], sems.at[slot]).wait()
    wt = wt_ref[...]
    gate2 = mod_ref[0, 5:6]
    chunks = []
    for s in range(SUBLANES):
        rows = pl.ds(s, tp, stride=SUBLANES)
        cols = slice(s * LANES, (s + 1) * LANES)
        y = wt[:, 0:1] * buf[slot, 0, rows, :] + wt[:, 1:2] * buf[slot, 1, rows, :]
        chunks.append(x1_ref[:, cols] + gate2[:, cols] * y)
    x = jnp.concatenate(chunks, axis=-1)
    if final:
        x = x * lax.rsqrt(jnp.mean(x * x, axis=-1, keepdims=True) + NORM_EPS) * fg_ref[...]
    o_ref[...] = x


def _combine(pos, x1, wts_t, mod, final_g, ys, seq_len, final):
    nt, _, tp = pos.shape
    t, d = x1.shape
    per_seq = seq_len // tp
    return pl.pallas_call(
        functools.partial(_combine_kernel, final=final),
        grid=(nt,),
        in_specs=[
            pl.BlockSpec((1, MOE_TOP_E, tp), lambda i: (i, 0, 0), memory_space=pltpu.SMEM),
            pl.BlockSpec((1, MOE_TOP_E, tp), lambda i: (jnp.minimum(i + 1, nt - 1), 0, 0),
                         memory_space=pltpu.SMEM),
            pl.BlockSpec((tp, d), lambda i: (i, 0)),
            pl.BlockSpec((tp, MOE_TOP_E), lambda i: (i, 0)),
            pl.BlockSpec((1, 6, d), lambda i: (i // per_seq, 0, 0)),
            pl.BlockSpec((1, d), lambda i: (0, 0)),
            pl.BlockSpec(memory_space=pl.ANY),
        ],
        out_specs=pl.BlockSpec((tp, d), lambda i: (i, 0)),
        out_shape=jax.ShapeDtypeStruct((t, d), F32),
        scratch_shapes=[pltpu.VMEM((2, MOE_TOP_E, tp * SUBLANES, LANES), F32), pltpu.SemaphoreType.DMA((2,))],
        compiler_params=pltpu.CompilerParams(
            dimension_semantics=("arbitrary",), vmem_limit_bytes=VMEM_LIMIT),
    )(pos, pos, x1, wts_t, mod, final_g, ys)


def _router_weights(w_group, b_group, w_expert, b_expert):
    d, g = w_group.shape
    e = w_expert.shape[1]
    w = jnp.concatenate([w_group, w_expert], axis=1).T
    w = jnp.pad(w, ((0, ROUTER_ROWS - g - e), (0, 0)))
    hi, lo = _split_bf16(w)
    bias = jnp.pad(jnp.concatenate([b_group, b_expert]), (0, ROUTER_ROWS - g - e)).reshape(ROUTER_ROWS, 1)
    return hi, lo, bias


def _tile_plan(hist, n_rows):
    n_e = hist.shape[0]
    tm = GMM_TILE
    counts = jnp.sum(hist, axis=1).astype(jnp.int32)
    ends = jnp.cumsum(counts)
    starts = ends - counts
    first_tile = starts // tm
    items_per = jnp.where(counts > 0, (ends - 1) // tm - first_tile + 1, 0)
    item_ends = jnp.cumsum(items_per)
    n_items = item_ends[-1:]
    max_items = n_rows // tm + n_e - 1
    w = jnp.minimum(jnp.arange(max_items, dtype=jnp.int32), n_items[0] - 1)
    item_expert = jnp.sum(w[:, None] >= item_ends[None, :], axis=1).astype(jnp.int32)
    item_tile = first_tile[item_expert] + w - (item_ends - items_per)[item_expert]
    item_lo = jnp.maximum(starts[item_expert] - item_tile * tm, 0)
    item_hi = jnp.minimum(ends[item_expert] - item_tile * tm, tm)
    offsets = starts.astype(F32).reshape(n_e, 1)
    used_ends = jnp.cumsum((counts > 0).astype(jnp.int32))
    n_seq = used_ends[-1:]
    q = jnp.minimum(jnp.arange(n_e, dtype=jnp.int32), n_seq[0] - 1)
    seq_expert = jnp.sum(q[:, None] >= used_ends[None, :], axis=1).astype(jnp.int32)
    item_seq = (used_ends - 1)[item_expert]
    return offsets, (item_tile.astype(jnp.int32), item_seq.astype(jnp.int32), item_lo.astype(jnp.int32),
                     item_hi.astype(jnp.int32), seq_expert, n_items.astype(jnp.int32), n_seq.astype(jnp.int32))


def kernel(x, c, norm_gain, w_mod, b_mod, pool_w, pool_b, pool_scale, gla_w_in, gla_w_gate, gla_b_gate, gla_norm_g, gla_w_out, moe_w_group, moe_b_group, moe_w_expert, moe_b_expert, moe_w_in, moe_w_out, final_norm_g):
    b, s, d = x.shape
    depth = w_mod.shape[0]
    t = b * s
    n_groups = moe_w_group.shape[-1]
    n_experts = moe_w_expert.shape[-1]
    n_rows = MOE_TOP_E * t
    mod_all = _modulation(c, w_mod, b_mod).reshape(depth, b, 6, d)
    fg = final_norm_g.reshape(1, d)
    for i in range(depth):
        mod = mod_all[i]
        wr_hi, wr_lo, br = _router_weights(moe_w_group[i], moe_b_group[i], moe_w_expert[i], moe_b_expert[i])
        j = i // 2
        if i % 2 == 0:
            x1, h2, ids, wts, hist = _pool_layer(x, mod, norm_gain[i], pool_w[j], pool_b[j], pool_scale[j],
                                                 wr_hi, wr_lo, br, n_groups, n_experts)
        else:
            x1, h2, ids, wts, hist = _gla_layer(x, mod, norm_gain[i], gla_w_in[j], gla_w_gate[j], gla_b_gate[j],
                                                gla_norm_g[j], gla_w_out[j], wr_hi, wr_lo, br, n_groups, n_experts)
        offsets, plan = _tile_plan(hist, n_rows)
        pos = _positions(ids, offsets)
        xs = _dispatch(pos, h2, n_rows)
        ys = _grouped_mlp(plan, xs, moe_w_in, moe_w_out, i)
        x = _combine(pos, x1.reshape(t, d), wts.T, mod, fg, ys, s, final=(i == depth - 1)).reshape(b, s, d)
    return x
```

```python
import functools

import jax
import jax.numpy as jnp
from jax import lax
from jax.experimental import pallas as pl
from jax.experimental.pallas import tpu as pltpu

F32 = jnp.float32
BF16 = jnp.bfloat16

NORM_EPS = 1e-6
POOL_WINDOWS = (2, 4, 8, 16)
POOL_HALO = 16
GLA_CHUNK = 64
GLA_GATE_TEMP = 16.0
MOE_TOP_E = 2
LANES = 128
SUBLANES = 8
ROUTER_ROWS = 128
VMEM_LIMIT = 56 * 1024 * 1024

SEQ_TILE_POOL = 512
SEQ_TILE_GLA = 512
GMM_TILE = 256
XS_RING = 3
ROW_TILE = 512


def _dot(a, b):
    return jnp.dot(a, b, preferred_element_type=F32)


def _dot_nt(a, b):
    return lax.dot_general(a, b, (((1,), (1,)), ((), ())), preferred_element_type=F32)


def _dot_tn(a, b):
    return lax.dot_general(a, b, (((0,), (0,)), ((), ())), preferred_element_type=F32)


def _split_bf16(x):
    hi = x.astype(BF16)
    lo = (x - hi.astype(F32)).astype(BF16)
    return hi, lo


def _store_token_tiles(ref, x):
    n = x.shape[0]
    for s in range(SUBLANES):
        ref[pl.ds(s, n, stride=SUBLANES), :] = x[:, s * LANES:(s + 1) * LANES]


def _load_token_tiles(ref, n, s):
    return ref[pl.ds(s, n, stride=SUBLANES), :]


def _norm_mod(x, gain, scale, shift):
    ms = jnp.mean(x * x, axis=-1, keepdims=True)
    return x * lax.rsqrt(ms + NORM_EPS) * gain * (1.0 + scale) + shift


def _mod_kernel(c_ref, w_ref, b_ref, o_ref):
    c = c_ref[...]
    sc = (c * (1.0 / (1.0 + jnp.exp(-c)))).astype(BF16)
    o_ref[0] = _dot(sc, w_ref[0].astype(BF16)) + b_ref[0]


def _modulation(c, w_mod, b_mod):
    depth, d, n = w_mod.shape
    b = c.shape[0]
    tn = n // 4
    return pl.pallas_call(
        _mod_kernel,
        grid=(depth, n // tn),
        in_specs=[
            pl.BlockSpec((b, d), lambda i, j: (0, 0)),
            pl.BlockSpec((1, d, tn), lambda i, j: (i, 0, j)),
            pl.BlockSpec((1, 1, tn), lambda i, j: (i, 0, j)),
        ],
        out_specs=pl.BlockSpec((1, b, tn), lambda i, j: (i, 0, j)),
        out_shape=jax.ShapeDtypeStruct((depth, b, n), F32),
        compiler_params=pltpu.CompilerParams(vmem_limit_bytes=VMEM_LIMIT),
    )(c, w_mod, b_mod.reshape(depth, 1, n))


def _router_epilogue(x1, mod, gain2, wr_hi_ref, wr_lo_ref, br_ref, n_groups, n_experts,
                     h2_ref, ids_ref, wts_ref, hist_ref, first_step):
    h2 = _norm_mod(x1, gain2, mod[4:5], mod[3:4])
    _store_token_tiles(h2_ref, h2)
    hh, hl = _split_bf16(h2)
    wh = wr_hi_ref[...]
    lt = _dot_nt(wh, hh) + _dot_nt(wh, hl) + _dot_nt(wr_lo_ref[...], hh) + br_ref[...]
    ts = lt.shape[1]
    per = n_experts // n_groups
    lg = lt[0:n_groups]
    mg = jnp.max(lg, axis=0, keepdims=True)
    p_g = 1.0 / jnp.sum(jnp.exp(lg - mg), axis=0, keepdims=True)
    gi = lax.broadcasted_iota(jnp.int32, lg.shape, 0)
    g_idx = jnp.min(jnp.where(lg == mg, gi, n_groups), axis=0, keepdims=True)
    sel = jnp.zeros((per, ts), F32)
    for g in range(n_groups):
        sel = jnp.where(g_idx == g, lt[n_groups + g * per:n_groups + (g + 1) * per], sel)
    ei = lax.broadcasted_iota(jnp.int32, sel.shape, 0)
    m1 = jnp.max(sel, axis=0, keepdims=True)
    i1 = jnp.min(jnp.where(sel == m1, ei, per), axis=0, keepdims=True)
    rest = jnp.where(ei == i1, -jnp.inf, sel)
    m2 = jnp.max(rest, axis=0, keepdims=True)
    i2 = jnp.min(jnp.where(rest == m2, ei, per), axis=0, keepdims=True)
    e21 = jnp.exp(m2 - m1)
    w1 = p_g / (1.0 + e21)
    w2 = p_g * e21 / (1.0 + e21)
    id1 = g_idx * per + i1
    id2 = g_idx * per + i2
    ids_ref[...] = jnp.concatenate([id1, id2], axis=0)
    wts_ref[...] = jnp.concatenate([w1, w2], axis=0)
    xi = lax.broadcasted_iota(jnp.int32, (n_experts, ts), 0)
    cnt = (xi == id1).astype(F32) + (xi == id2).astype(F32)

    @pl.when(first_step)
    def _():
        hist_ref[...] = cnt

    @pl.when(jnp.logical_not(first_step))
    def _():
        hist_ref[...] += cnt


def _router_specs(d, ts, n_seq_tiles, n_experts):
    in_specs = [
        pl.BlockSpec((ROUTER_ROWS, d), lambda b, s: (0, 0)),
        pl.BlockSpec((ROUTER_ROWS, d), lambda b, s: (0, 0)),
        pl.BlockSpec((ROUTER_ROWS, 1), lambda b, s: (0, 0)),
    ]
    out_specs = [
        pl.BlockSpec((1, ts, d), lambda b, s: (b, s, 0)),
        pl.BlockSpec((ts * SUBLANES, LANES), lambda b, s: (b * n_seq_tiles + s, 0)),
        pl.BlockSpec((MOE_TOP_E, ts), lambda b, s: (0, b * n_seq_tiles + s)),
        pl.BlockSpec((MOE_TOP_E, ts), lambda b, s: (0, b * n_seq_tiles + s)),
        pl.BlockSpec((n_experts, ts), lambda b, s: (0, 0)),
    ]
    return in_specs, out_specs


def _router_out_shapes(b, s, d, ts, n_experts):
    assert d == SUBLANES * LANES, "token-per-tile row layout needs d_model == one (8,128) tile"
    return [
        jax.ShapeDtypeStruct((b, s, d), F32),
        jax.ShapeDtypeStruct((b * s * SUBLANES, LANES), F32),
        jax.ShapeDtypeStruct((MOE_TOP_E, b * s), jnp.int32),
        jax.ShapeDtypeStruct((MOE_TOP_E, b * s), F32),
        jax.ShapeDtypeStruct((n_experts, ts), F32),
    ]


def _pool_kernel(x_ref, mod_ref, gain_ref, pw_ref, pb_ref, ps_ref, wr_hi_ref, wr_lo_ref, br_ref,
                 x1_ref, h2_ref, ids_ref, wts_ref, hist_ref, hbuf_ref, sa_ref, sb_ref, *, n_groups, n_experts):
    b, s = pl.program_id(0), pl.program_id(1)
    x = x_ref[0]
    ts, d = x.shape
    mod = mod_ref[0]
    h = _norm_mod(x, gain_ref[0:1], mod[1:2], mod[0:1])

    first = SUBLANES + POOL_HALO
    n = first + ts

    @pl.when(s == 0)
    def _():
        hbuf_ref[0:first] = jnp.zeros((first, d), F32)
        sa_ref[0:SUBLANES] = jnp.zeros((SUBLANES, d), F32)
        sb_ref[0:SUBLANES] = jnp.zeros((SUBLANES, d), F32)

    hbuf_ref[first:n] = h
    pos = (s * ts + 1 + lax.broadcasted_iota(jnp.int32, (ts, 1), 0)).astype(F32)
    cg = d // len(POOL_WINDOWS)
    src, dst = hbuf_ref, sa_ref
    ys = []
    for gi, win in enumerate(POOL_WINDOWS):
        half = win // 2
        assert win == 2 ** (gi + 1) and win <= POOL_HALO
        lo = gi * cg
        dst[SUBLANES:n, lo:] = src[SUBLANES:n, lo:] + src[SUBLANES - half:n - half, lo:]
        cols = slice(lo, lo + cg)
        dgi = dst[first:n, cols] / jnp.minimum(pos, float(win)) - h[:, cols]
        ys.append(_dot(dgi.astype(BF16), pw_ref[gi]))
        src, dst = dst, (sb_ref if dst is sa_ref else sa_ref)
    y = (jnp.concatenate(ys, axis=-1) + pb_ref[...]) * ps_ref[...]
    hbuf_ref[SUBLANES:first] = hbuf_ref[n - POOL_HALO:n]
    x1 = x + mod[2:3] * y
    x1_ref[0] = x1
    _router_epilogue(x1, mod, gain_ref[1:2], wr_hi_ref, wr_lo_ref, br_ref, n_groups, n_experts,
                     h2_ref, ids_ref, wts_ref, hist_ref, jnp.logical_and(b == 0, s == 0))


def _pool_layer(x, mod, gain, pw, pb, ps, wr_hi, wr_lo, br, n_groups, n_experts):
    b, s, d = x.shape
    ts = min(SEQ_TILE_POOL, s)
    nst = s // ts
    g, cg = pw.shape[0], pw.shape[1]
    r_in, r_out = _router_specs(d, ts, nst, n_experts)
    return pl.pallas_call(
        functools.partial(_pool_kernel, n_groups=n_groups, n_experts=n_experts),
        grid=(b, nst),
        in_specs=[
            pl.BlockSpec((1, ts, d), lambda bb, ss: (bb, ss, 0)),
            pl.BlockSpec((1, 6, d), lambda bb, ss: (bb, 0, 0)),
            pl.BlockSpec((2, d), lambda bb, ss: (0, 0)),
            pl.BlockSpec((g, cg, cg), lambda bb, ss: (0, 0, 0)),
            pl.BlockSpec((1, d), lambda bb, ss: (0, 0)),
            pl.BlockSpec((1, d), lambda bb, ss: (0, 0)),
        ] + r_in,
        out_specs=r_out,
        out_shape=_router_out_shapes(b, s, d, ts, n_experts),
        scratch_shapes=[pltpu.VMEM((SUBLANES + POOL_HALO + ts, d), F32)] * 3,
        compiler_params=pltpu.CompilerParams(
            dimension_semantics=("arbitrary", "arbitrary"), vmem_limit_bytes=VMEM_LIMIT),
    )(x, mod, gain, pw.astype(BF16), pb.reshape(1, d), ps.reshape(1, d), wr_hi, wr_lo, br)


def _gla_kernel(x_ref, mod_ref, gain_ref, wp_ref, wz_ref, wg_ref, bg_ref, ng_ref, wo_ref,
                wr_hi_ref, wr_lo_ref, br_ref,
                x1_ref, h2_ref, ids_ref, wts_ref, hist_ref,
                state_ref, q_s, k_s, g_s, v_s, r_s, o_s, *, n_heads, n_groups, n_experts):
    b, s = pl.program_id(0), pl.program_id(1)
    x = x_ref[0]
    ts, d = x.shape
    dk_all = q_s.shape[1]
    dk = dk_all // n_heads
    dv = d // n_heads
    mod = mod_ref[0]
    h = _norm_mod(x, gain_ref[0:1], mod[1:2], mod[0:1]).astype(BF16)

    @pl.when(s == 0)
    def _():
        state_ref[...] = jnp.zeros(state_ref.shape, F32)

    q_s[...] = _dot(h, wp_ref[:, 0:dk_all]) * (dk ** -0.5)
    k_s[...] = _dot(h, wp_ref[:, dk_all:2 * dk_all])
    v_s[...] = _dot(h, wp_ref[:, 2 * dk_all:2 * dk_all + d]).astype(BF16)
    r = _dot(h, wp_ref[:, 2 * dk_all + d:2 * dk_all + 2 * d])
    r_s[...] = r * (1.0 / (1.0 + jnp.exp(-r)))
    z = _dot(h, wz_ref[...]).astype(BF16)
    u = _dot(z, wg_ref[...]) + bg_ref[...]
    g_s[...] = (jnp.minimum(u, 0.0) - jnp.log(1.0 + jnp.exp(-jnp.abs(u)))) * (1.0 / GLA_GATE_TEMP)

    c = GLA_CHUNK
    ri = lax.broadcasted_iota(jnp.int32, (c, c), 0)
    ci = lax.broadcasted_iota(jnp.int32, (c, c), 1)
    causal = ri >= ci
    tril = jnp.where(causal, 1.0, 0.0).astype(BF16)
    ng = ng_ref[...]

    def chunk(n, carry):
        rows = pl.ds(pl.multiple_of(n * c, c), c)
        g_hi, g_lo = _split_bf16(g_s[rows, :])
        bc = _dot(tril, g_hi) + _dot(tril, g_lo)
        bl = bc[c - 1:c, :]
        q_e = (q_s[rows, :] * jnp.exp(bc)).astype(BF16)
        kk = k_s[rows, :]
        k_e = (kk * jnp.exp(-bc)).astype(BF16)
        k_d = (kk * jnp.exp(bl - bc)).astype(BF16)
        e_l = jnp.exp(bl)
        vv = v_s[rows, :]
        for hd in range(n_heads):
            ks = slice(hd * dk, (hd + 1) * dk)
            vs = slice(hd * dv, (hd + 1) * dv)
            sc = jnp.where(causal, _dot_nt(q_e[:, ks], k_e[:, ks]), 0.0).astype(BF16)
            st = state_ref[hd]
            o = _dot(sc, vv[:, vs]) + _dot_nt(q_e[:, ks], st.astype(BF16))
            state_ref[hd] = st * e_l[:, ks] + _dot_tn(vv[:, vs], k_d[:, ks])
            o = o * lax.rsqrt(jnp.mean(o * o, axis=-1, keepdims=True) + NORM_EPS) * ng
            o_s[rows, vs] = (o * r_s[rows, vs]).astype(BF16)
        return carry

    lax.fori_loop(0, ts // c, chunk, 0, unroll=4)
    x1 = x + mod[2:3] * _dot(o_s[...], wo_ref[...])
    x1_ref[0] = x1
    _router_epilogue(x1, mod, gain_ref[1:2], wr_hi_ref, wr_lo_ref, br_ref, n_groups, n_experts,
                     h2_ref, ids_ref, wts_ref, hist_ref, jnp.logical_and(b == 0, s == 0))


def _gla_layer(x, mod, gain, w_in, w_gate, b_gate, norm_g, w_out, wr_hi, wr_lo, br, n_groups, n_experts):
    b, s, d = x.shape
    ts = min(SEQ_TILE_GLA, s)
    nst = s // ts
    rank, dk_all = w_gate.shape
    dv = norm_g.shape[0]
    n_heads = d // dv
    n_proj = 2 * dk_all + 2 * d
    wp = w_in[:, :n_proj].astype(BF16)
    wz = jnp.pad(w_in[:, n_proj:], ((0, 0), (0, LANES - rank))).astype(BF16)
    wg = jnp.pad(w_gate, ((0, LANES - rank), (0, 0))).astype(BF16)
    r_in, r_out = _router_specs(d, ts, nst, n_experts)
    const = lambda bb, ss: (0, 0)
    return pl.pallas_call(
        functools.partial(_gla_kernel, n_heads=n_heads, n_groups=n_groups, n_experts=n_experts),
        grid=(b, nst),
        in_specs=[
            pl.BlockSpec((1, ts, d), lambda bb, ss: (bb, ss, 0)),
            pl.BlockSpec((1, 6, d), lambda bb, ss: (bb, 0, 0)),
            pl.BlockSpec((2, d), const),
            pl.BlockSpec((d, n_proj), const),
            pl.BlockSpec((d, LANES), const),
            pl.BlockSpec((LANES, dk_all), const),
            pl.BlockSpec((1, dk_all), const),
            pl.BlockSpec((1, dv), const),
            pl.BlockSpec((d, d), const),
        ] + r_in,
        out_specs=r_out,
        out_shape=_router_out_shapes(b, s, d, ts, n_experts),
        scratch_shapes=[
            pltpu.VMEM((n_heads, dv, dk_all // n_heads), F32),
            pltpu.VMEM((ts, dk_all), F32),
            pltpu.VMEM((ts, dk_all), F32),
            pltpu.VMEM((ts, dk_all), F32),
            pltpu.VMEM((ts, d), BF16),
            pltpu.VMEM((ts, d), F32),
            pltpu.VMEM((ts, d), BF16),
        ],
        compiler_params=pltpu.CompilerParams(
            dimension_semantics=("arbitrary", "arbitrary"), vmem_limit_bytes=VMEM_LIMIT),
    )(x, mod, gain, wp, wz, wg, b_gate.reshape(1, dk_all), norm_g.reshape(1, dv), w_out.astype(BF16),
      wr_hi, wr_lo, br)


def _pos_kernel(ids_ref, off_ref, pos_ref, run_ref):
    i = pl.program_id(0)
    ids = ids_ref[...]
    tp = ids.shape[1]
    n_e = off_ref.shape[0]

    @pl.when(i == 0)
    def _():
        run_ref[...] = jnp.zeros(run_ref.shape, F32)

    ei = lax.broadcasted_iota(jnp.int32, (n_e, tp), 0)
    oh0 = ei == ids[0:1]
    oh1 = ei == ids[1:2]
    tri = jnp.where(lax.broadcasted_iota(jnp.int32, (tp, tp), 0) <= lax.broadcasted_iota(jnp.int32, (tp, tp), 1),
                    1.0, 0.0).astype(BF16)
    inc0 = _dot(jnp.where(oh0, 1.0, 0.0).astype(BF16), tri)
    inc1 = _dot(jnp.where(oh1, 1.0, 0.0).astype(BF16), tri)
    tot0 = inc0[:, tp - 1:tp]
    tot1 = inc1[:, tp - 1:tp]
    base = off_ref[...] + run_ref[...]
    p0 = jnp.sum(jnp.where(oh0, base + inc0 - 1.0, 0.0), axis=0, keepdims=True)
    p1 = jnp.sum(jnp.where(oh1, base + tot0 + inc1 - 1.0, 0.0), axis=0, keepdims=True)
    pos_ref[0] = jnp.concatenate([p0, p1], axis=0).astype(jnp.int32)
    run_ref[...] += tot0 + tot1


def _positions(ids, offsets):
    t = ids.shape[1]
    tp = min(ROW_TILE, t)
    n_e = offsets.shape[0]
    return pl.pallas_call(
        _pos_kernel,
        grid=(t // tp,),
        in_specs=[pl.BlockSpec((MOE_TOP_E, tp), lambda i: (0, i)),
                  pl.BlockSpec((n_e, 1), lambda i: (0, 0))],
        out_specs=pl.BlockSpec((1, MOE_TOP_E, tp), lambda i: (i, 0, 0)),
        out_shape=jax.ShapeDtypeStruct((t // tp, MOE_TOP_E, tp), jnp.int32),
        scratch_shapes=[pltpu.VMEM((n_e, 1), F32)],
        compiler_params=pltpu.CompilerParams(dimension_semantics=("arbitrary",)),
    )(ids, offsets)


def _tile_rows(row):
    return pl.ds(pl.multiple_of(row * SUBLANES, SUBLANES), SUBLANES)


def _for_each_token(tp, fn):
    def group(j, carry):
        r0 = pl.multiple_of(j * SUBLANES, SUBLANES)
        for u in range(SUBLANES):
            fn(r0 + u)
        return carry
    lax.fori_loop(0, tp // SUBLANES, group, 0)


def _dispatch_kernel(pos_ref, h_ref, xs_ref, sem):
    tp = pos_ref.shape[2]

    def issue(r):
        for k in range(MOE_TOP_E):
            pltpu.make_async_copy(h_ref.at[_tile_rows(r)], xs_ref.at[_tile_rows(pos_ref[0, k, r])],
                                  sem).start(priority=k)

    _for_each_token(tp, issue)
    for k in range(MOE_TOP_E):
        pltpu.make_async_copy(h_ref, xs_ref.at[pl.ds(0, tp * SUBLANES)], sem).wait()


def _dispatch(pos, h2, n_rows):
    nt, _, tp = pos.shape
    return pl.pallas_call(
        _dispatch_kernel,
        grid=(nt,),
        in_specs=[pl.BlockSpec((1, MOE_TOP_E, tp), lambda i: (i, 0, 0), memory_space=pltpu.SMEM),
                  pl.BlockSpec((tp * SUBLANES, LANES), lambda i: (i, 0))],
        out_specs=pl.BlockSpec(memory_space=pl.ANY),
        out_shape=jax.ShapeDtypeStruct((n_rows * SUBLANES, LANES), F32),
        scratch_shapes=[pltpu.SemaphoreType.DMA(())],
        compiler_params=pltpu.CompilerParams(dimension_semantics=("arbitrary",), has_side_effects=True),
    )(pos, h2)


def _gmm_kernel(it_ref, iq_ref, lo_ref, hi_ref, qe_ref, n_ref, nq_ref, xs_hbm, wi_hbm, wo_hbm, ys_ref,
                wi_s, wo_s, a_s, acc_s, wi_f, wo_f, xs_buf, sem_i, sem_o, sem_x, *, layer):
    w = pl.program_id(0)
    f = wo_s.shape[0]
    tm = acc_s.shape[0]
    n = n_ref[0]

    def weight_copies(q, slot):
        e = qe_ref[q]
        return (pltpu.make_async_copy(wi_hbm.at[layer, e], wi_f.at[slot], sem_i.at[slot]),
                pltpu.make_async_copy(wo_hbm.at[layer, e], wo_f.at[slot], sem_o.at[slot]))

    def rows_copy(j):
        slot = lax.rem(j, XS_RING)
        src = xs_hbm.at[pl.ds(pl.multiple_of(it_ref[j] * (tm * SUBLANES), tm * SUBLANES), tm * SUBLANES)]
        return pltpu.make_async_copy(src, xs_buf.at[slot], sem_x.at[slot])

    @pl.when(w <= n)
    def _():
        ja = jnp.minimum(w, n - 1)
        jb = jnp.maximum(w - 1, 0)
        slot_a = lax.rem(w, 2)
        qa = iq_ref[ja]
        qb = iq_ref[jb]

        @pl.when(w == 0)
        def _():
            for c in weight_copies(0, 0):
                c.start()
            for j in range(XS_RING - 1):
                @pl.when(j < n)
                def _():
                    rows_copy(j).start()
            a_s[1] = jnp.zeros(a_s.shape[1:], BF16)
            acc_s[...] = jnp.zeros(acc_s.shape, F32)
            wo_s[...] = jnp.zeros(wo_s.shape, BF16)

        @pl.when(w + XS_RING - 1 < n)
        def _():
            rows_copy(w + XS_RING - 1).start()

        @pl.when(w < n)
        def _():
            rows_copy(w).wait()

        @pl.when(jnp.logical_or(w == 1, jnp.logical_and(w >= 2, qb != iq_ref[jnp.maximum(w - 2, 0)])))
        def _():
            slot = lax.rem(qb, 2)
            weight_copies(qb, slot)[1].wait()
            wo_s[...] = wo_f[slot].astype(BF16)

        @pl.when(jnp.logical_or(w == 0, qa != qb))
        def _():
            @pl.when(qa + 1 < nq_ref[0])
            def _():
                for c in weight_copies(qa + 1, lax.rem(qa + 1, 2)):
                    c.start()

            slot = lax.rem(qa, 2)
            weight_copies(qa, slot)[0].wait()
            wi_s[...] = wi_f[slot].astype(BF16)

        y = _dot(a_s[1 - slot_a], wo_s[...])
        row = lax.broadcasted_iota(jnp.int32, (tm, 1), 0)
        mine = jnp.logical_and(jnp.logical_and(row >= lo_ref[jb], row < hi_ref[jb]), w >= 1)
        merged = jnp.where(mine, y, acc_s[...])
        acc_s[...] = merged
        _store_token_tiles(ys_ref, merged)

        xs_tile = xs_buf.at[lax.rem(ja, XS_RING)]
        x = jnp.concatenate([_load_token_tiles(xs_tile, tm, s).astype(BF16) for s in range(SUBLANES)], axis=-1)
        gu = _dot(x, wi_s[...])
        gate, up = gu[:, :f], gu[:, f:]
        a_s[slot_a] = (gate * (1.0 / (1.0 + jnp.exp(-gate))) * up).astype(BF16)


def _grouped_mlp(plan, xs, w_in, w_out, layer):
    item_tile, item_seq, item_lo, item_hi, seq_expert, n_items, n_seq = plan
    d, f2 = w_in.shape[-2:]
    f = f2 // 2
    tm = GMM_TILE
    rows = tm * SUBLANES

    def second(w, n):
        return jnp.minimum(jnp.maximum(w - 1, 0), n[0] - 1)

    return pl.pallas_call(
        functools.partial(_gmm_kernel, layer=layer),
        grid_spec=pltpu.PrefetchScalarGridSpec(
            num_scalar_prefetch=7,
            grid=(item_tile.shape[0] + 1,),
            in_specs=[pl.BlockSpec(memory_space=pl.ANY)] * 3,
            out_specs=pl.BlockSpec((rows, LANES), lambda w, it, iq, lo, hi, qe, n, nq: (it[second(w, n)], 0)),
            scratch_shapes=[pltpu.VMEM((d, f2), BF16), pltpu.VMEM((f, d), BF16), pltpu.VMEM((2, tm, f), BF16),
                            pltpu.VMEM((tm, d), F32), pltpu.VMEM((2, d, f2), F32), pltpu.VMEM((2, f, d), F32),
                            pltpu.VMEM((XS_RING, rows, LANES), F32),
                            pltpu.SemaphoreType.DMA((2,)), pltpu.SemaphoreType.DMA((2,)),
                            pltpu.SemaphoreType.DMA((XS_RING,))],
        ),
        out_shape=jax.ShapeDtypeStruct(xs.shape, F32),
        compiler_params=pltpu.CompilerParams(
            dimension_semantics=("arbitrary",), vmem_limit_bytes=VMEM_LIMIT),
    )(item_tile, item_seq, item_lo, item_hi, seq_expert, n_items, n_seq, xs, w_in, w_out)


def _combine_kernel(pos_ref, posn_ref, x1_ref, wt_ref, mod_ref, fg_ref, ys_ref, o_ref, buf, sems, *, final):
    i = pl.program_id(0)
    n = pl.num_programs(0)
    tp = pos_ref.shape[2]

    def issue(idx_ref, slot):
        def one(r):
            for k in range(MOE_TOP_E):
                pltpu.make_async_copy(ys_ref.at[_tile_rows(idx_ref[0, k, r])], buf.at[slot, k, _tile_rows(r)],
                                      sems.at[slot]).start(priority=k)
        _for_each_token(tp, one)

    slot = lax.rem(i, 2)

    @pl.when(i == 0)
    def _():
        issue(pos_ref, 0)

    @pl.when(i + 1 < n)
    def _():
        issue(posn_ref, 1 - slot)

    for k in range(MOE_TOP_E):
        pltpu.make_async_copy(ys_ref.at[pl.ds(0, tp * SUBLANES)], buf.at[slot, k], sems.at[slot]).wait()
    wt = wt_ref[...]
    gate2 = mod_ref[0, 5:6]
    chunks = []
    for s in range(SUBLANES):
        rows = pl.ds(s, tp, stride=SUBLANES)
        cols = slice(s * LANES, (s + 1) * LANES)
        y = wt[:, 0:1] * buf[slot, 0, rows, :] + wt[:, 1:2] * buf[slot, 1, rows, :]
        chunks.append(x1_ref[:, cols] + gate2[:, cols] * y)
    x = jnp.concatenate(chunks, axis=-1)
    if final:
        x = x * lax.rsqrt(jnp.mean(x * x, axis=-1, keepdims=True) + NORM_EPS) * fg_ref[...]
    o_ref[...] = x


def _combine(pos, x1, wts_t, mod, final_g, ys, seq_len, final):
    nt, _, tp = pos.shape
    t, d = x1.shape
    per_seq = seq_len // tp
    return pl.pallas_call(
        functools.partial(_combine_kernel, final=final),
        grid=(nt,),
        in_specs=[
            pl.BlockSpec((1, MOE_TOP_E, tp), lambda i: (i, 0, 0), memory_space=pltpu.SMEM),
            pl.BlockSpec((1, MOE_TOP_E, tp), lambda i: (jnp.minimum(i + 1, nt - 1), 0, 0),
                         memory_space=pltpu.SMEM),
            pl.BlockSpec((tp, d), lambda i: (i, 0)),
            pl.BlockSpec((tp, MOE_TOP_E), lambda i: (i, 0)),
            pl.BlockSpec((1, 6, d), lambda i: (i // per_seq, 0, 0)),
            pl.BlockSpec((1, d), lambda i: (0, 0)),
            pl.BlockSpec(memory_space=pl.ANY),
        ],
        out_specs=pl.BlockSpec((tp, d), lambda i: (i, 0)),
        out_shape=jax.ShapeDtypeStruct((t, d), F32),
        scratch_shapes=[pltpu.VMEM((2, MOE_TOP_E, tp * SUBLANES, LANES), F32), pltpu.SemaphoreType.DMA((2,))],
        compiler_params=pltpu.CompilerParams(
            dimension_semantics=("arbitrary",), vmem_limit_bytes=VMEM_LIMIT),
    )(pos, pos, x1, wts_t, mod, final_g, ys)


def _router_weights(w_group, b_group, w_expert, b_expert):
    d, g = w_group.shape
    e = w_expert.shape[1]
    w = jnp.concatenate([w_group, w_expert], axis=1).T
    w = jnp.pad(w, ((0, ROUTER_ROWS - g - e), (0, 0)))
    hi, lo = _split_bf16(w)
    bias = jnp.pad(jnp.concatenate([b_group, b_expert]), (0, ROUTER_ROWS - g - e)).reshape(ROUTER_ROWS, 1)
    return hi, lo, bias


def _tile_plan(hist, n_rows):
    n_e = hist.shape[0]
    tm = GMM_TILE
    counts = jnp.sum(hist, axis=1).astype(jnp.int32)
    ends = jnp.cumsum(counts)
    starts = ends - counts
    first_tile = starts // tm
    items_per = jnp.where(counts > 0, (ends - 1) // tm - first_tile + 1, 0)
    item_ends = jnp.cumsum(items_per)
    n_items = item_ends[-1:]
    max_items = n_rows // tm + n_e - 1
    w = jnp.minimum(jnp.arange(max_items, dtype=jnp.int32), n_items[0] - 1)
    item_expert = jnp.sum(w[:, None] >= item_ends[None, :], axis=1).astype(jnp.int32)
    item_tile = first_tile[item_expert] + w - (item_ends - items_per)[item_expert]
    item_lo = jnp.maximum(starts[item_expert] - item_tile * tm, 0)
    item_hi = jnp.minimum(ends[item_expert] - item_tile * tm, tm)
    offsets = starts.astype(F32).reshape(n_e, 1)
    used_ends = jnp.cumsum((counts > 0).astype(jnp.int32))
    n_seq = used_ends[-1:]
    q = jnp.minimum(jnp.arange(n_e, dtype=jnp.int32), n_seq[0] - 1)
    seq_expert = jnp.sum(q[:, None] >= used_ends[None, :], axis=1).astype(jnp.int32)
    item_seq = (used_ends - 1)[item_expert]
    return offsets, (item_tile.astype(jnp.int32), item_seq.astype(jnp.int32), item_lo.astype(jnp.int32),
                     item_hi.astype(jnp.int32), seq_expert, n_items.astype(jnp.int32), n_seq.astype(jnp.int32))


def kernel(x, c, norm_gain, w_mod, b_mod, pool_w, pool_b, pool_scale, gla_w_in, gla_w_gate, gla_b_gate, gla_norm_g, gla_w_out, moe_w_group, moe_b_group, moe_w_expert, moe_b_expert, moe_w_in, moe_w_out, final_norm_g):
    b, s, d = x.shape
    depth = w_mod.shape[0]
    t = b * s
    n_groups = moe_w_group.shape[-1]
    n_experts = moe_w_expert.shape[-1]
    n_rows = MOE_TOP_E * t
    mod_all = _modulation(c, w_mod, b_mod).reshape(depth, b, 6, d)
    fg = final_norm_g.reshape(1, d)
    for i in range(depth):
        mod = mod_all[i]
        wr_hi, wr_lo, br = _router_weights(moe_w_group[i], moe_b_group[i], moe_w_expert[i], moe_b_expert[i])
        j = i // 2
        if i % 2 == 0:
            x1, h2, ids, wts, hist = _pool_layer(x, mod, norm_gain[i], pool_w[j], pool_b[j], pool_scale[j],
                                                 wr_hi, wr_lo, br, n_groups, n_experts)
        else:
            x1, h2, ids, wts, hist = _gla_layer(x, mod, norm_gain[i], gla_w_in[j], gla_w_gate[j], gla_b_gate[j],
                                                gla_norm_g[j], gla_w_out[j], wr_hi, wr_lo, br, n_groups, n_experts)
        offsets, plan = _tile_plan(hist, n_rows)
        pos = _positions(ids, offsets)
        xs = _dispatch(pos, h2, n_rows)
        ys = _grouped_mlp(plan, xs, moe_w_in, moe_w_out, i)
        x = _combine(pos, x1.reshape(t, d), wts.T, mod, fg, ys, s, final=(i == depth - 1)).reshape(b, s, d)
    return x
```

```python
import functools

import jax
import jax.numpy as jnp
from jax import lax
from jax.experimental import pallas as pl
from jax.experimental.pallas import tpu as pltpu

F32 = jnp.float32
BF16 = jnp.bfloat16

NORM_EPS = 1e-6
POOL_WINDOWS = (2, 4, 8, 16)
POOL_HALO = 16
GLA_CHUNK = 64
GLA_GATE_TEMP = 16.0
MOE_TOP_E = 2
LANES = 128
SUBLANES = 8
ROUTER_ROWS = 128
VMEM_LIMIT = 56 * 1024 * 1024

SEQ_TILE_POOL = 512
SEQ_TILE_GLA = 512
GMM_TILE = 256
XS_RING = 3
COMBINE_RING = 3
ROW_TILE = 512


def _dot(a, b):
    return jnp.dot(a, b, preferred_element_type=F32)


def _dot_nt(a, b):
    return lax.dot_general(a, b, (((1,), (1,)), ((), ())), preferred_element_type=F32)


def _dot_tn(a, b):
    return lax.dot_general(a, b, (((0,), (0,)), ((), ())), preferred_element_type=F32)


def _split_bf16(x):
    hi = x.astype(BF16)
    lo = (x - hi.astype(F32)).astype(BF16)
    return hi, lo


def _store_token_tiles(ref, x):
    n = x.shape[0]
    for s in range(SUBLANES):
        ref[pl.ds(s, n, stride=SUBLANES), :] = x[:, s * LANES:(s + 1) * LANES]


def _load_token_tiles(ref, n, s):
    return ref[pl.ds(s, n, stride=SUBLANES), :]


def _norm_mod(x, gain, scale, shift):
    ms = jnp.mean(x * x, axis=-1, keepdims=True)
    return x * lax.rsqrt(ms + NORM_EPS) * gain * (1.0 + scale) + shift


def _mod_kernel(c_ref, w_ref, b_ref, o_ref):
    c = c_ref[...]
    sc = (c * (1.0 / (1.0 + jnp.exp(-c)))).astype(BF16)
    o_ref[0] = _dot(sc, w_ref[0].astype(BF16)) + b_ref[0]


def _modulation(c, w_mod, b_mod):
    depth, d, n = w_mod.shape
    b = c.shape[0]
    tn = n // 4
    return pl.pallas_call(
        _mod_kernel,
        grid=(depth, n // tn),
        in_specs=[
            pl.BlockSpec((b, d), lambda i, j: (0, 0)),
            pl.BlockSpec((1, d, tn), lambda i, j: (i, 0, j)),
            pl.BlockSpec((1, 1, tn), lambda i, j: (i, 0, j)),
        ],
        out_specs=pl.BlockSpec((1, b, tn), lambda i, j: (i, 0, j)),
        out_shape=jax.ShapeDtypeStruct((depth, b, n), F32),
        compiler_params=pltpu.CompilerParams(vmem_limit_bytes=VMEM_LIMIT),
    )(c, w_mod, b_mod.reshape(depth, 1, n))


def _router_epilogue(x1, mod, gain2, wr_hi_ref, wr_lo_ref, br_ref, n_groups, n_experts,
                     h2_ref, ids_ref, wts_ref, hist_ref, first_step):
    h2 = _norm_mod(x1, gain2, mod[4:5], mod[3:4])
    _store_token_tiles(h2_ref, h2)
    hh, hl = _split_bf16(h2)
    wh = wr_hi_ref[...]
    lt = _dot_nt(wh, hh) + _dot_nt(wh, hl) + _dot_nt(wr_lo_ref[...], hh) + br_ref[...]
    ts = lt.shape[1]
    per = n_experts // n_groups
    lg = lt[0:n_groups]
    mg = jnp.max(lg, axis=0, keepdims=True)
    p_g = 1.0 / jnp.sum(jnp.exp(lg - mg), axis=0, keepdims=True)
    gi = lax.broadcasted_iota(jnp.int32, lg.shape, 0)
    g_idx = jnp.min(jnp.where(lg == mg, gi, n_groups), axis=0, keepdims=True)
    sel = jnp.zeros((per, ts), F32)
    for g in range(n_groups):
        sel = jnp.where(g_idx == g, lt[n_groups + g * per:n_groups + (g + 1) * per], sel)
    ei = lax.broadcasted_iota(jnp.int32, sel.shape, 0)
    m1 = jnp.max(sel, axis=0, keepdims=True)
    i1 = jnp.min(jnp.where(sel == m1, ei, per), axis=0, keepdims=True)
    rest = jnp.where(ei == i1, -jnp.inf, sel)
    m2 = jnp.max(rest, axis=0, keepdims=True)
    i2 = jnp.min(jnp.where(rest == m2, ei, per), axis=0, keepdims=True)
    e21 = jnp.exp(m2 - m1)
    w1 = p_g / (1.0 + e21)
    w2 = p_g * e21 / (1.0 + e21)
    id1 = g_idx * per + i1
    id2 = g_idx * per + i2
    ids_ref[...] = jnp.concatenate([id1, id2], axis=0)
    wts_ref[...] = jnp.concatenate([w1, w2], axis=0)
    xi = lax.broadcasted_iota(jnp.int32, (n_experts, ts), 0)
    cnt = (xi == id1).astype(F32) + (xi == id2).astype(F32)

    @pl.when(first_step)
    def _():
        hist_ref[...] = cnt

    @pl.when(jnp.logical_not(first_step))
    def _():
        hist_ref[...] += cnt


def _router_specs(d, ts, n_seq_tiles, n_experts):
    in_specs = [
        pl.BlockSpec((ROUTER_ROWS, d), lambda b, s: (0, 0)),
        pl.BlockSpec((ROUTER_ROWS, d), lambda b, s: (0, 0)),
        pl.BlockSpec((ROUTER_ROWS, 1), lambda b, s: (0, 0)),
    ]
    out_specs = [
        pl.BlockSpec((1, ts, d), lambda b, s: (b, s, 0)),
        pl.BlockSpec((ts * SUBLANES, LANES), lambda b, s: (b * n_seq_tiles + s, 0)),
        pl.BlockSpec((MOE_TOP_E, ts), lambda b, s: (0, b * n_seq_tiles + s)),
        pl.BlockSpec((MOE_TOP_E, ts), lambda b, s: (0, b * n_seq_tiles + s)),
        pl.BlockSpec((n_experts, ts), lambda b, s: (0, 0)),
    ]
    return in_specs, out_specs


def _router_out_shapes(b, s, d, ts, n_experts):
    assert d == SUBLANES * LANES, "token-per-tile row layout needs d_model == one (8,128) tile"
    return [
        jax.ShapeDtypeStruct((b, s, d), F32),
        jax.ShapeDtypeStruct((b * s * SUBLANES, LANES), F32),
        jax.ShapeDtypeStruct((MOE_TOP_E, b * s), jnp.int32),
        jax.ShapeDtypeStruct((MOE_TOP_E, b * s), F32),
        jax.ShapeDtypeStruct((n_experts, ts), F32),
    ]


def _pool_kernel(x_ref, mod_ref, gain_ref, pw_ref, pb_ref, ps_ref, wr_hi_ref, wr_lo_ref, br_ref,
                 x1_ref, h2_ref, ids_ref, wts_ref, hist_ref, hbuf_ref, sa_ref, sb_ref, *, n_groups, n_experts):
    b, s = pl.program_id(0), pl.program_id(1)
    x = x_ref[0]
    ts, d = x.shape
    mod = mod_ref[0]
    h = _norm_mod(x, gain_ref[0:1], mod[1:2], mod[0:1])

    first = SUBLANES + POOL_HALO
    n = first + ts

    @pl.when(s == 0)
    def _():
        hbuf_ref[0:first] = jnp.zeros((first, d), F32)
        sa_ref[0:SUBLANES] = jnp.zeros((SUBLANES, d), F32)
        sb_ref[0:SUBLANES] = jnp.zeros((SUBLANES, d), F32)

    hbuf_ref[first:n] = h
    pos = (s * ts + 1 + lax.broadcasted_iota(jnp.int32, (ts, 1), 0)).astype(F32)
    cg = d // len(POOL_WINDOWS)
    src, dst = hbuf_ref, sa_ref
    ys = []
    for gi, win in enumerate(POOL_WINDOWS):
        half = win // 2
        assert win == 2 ** (gi + 1) and win <= POOL_HALO
        lo = gi * cg
        dst[SUBLANES:n, lo:] = src[SUBLANES:n, lo:] + src[SUBLANES - half:n - half, lo:]
        cols = slice(lo, lo + cg)
        dgi = dst[first:n, cols] / jnp.minimum(pos, float(win)) - h[:, cols]
        ys.append(_dot(dgi.astype(BF16), pw_ref[gi]))
        src, dst = dst, (sb_ref if dst is sa_ref else sa_ref)
    y = (jnp.concatenate(ys, axis=-1) + pb_ref[...]) * ps_ref[...]
    hbuf_ref[SUBLANES:first] = hbuf_ref[n - POOL_HALO:n]
    x1 = x + mod[2:3] * y
    x1_ref[0] = x1
    _router_epilogue(x1, mod, gain_ref[1:2], wr_hi_ref, wr_lo_ref, br_ref, n_groups, n_experts,
                     h2_ref, ids_ref, wts_ref, hist_ref, jnp.logical_and(b == 0, s == 0))


def _pool_layer(x, mod, gain, pw, pb, ps, wr_hi, wr_lo, br, n_groups, n_experts):
    b, s, d = x.shape
    ts = min(SEQ_TILE_POOL, s)
    nst = s // ts
    g, cg = pw.shape[0], pw.shape[1]
    r_in, r_out = _router_specs(d, ts, nst, n_experts)
    return pl.pallas_call(
        functools.partial(_pool_kernel, n_groups=n_groups, n_experts=n_experts),
        grid=(b, nst),
        in_specs=[
            pl.BlockSpec((1, ts, d), lambda bb, ss: (bb, ss, 0)),
            pl.BlockSpec((1, 6, d), lambda bb, ss: (bb, 0, 0)),
            pl.BlockSpec((2, d), lambda bb, ss: (0, 0)),
            pl.BlockSpec((g, cg, cg), lambda bb, ss: (0, 0, 0)),
            pl.BlockSpec((1, d), lambda bb, ss: (0, 0)),
            pl.BlockSpec((1, d), lambda bb, ss: (0, 0)),
        ] + r_in,
        out_specs=r_out,
        out_shape=_router_out_shapes(b, s, d, ts, n_experts),
        scratch_shapes=[pltpu.VMEM((SUBLANES + POOL_HALO + ts, d), F32)] * 3,
        compiler_params=pltpu.CompilerParams(
            dimension_semantics=("arbitrary", "arbitrary"), vmem_limit_bytes=VMEM_LIMIT),
    )(x, mod, gain, pw.astype(BF16), pb.reshape(1, d), ps.reshape(1, d), wr_hi, wr_lo, br)


def _gla_kernel(x_ref, mod_ref, gain_ref, wp_ref, wz_ref, wg_ref, bg_ref, ng_ref, wo_ref,
                wr_hi_ref, wr_lo_ref, br_ref,
                x1_ref, h2_ref, ids_ref, wts_ref, hist_ref,
                state_ref, q_s, k_s, g_s, v_s, r_s, o_s, *, n_heads, n_groups, n_experts):
    b, s = pl.program_id(0), pl.program_id(1)
    x = x_ref[0]
    ts, d = x.shape
    dk_all = q_s.shape[1]
    dk = dk_all // n_heads
    dv = d // n_heads
    mod = mod_ref[0]
    h = _norm_mod(x, gain_ref[0:1], mod[1:2], mod[0:1]).astype(BF16)

    @pl.when(s == 0)
    def _():
        state_ref[...] = jnp.zeros(state_ref.shape, F32)

    q_s[...] = _dot(h, wp_ref[:, 0:dk_all]) * (dk ** -0.5)
    k_s[...] = _dot(h, wp_ref[:, dk_all:2 * dk_all])
    v_s[...] = _dot(h, wp_ref[:, 2 * dk_all:2 * dk_all + d]).astype(BF16)
    r = _dot(h, wp_ref[:, 2 * dk_all + d:2 * dk_all + 2 * d])
    r_s[...] = r * (1.0 / (1.0 + jnp.exp(-r)))
    z = _dot(h, wz_ref[...]).astype(BF16)
    u = _dot(z, wg_ref[...]) + bg_ref[...]
    g_s[...] = (jnp.minimum(u, 0.0) - jnp.log(1.0 + jnp.exp(-jnp.abs(u)))) * (1.0 / GLA_GATE_TEMP)

    c = GLA_CHUNK
    ri = lax.broadcasted_iota(jnp.int32, (c, c), 0)
    ci = lax.broadcasted_iota(jnp.int32, (c, c), 1)
    causal = ri >= ci
    tril = jnp.where(causal, 1.0, 0.0).astype(BF16)
    ng = ng_ref[...]

    def chunk(n, carry):
        rows = pl.ds(pl.multiple_of(n * c, c), c)
        g_hi, g_lo = _split_bf16(g_s[rows, :])
        bc = _dot(tril, g_hi) + _dot(tril, g_lo)
        bl = bc[c - 1:c, :]
        q_e = (q_s[rows, :] * jnp.exp(bc)).astype(BF16)
        kk = k_s[rows, :]
        k_e = (kk * jnp.exp(-bc)).astype(BF16)
        k_d = (kk * jnp.exp(bl - bc)).astype(BF16)
        e_l = jnp.exp(bl)
        vv = v_s[rows, :]
        for hd in range(n_heads):
            ks = slice(hd * dk, (hd + 1) * dk)
            vs = slice(hd * dv, (hd + 1) * dv)
            sc = jnp.where(causal, _dot_nt(q_e[:, ks], k_e[:, ks]), 0.0).astype(BF16)
            st = state_ref[hd]
            o = _dot(sc, vv[:, vs]) + _dot_nt(q_e[:, ks], st.astype(BF16))
            state_ref[hd] = st * e_l[:, ks] + _dot_tn(vv[:, vs], k_d[:, ks])
            o = o * lax.rsqrt(jnp.mean(o * o, axis=-1, keepdims=True) + NORM_EPS) * ng
            o_s[rows, vs] = (o * r_s[rows, vs]).astype(BF16)
        return carry

    lax.fori_loop(0, ts // c, chunk, 0, unroll=4)
    x1 = x + mod[2:3] * _dot(o_s[...], wo_ref[...])
    x1_ref[0] = x1
    _router_epilogue(x1, mod, gain_ref[1:2], wr_hi_ref, wr_lo_ref, br_ref, n_groups, n_experts,
                     h2_ref, ids_ref, wts_ref, hist_ref, jnp.logical_and(b == 0, s == 0))


def _gla_layer(x, mod, gain, w_in, w_gate, b_gate, norm_g, w_out, wr_hi, wr_lo, br, n_groups, n_experts):
    b, s, d = x.shape
    ts = min(SEQ_TILE_GLA, s)
    nst = s // ts
    rank, dk_all = w_gate.shape
    dv = norm_g.shape[0]
    n_heads = d // dv
    n_proj = 2 * dk_all + 2 * d
    wp = w_in[:, :n_proj].astype(BF16)
    wz = jnp.pad(w_in[:, n_proj:], ((0, 0), (0, LANES - rank))).astype(BF16)
    wg = jnp.pad(w_gate, ((0, LANES - rank), (0, 0))).astype(BF16)
    r_in, r_out = _router_specs(d, ts, nst, n_experts)
    const = lambda bb, ss: (0, 0)
    return pl.pallas_call(
        functools.partial(_gla_kernel, n_heads=n_heads, n_groups=n_groups, n_experts=n_experts),
        grid=(b, nst),
        in_specs=[
            pl.BlockSpec((1, ts, d), lambda bb, ss: (bb, ss, 0)),
            pl.BlockSpec((1, 6, d), lambda bb, ss: (bb, 0, 0)),
            pl.BlockSpec((2, d), const),
            pl.BlockSpec((d, n_proj), const),
            pl.BlockSpec((d, LANES), const),
            pl.BlockSpec((LANES, dk_all), const),
            pl.BlockSpec((1, dk_all), const),
            pl.BlockSpec((1, dv), const),
            pl.BlockSpec((d, d), const),
        ] + r_in,
        out_specs=r_out,
        out_shape=_router_out_shapes(b, s, d, ts, n_experts),
        scratch_shapes=[
            pltpu.VMEM((n_heads, dv, dk_all // n_heads), F32),
            pltpu.VMEM((ts, dk_all), F32),
            pltpu.VMEM((ts, dk_all), F32),
            pltpu.VMEM((ts, dk_all), F32),
            pltpu.VMEM((ts, d), BF16),
            pltpu.VMEM((ts, d), F32),
            pltpu.VMEM((ts, d), BF16),
        ],
        compiler_params=pltpu.CompilerParams(
            dimension_semantics=("arbitrary", "arbitrary"), vmem_limit_bytes=VMEM_LIMIT),
    )(x, mod, gain, wp, wz, wg, b_gate.reshape(1, dk_all), norm_g.reshape(1, dv), w_out.astype(BF16),
      wr_hi, wr_lo, br)


def _pos_kernel(ids_ref, off_ref, pos_ref, run_ref):
    i = pl.program_id(0)
    ids = ids_ref[...]
    tp = ids.shape[1]
    n_e = off_ref.shape[0]

    @pl.when(i == 0)
    def _():
        run_ref[...] = jnp.zeros(run_ref.shape, F32)

    ei = lax.broadcasted_iota(jnp.int32, (n_e, tp), 0)
    oh0 = ei == ids[0:1]
    oh1 = ei == ids[1:2]
    tri = jnp.where(lax.broadcasted_iota(jnp.int32, (tp, tp), 0) <= lax.broadcasted_iota(jnp.int32, (tp, tp), 1),
                    1.0, 0.0).astype(BF16)
    inc0 = _dot(jnp.where(oh0, 1.0, 0.0).astype(BF16), tri)
    inc1 = _dot(jnp.where(oh1, 1.0, 0.0).astype(BF16), tri)
    tot0 = inc0[:, tp - 1:tp]
    tot1 = inc1[:, tp - 1:tp]
    base = off_ref[...] + run_ref[...]
    p0 = jnp.sum(jnp.where(oh0, base + inc0 - 1.0, 0.0), axis=0, keepdims=True)
    p1 = jnp.sum(jnp.where(oh1, base + tot0 + inc1 - 1.0, 0.0), axis=0, keepdims=True)
    p = [pk[:, c * LANES:(c + 1) * LANES] for pk in (p0, p1) for c in range(tp // LANES)]
    pos_ref[0] = jnp.concatenate(p, axis=0).astype(jnp.int32)
    run_ref[...] += tot0 + tot1


def _positions(ids, offsets):
    t = ids.shape[1]
    tp = min(ROW_TILE, t)
    n_e = offsets.shape[0]
    return pl.pallas_call(
        _pos_kernel,
        grid=(t // tp,),
        in_specs=[pl.BlockSpec((MOE_TOP_E, tp), lambda i: (0, i)),
                  pl.BlockSpec((n_e, 1), lambda i: (0, 0))],
        out_specs=pl.BlockSpec((1, MOE_TOP_E * tp // LANES, LANES), lambda i: (i, 0, 0)),
        out_shape=jax.ShapeDtypeStruct((t // tp, MOE_TOP_E * tp // LANES, LANES), jnp.int32),
        scratch_shapes=[pltpu.VMEM((n_e, 1), F32)],
        compiler_params=pltpu.CompilerParams(dimension_semantics=("arbitrary",)),
    )(ids, offsets)


def _tile_rows(row):
    return pl.ds(pl.multiple_of(row * SUBLANES, SUBLANES), SUBLANES)


def _for_each_token(tp, fn):
    chunks = tp // LANES
    for c in range(chunks):
        for lane in range(LANES):
            for k in range(MOE_TOP_E):
                fn(k, k * chunks + c, lane, c * LANES + lane)


def _dispatch_kernel(pos_ref, h_ref, xs_ref, sem):
    tp = h_ref.shape[0] // SUBLANES

    def issue(k, row, lane, token):
        pltpu.make_async_copy(h_ref.at[_tile_rows(token)], xs_ref.at[_tile_rows(pos_ref[0, row, lane])],
                              sem).start(priority=k)

    _for_each_token(tp, issue)
    for k in range(MOE_TOP_E):
        pltpu.make_async_copy(h_ref, xs_ref.at[pl.ds(0, tp * SUBLANES)], sem).wait()


def _dispatch(pos, h2, n_rows):
    nt, idx_rows, _ = pos.shape
    tp = idx_rows * LANES // MOE_TOP_E
    return pl.pallas_call(
        _dispatch_kernel,
        grid=(nt,),
        in_specs=[pl.BlockSpec((1, idx_rows, LANES), lambda i: (i, 0, 0), memory_space=pltpu.SMEM),
                  pl.BlockSpec((tp * SUBLANES, LANES), lambda i: (i, 0))],
        out_specs=pl.BlockSpec(memory_space=pl.ANY),
        out_shape=jax.ShapeDtypeStruct((n_rows * SUBLANES, LANES), F32),
        scratch_shapes=[pltpu.SemaphoreType.DMA(())],
        compiler_params=pltpu.CompilerParams(dimension_semantics=("arbitrary",), has_side_effects=True),
    )(pos, h2)


def _gmm_kernel(it_ref, iq_ref, lo_ref, hi_ref, qe_ref, n_ref, nq_ref, xs_hbm, wi_hbm, wo_hbm, ys_ref,
                wi_s, wo_s, a_s, acc_s, wi_f, wo_f, xs_buf, sem_i, sem_o, sem_x, *, layer):
    w = pl.program_id(0)
    f = wo_s.shape[0]
    tm = acc_s.shape[0]
    n = n_ref[0]

    def weight_copies(q, slot):
        e = qe_ref[q]
        return (pltpu.make_async_copy(wi_hbm.at[layer, e], wi_f.at[slot], sem_i.at[slot]),
                pltpu.make_async_copy(wo_hbm.at[layer, e], wo_f.at[slot], sem_o.at[slot]))

    def rows_copy(j):
        slot = lax.rem(j, XS_RING)
        src = xs_hbm.at[pl.ds(pl.multiple_of(it_ref[j] * (tm * SUBLANES), tm * SUBLANES), tm * SUBLANES)]
        return pltpu.make_async_copy(src, xs_buf.at[slot], sem_x.at[slot])

    @pl.when(w <= n)
    def _():
        ja = jnp.minimum(w, n - 1)
        jb = jnp.maximum(w - 1, 0)
        slot_a = lax.rem(w, 2)
        qa = iq_ref[ja]
        qb = iq_ref[jb]

        @pl.when(w == 0)
        def _():
            for c in weight_copies(0, 0):
                c.start()
            for j in range(XS_RING - 1):
                @pl.when(j < n)
                def _():
                    rows_copy(j).start()
            a_s[1] = jnp.zeros(a_s.shape[1:], BF16)
            acc_s[...] = jnp.zeros(acc_s.shape, F32)
            wo_s[...] = jnp.zeros(wo_s.shape, BF16)

        @pl.when(w + XS_RING - 1 < n)
        def _():
            rows_copy(w + XS_RING - 1).start()

        @pl.when(w < n)
        def _():
            rows_copy(w).wait()

        @pl.when(jnp.logical_or(w == 1, jnp.logical_and(w >= 2, qb != iq_ref[jnp.maximum(w - 2, 0)])))
        def _():
            slot = lax.rem(qb, 2)
            weight_copies(qb, slot)[1].wait()
            wo_s[...] = wo_f[slot].astype(BF16)

        @pl.when(jnp.logical_or(w == 0, qa != qb))
        def _():
            @pl.when(qa + 1 < nq_ref[0])
            def _():
                for c in weight_copies(qa + 1, lax.rem(qa + 1, 2)):
                    c.start()

            slot = lax.rem(qa, 2)
            weight_copies(qa, slot)[0].wait()
            wi_s[...] = wi_f[slot].astype(BF16)

        y = _dot(a_s[1 - slot_a], wo_s[...])
        row = lax.broadcasted_iota(jnp.int32, (tm, 1), 0)
        mine = jnp.logical_and(jnp.logical_and(row >= lo_ref[jb], row < hi_ref[jb]), w >= 1)
        merged = jnp.where(mine, y, acc_s[...])
        acc_s[...] = merged
        _store_token_tiles(ys_ref, merged)

        xs_tile = xs_buf.at[lax.rem(ja, XS_RING)]
        x = jnp.concatenate([_load_token_tiles(xs_tile, tm, s).astype(BF16) for s in range(SUBLANES)], axis=-1)
        gu = _dot(x, wi_s[...])
        gate, up = gu[:, :f], gu[:, f:]
        a_s[slot_a] = (gate * (1.0 / (1.0 + jnp.exp(-gate))) * up).astype(BF16)


def _grouped_mlp(plan, xs, w_in, w_out, layer):
    item_tile, item_seq, item_lo, item_hi, seq_expert, n_items, n_seq = plan
    d, f2 = w_in.shape[-2:]
    f = f2 // 2
    tm = GMM_TILE
    rows = tm * SUBLANES

    def second(w, n):
        return jnp.minimum(jnp.maximum(w - 1, 0), n[0] - 1)

    return pl.pallas_call(
        functools.partial(_gmm_kernel, layer=layer),
        grid_spec=pltpu.PrefetchScalarGridSpec(
            num_scalar_prefetch=7,
            grid=(item_tile.shape[0] + 1,),
            in_specs=[pl.BlockSpec(memory_space=pl.ANY)] * 3,
            out_specs=pl.BlockSpec((rows, LANES), lambda w, it, iq, lo, hi, qe, n, nq: (it[second(w, n)], 0)),
            scratch_shapes=[pltpu.VMEM((d, f2), BF16), pltpu.VMEM((f, d), BF16), pltpu.VMEM((2, tm, f), BF16),
                            pltpu.VMEM((tm, d), F32), pltpu.VMEM((2, d, f2), F32), pltpu.VMEM((2, f, d), F32),
                            pltpu.VMEM((XS_RING, rows, LANES), F32),
                            pltpu.SemaphoreType.DMA((2,)), pltpu.SemaphoreType.DMA((2,)),
                            pltpu.SemaphoreType.DMA((XS_RING,))],
        ),
        out_shape=jax.ShapeDtypeStruct(xs.shape, F32),
        compiler_params=pltpu.CompilerParams(
            dimension_semantics=("arbitrary",), vmem_limit_bytes=VMEM_LIMIT),
    )(item_tile, item_seq, item_lo, item_hi, seq_expert, n_items, n_seq, xs, w_in, w_out)


def _combine_kernel(pos0_ref, pos1_ref, pos2_ref, x1_ref, wt_ref, mod_ref, fg_ref, ys_ref, o_ref, *scratch, final):
    bufs, sems = scratch[:COMBINE_RING], scratch[COMBINE_RING]
    i = pl.program_id(0)
    n = pl.num_programs(0)
    tp = x1_ref.shape[0]

    def issue(idx_ref, slot):
        def one(k, row, lane, token):
            pltpu.make_async_copy(ys_ref.at[_tile_rows(idx_ref[0, row, lane])], bufs[slot].at[k, _tile_rows(token)],
                                  sems.at[slot]).start(priority=k)
        _for_each_token(tp, one)

    def compute(slot):
        wt = wt_ref[...]
        gate2 = mod_ref[0, 5:6]
        chunks = []
        for s in range(SUBLANES):
            rows = pl.ds(s, tp, stride=SUBLANES)
            cols = slice(s * LANES, (s + 1) * LANES)
            y = wt[:, 0:1] * bufs[slot][0, rows, :] + wt[:, 1:2] * bufs[slot][1, rows, :]
            chunks.append(x1_ref[:, cols] + gate2[:, cols] * y)
        x = jnp.concatenate(chunks, axis=-1)
        if final:
            x = x * lax.rsqrt(jnp.mean(x * x, axis=-1, keepdims=True) + NORM_EPS) * fg_ref[...]
        o_ref[...] = x

    @pl.when(i == 0)
    def _():
        issue(pos0_ref, 0)

        @pl.when(n > 1)
        def _():
            issue(pos1_ref, 1)

    for slot in range(COMBINE_RING):
        @pl.when(lax.rem(i, COMBINE_RING) == slot)
        def _():
            for k in range(MOE_TOP_E):
                pltpu.make_async_copy(ys_ref.at[pl.ds(0, tp * SUBLANES)], bufs[slot].at[k], sems.at[slot]).wait()

            @pl.when(i + 2 < n)
            def _():
                issue(pos2_ref, (slot + 2) % COMBINE_RING)
                compute(slot)

            @pl.when(i + 2 >= n)
            def _():
                compute(slot)


def _combine(pos, x1, wts_t, mod, final_g, ys, seq_len, final):
    nt, idx_rows, _ = pos.shape
    t, d = x1.shape
    tp = t // nt
    per_seq = seq_len // tp
    idx_spec = lambda ahead: pl.BlockSpec((1, idx_rows, LANES), lambda i: (jnp.minimum(i + ahead, nt - 1), 0, 0),
                                          memory_space=pltpu.SMEM)
    return pl.pallas_call(
        functools.partial(_combine_kernel, final=final),
        grid=(nt,),
        in_specs=[
            idx_spec(0), idx_spec(1), idx_spec(2),
            pl.BlockSpec((tp, d), lambda i: (i, 0)),
            pl.BlockSpec((tp, MOE_TOP_E), lambda i: (i, 0)),
            pl.BlockSpec((1, 6, d), lambda i: (i // per_seq, 0, 0)),
            pl.BlockSpec((1, d), lambda i: (0, 0)),
            pl.BlockSpec(memory_space=pl.ANY),
        ],
        out_specs=pl.BlockSpec((tp, d), lambda i: (i, 0)),
        out_shape=jax.ShapeDtypeStruct((t, d), F32),
        scratch_shapes=[pltpu.VMEM((MOE_TOP_E, tp * SUBLANES, LANES), F32)] * COMBINE_RING
        + [pltpu.SemaphoreType.DMA((COMBINE_RING,))],
        compiler_params=pltpu.CompilerParams(
            dimension_semantics=("arbitrary",), vmem_limit_bytes=VMEM_LIMIT),
    )(pos, pos, pos, x1, wts_t, mod, final_g, ys)


def _router_weights(w_group, b_group, w_expert, b_expert):
    d, g = w_group.shape
    e = w_expert.shape[1]
    w = jnp.concatenate([w_group, w_expert], axis=1).T
    w = jnp.pad(w, ((0, ROUTER_ROWS - g - e), (0, 0)))
    hi, lo = _split_bf16(w)
    bias = jnp.pad(jnp.concatenate([b_group, b_expert]), (0, ROUTER_ROWS - g - e)).reshape(ROUTER_ROWS, 1)
    return hi, lo, bias


def _tile_plan(hist, n_rows):
    n_e = hist.shape[0]
    tm = GMM_TILE
    counts = jnp.sum(hist, axis=1).astype(jnp.int32)
    ends = jnp.cumsum(counts)
    starts = ends - counts
    first_tile = starts // tm
    items_per = jnp.where(counts > 0, (ends - 1) // tm - first_tile + 1, 0)
    item_ends = jnp.cumsum(items_per)
    n_items = item_ends[-1:]
    max_items = n_rows // tm + n_e - 1
    w = jnp.minimum(jnp.arange(max_items, dtype=jnp.int32), n_items[0] - 1)
    item_expert = jnp.sum(w[:, None] >= item_ends[None, :], axis=1).astype(jnp.int32)
    item_tile = first_tile[item_expert] + w - (item_ends - items_per)[item_expert]
    item_lo = jnp.maximum(starts[item_expert] - item_tile * tm, 0)
    item_hi = jnp.minimum(ends[item_expert] - item_tile * tm, tm)
    offsets = starts.astype(F32).reshape(n_e, 1)
    used_ends = jnp.cumsum((counts > 0).astype(jnp.int32))
    n_seq = used_ends[-1:]
    q = jnp.minimum(jnp.arange(n_e, dtype=jnp.int32), n_seq[0] - 1)
    seq_expert = jnp.sum(q[:, None] >= used_ends[None, :], axis=1).astype(jnp.int32)
    item_seq = (used_ends - 1)[item_expert]
    return offsets, (item_tile.astype(jnp.int32), item_seq.astype(jnp.int32), item_lo.astype(jnp.int32),
                     item_hi.astype(jnp.int32), seq_expert, n_items.astype(jnp.int32), n_seq.astype(jnp.int32))


def kernel(x, c, norm_gain, w_mod, b_mod, pool_w, pool_b, pool_scale, gla_w_in, gla_w_gate, gla_b_gate, gla_norm_g, gla_w_out, moe_w_group, moe_b_group, moe_w_expert, moe_b_expert, moe_w_in, moe_w_out, final_norm_g):
    b, s, d = x.shape
    depth = w_mod.shape[0]
    t = b * s
    n_groups = moe_w_group.shape[-1]
    n_experts = moe_w_expert.shape[-1]
    n_rows = MOE_TOP_E * t
    mod_all = _modulation(c, w_mod, b_mod).reshape(depth, b, 6, d)
    fg = final_norm_g.reshape(1, d)
    for i in range(depth):
        mod = mod_all[i]
        wr_hi, wr_lo, br = _router_weights(moe_w_group[i], moe_b_group[i], moe_w_expert[i], moe_b_expert[i])
        j = i // 2
        if i % 2 == 0:
            x1, h2, ids, wts, hist = _pool_layer(x, mod, norm_gain[i], pool_w[j], pool_b[j], pool_scale[j],
                                                 wr_hi, wr_lo, br, n_groups, n_experts)
        else:
            x1, h2, ids, wts, hist = _gla_layer(x, mod, norm_gain[i], gla_w_in[j], gla_w_gate[j], gla_b_gate[j],
                                                gla_norm_g[j], gla_w_out[j], wr_hi, wr_lo, br, n_groups, n_experts)
        offsets, plan = _tile_plan(hist, n_rows)
        pos = _positions(ids, offsets)
        xs = _dispatch(pos, h2, n_rows)
        ys = _grouped_mlp(plan, xs, moe_w_in, moe_w_out, i)
        x = _combine(pos, x1.reshape(t, d), wts.T, mod, fg, ys, s, final=(i == depth - 1)).reshape(b, s, d)
    return x
```

```python
import functools

import jax
import jax.numpy as jnp
from jax import lax
from jax.experimental import pallas as pl
from jax.experimental.pallas import tpu as pltpu

F32 = jnp.float32
BF16 = jnp.bfloat16

NORM_EPS = 1e-6
POOL_WINDOWS = (2, 4, 8, 16)
POOL_HALO = 16
GLA_CHUNK = 64
GLA_GATE_TEMP = 16.0
MOE_TOP_E = 2
LANES = 128
SUBLANES = 8
ROUTER_ROWS = 128
VMEM_LIMIT = 56 * 1024 * 1024

SEQ_TILE_POOL = 512
SEQ_TILE_GLA = 512
GMM_TILE = 256
XS_RING = 3
COMBINE_RING = 3
ROW_TILE = 512


def _dot(a, b):
    return jnp.dot(a, b, preferred_element_type=F32)


def _dot_nt(a, b):
    return lax.dot_general(a, b, (((1,), (1,)), ((), ())), preferred_element_type=F32)


def _dot_tn(a, b):
    return lax.dot_general(a, b, (((0,), (0,)), ((), ())), preferred_element_type=F32)


def _split_bf16(x):
    hi = x.astype(BF16)
    lo = (x - hi.astype(F32)).astype(BF16)
    return hi, lo


def _store_token_tiles(ref, x):
    n = x.shape[0]
    for s in range(SUBLANES):
        ref[pl.ds(s, n, stride=SUBLANES), :] = x[:, s * LANES:(s + 1) * LANES]


def _load_token_tiles(ref, n, s):
    return ref[pl.ds(s, n, stride=SUBLANES), :]


def _norm_mod(x, gain, scale, shift):
    ms = jnp.mean(x * x, axis=-1, keepdims=True)
    return x * lax.rsqrt(ms + NORM_EPS) * gain * (1.0 + scale) + shift


def _mod_kernel(c_ref, w_ref, b_ref, o_ref):
    c = c_ref[...]
    sc = (c * (1.0 / (1.0 + jnp.exp(-c)))).astype(BF16)
    o_ref[0] = _dot(sc, w_ref[0].astype(BF16)) + b_ref[0]


def _modulation(c, w_mod, b_mod):
    depth, d, n = w_mod.shape
    b = c.shape[0]
    tn = n // 4
    return pl.pallas_call(
        _mod_kernel,
        grid=(depth, n // tn),
        in_specs=[
            pl.BlockSpec((b, d), lambda i, j: (0, 0)),
            pl.BlockSpec((1, d, tn), lambda i, j: (i, 0, j)),
            pl.BlockSpec((1, 1, tn), lambda i, j: (i, 0, j)),
        ],
        out_specs=pl.BlockSpec((1, b, tn), lambda i, j: (i, 0, j)),
        out_shape=jax.ShapeDtypeStruct((depth, b, n), F32),
        compiler_params=pltpu.CompilerParams(vmem_limit_bytes=VMEM_LIMIT),
    )(c, w_mod, b_mod.reshape(depth, 1, n))


def _router_epilogue(x1, mod, gain2, wr_hi_ref, wr_lo_ref, br_ref, n_groups, n_experts,
                     h2_ref, ids_ref, wts_ref, hist_ref, first_step):
    h2 = _norm_mod(x1, gain2, mod[4:5], mod[3:4])
    _store_token_tiles(h2_ref, h2)
    hh, hl = _split_bf16(h2)
    wh = wr_hi_ref[...]
    lt = _dot_nt(wh, hh) + _dot_nt(wh, hl) + _dot_nt(wr_lo_ref[...], hh) + br_ref[...]
    ts = lt.shape[1]
    per = n_experts // n_groups
    lg = lt[0:n_groups]
    mg = jnp.max(lg, axis=0, keepdims=True)
    p_g = 1.0 / jnp.sum(jnp.exp(lg - mg), axis=0, keepdims=True)
    gi = lax.broadcasted_iota(jnp.int32, lg.shape, 0)
    g_idx = jnp.min(jnp.where(lg == mg, gi, n_groups), axis=0, keepdims=True)
    sel = jnp.zeros((per, ts), F32)
    for g in range(n_groups):
        sel = jnp.where(g_idx == g, lt[n_groups + g * per:n_groups + (g + 1) * per], sel)
    ei = lax.broadcasted_iota(jnp.int32, sel.shape, 0)
    m1 = jnp.max(sel, axis=0, keepdims=True)
    i1 = jnp.min(jnp.where(sel == m1, ei, per), axis=0, keepdims=True)
    rest = jnp.where(ei == i1, -jnp.inf, sel)
    m2 = jnp.max(rest, axis=0, keepdims=True)
    i2 = jnp.min(jnp.where(rest == m2, ei, per), axis=0, keepdims=True)
    e21 = jnp.exp(m2 - m1)
    w1 = p_g / (1.0 + e21)
    w2 = p_g * e21 / (1.0 + e21)
    id1 = g_idx * per + i1
    id2 = g_idx * per + i2
    ids_ref[...] = jnp.concatenate([id1, id2], axis=0)
    wts_ref[...] = jnp.concatenate([w1, w2], axis=0)
    xi = lax.broadcasted_iota(jnp.int32, (n_experts, ts), 0)
    cnt = (xi == id1).astype(F32) + (xi == id2).astype(F32)

    @pl.when(first_step)
    def _():
        hist_ref[...] = cnt

    @pl.when(jnp.logical_not(first_step))
    def _():
        hist_ref[...] += cnt


def _router_specs(d, ts, n_seq_tiles, n_experts):
    in_specs = [
        pl.BlockSpec((ROUTER_ROWS, d), lambda b, s: (0, 0)),
        pl.BlockSpec((ROUTER_ROWS, d), lambda b, s: (0, 0)),
        pl.BlockSpec((ROUTER_ROWS, 1), lambda b, s: (0, 0)),
    ]
    out_specs = [
        pl.BlockSpec((1, ts, d), lambda b, s: (b, s, 0)),
        pl.BlockSpec((ts * SUBLANES, LANES), lambda b, s: (b * n_seq_tiles + s, 0)),
        pl.BlockSpec((MOE_TOP_E, ts), lambda b, s: (0, b * n_seq_tiles + s)),
        pl.BlockSpec((MOE_TOP_E, ts), lambda b, s: (0, b * n_seq_tiles + s)),
        pl.BlockSpec((n_experts, ts), lambda b, s: (0, 0)),
    ]
    return in_specs, out_specs


def _router_out_shapes(b, s, d, ts, n_experts):
    assert d == SUBLANES * LANES, "token-per-tile row layout needs d_model == one (8,128) tile"
    return [
        jax.ShapeDtypeStruct((b, s, d), F32),
        jax.ShapeDtypeStruct((b * s * SUBLANES, LANES), F32),
        jax.ShapeDtypeStruct((MOE_TOP_E, b * s), jnp.int32),
        jax.ShapeDtypeStruct((MOE_TOP_E, b * s), F32),
        jax.ShapeDtypeStruct((n_experts, ts), F32),
    ]


def _moe_specs(moe, ts, n_seq_tiles, d):
    pos = moe[0]
    nt, idx_rows, _ = pos.shape
    assert idx_rows * LANES // MOE_TOP_E == ts, "row tile of the index blocks must equal the mixer's sequence tile"
    tile = lambda bb, ss: bb * n_seq_tiles + ss
    idx_spec = lambda ahead: pl.BlockSpec(
        (1, idx_rows, LANES), lambda bb, ss: (jnp.minimum(tile(bb, ss) + ahead, nt - 1), 0, 0),
        memory_space=pltpu.SMEM)
    in_specs = [idx_spec(0), idx_spec(1),
                pl.BlockSpec((ts, MOE_TOP_E), lambda bb, ss: (tile(bb, ss), 0)),
                pl.BlockSpec((1, 6, d), lambda bb, ss: (bb, 0, 0)),
                pl.BlockSpec(memory_space=pl.ANY)]
    scratch = [pltpu.VMEM((MOE_TOP_E, ts * SUBLANES, LANES), F32),
               pltpu.SemaphoreType.DMA(()),
               pltpu.VMEM((ts, d), F32)]
    return in_specs, scratch


N_MOE_REFS = 5


def _moe_args(moe):
    return (moe[0],) * 2 + tuple(moe[1:]) if moe else ()


def _moe_combine(j, moe_refs, x1p_ref, scratch):
    pos0_ref, pos1_ref, wtp_ref, modp_ref, ysp_ref = moe_refs
    buf, sem, x_s = scratch
    tp = x_s.shape[0]

    def copy(idx_ref, k, row, lane, token):
        return pltpu.make_async_copy(ysp_ref.at[_tile_rows(idx_ref[0, row, lane])],
                                     buf.at[k, _tile_rows(token)], sem)

    def wait():
        for k in range(MOE_TOP_E):
            pltpu.make_async_copy(ysp_ref.at[pl.ds(0, tp * SUBLANES)], buf.at[k], sem).wait()

    @pl.when(j == 0)
    def _():
        _for_each_token_looped(tp, lambda k, row, lane, token: copy(pos0_ref, k, row, lane, token).start(priority=k))

    wait()
    wt = wtp_ref[...]
    gate2 = modp_ref[0, 5:6]
    for s in range(SUBLANES):
        rows = pl.ds(s, tp, stride=SUBLANES)
        cols = slice(s * LANES, (s + 1) * LANES)
        y = wt[:, 0:1] * buf[0, rows, :] + wt[:, 1:2] * buf[1, rows, :]
        x_s[:, cols] = x1p_ref[0, :, cols] + gate2[:, cols] * y

    def issue_ahead():
        _for_each_token(tp, lambda k, row, lane, token: copy(pos1_ref, k, row, lane, token).start(priority=k))

    return issue_ahead, wait


def _pool_kernel(*refs, n_groups, n_experts, fused, n_tiles):
    moe_refs, refs = (refs[:N_MOE_REFS], refs[N_MOE_REFS:]) if fused else (None, refs)
    (x_ref, mod_ref, gain_ref, pw_ref, pb_ref, ps_ref, wr_hi_ref, wr_lo_ref, br_ref,
     x1_ref, h2_ref, ids_ref, wts_ref, hist_ref, hbuf_ref, sa_ref, sb_ref) = refs[:17]
    b, s = pl.program_id(0), pl.program_id(1)
    if fused:
        tile = b * pl.num_programs(1) + s
        issue_ahead, drain = _moe_combine(tile, moe_refs, x_ref, refs[17:])
        x = refs[19][...]
    else:
        x = x_ref[0]
    ts, d = x.shape
    mod = mod_ref[0]
    h = _norm_mod(x, gain_ref[0:1], mod[1:2], mod[0:1])

    first = SUBLANES + POOL_HALO
    n = first + ts

    @pl.when(s == 0)
    def _():
        hbuf_ref[0:first] = jnp.zeros((first, d), F32)
        sa_ref[0:SUBLANES] = jnp.zeros((SUBLANES, d), F32)
        sb_ref[0:SUBLANES] = jnp.zeros((SUBLANES, d), F32)

    if fused:
        issue_ahead()
    hbuf_ref[first:n] = h
    pos = (s * ts + 1 + lax.broadcasted_iota(jnp.int32, (ts, 1), 0)).astype(F32)
    cg = d // len(POOL_WINDOWS)
    src, dst = hbuf_ref, sa_ref
    ys = []
    for gi, win in enumerate(POOL_WINDOWS):
        half = win // 2
        assert win == 2 ** (gi + 1) and win <= POOL_HALO
        lo = gi * cg
        dst[SUBLANES:n, lo:] = src[SUBLANES:n, lo:] + src[SUBLANES - half:n - half, lo:]
        cols = slice(lo, lo + cg)
        dgi = dst[first:n, cols] / jnp.minimum(pos, float(win)) - h[:, cols]
        ys.append(_dot(dgi.astype(BF16), pw_ref[gi]))
        src, dst = dst, (sb_ref if dst is sa_ref else sa_ref)
    y = (jnp.concatenate(ys, axis=-1) + pb_ref[...]) * ps_ref[...]
    hbuf_ref[SUBLANES:first] = hbuf_ref[n - POOL_HALO:n]
    x1 = x + mod[2:3] * y
    x1_ref[0] = x1
    _router_epilogue(x1, mod, gain_ref[1:2], wr_hi_ref, wr_lo_ref, br_ref, n_groups, n_experts,
                     h2_ref, ids_ref, wts_ref, hist_ref, jnp.logical_and(b == 0, s == 0))
    if fused:
        pl.when(tile == n_tiles - 1)(drain)


def _pool_layer(x, moe, mod, gain, pw, pb, ps, wr_hi, wr_lo, br, n_groups, n_experts):
    b, s, d = x.shape
    ts = min(SEQ_TILE_POOL, s)
    nst = s // ts
    g, cg = pw.shape[0], pw.shape[1]
    r_in, r_out = _router_specs(d, ts, nst, n_experts)
    m_in, m_scratch = _moe_specs(moe, ts, nst, d) if moe else ([], [])
    return pl.pallas_call(
        functools.partial(_pool_kernel, n_groups=n_groups, n_experts=n_experts, fused=bool(moe), n_tiles=b * nst),
        grid=(b, nst),
        in_specs=m_in + [
            pl.BlockSpec((1, ts, d), lambda bb, ss: (bb, ss, 0)),
            pl.BlockSpec((1, 6, d), lambda bb, ss: (bb, 0, 0)),
            pl.BlockSpec((2, d), lambda bb, ss: (0, 0)),
            pl.BlockSpec((g, cg, cg), lambda bb, ss: (0, 0, 0)),
            pl.BlockSpec((1, d), lambda bb, ss: (0, 0)),
            pl.BlockSpec((1, d), lambda bb, ss: (0, 0)),
        ] + r_in,
        out_specs=r_out,
        out_shape=_router_out_shapes(b, s, d, ts, n_experts),
        scratch_shapes=[pltpu.VMEM((SUBLANES + POOL_HALO + ts, d), F32)] * 3 + m_scratch,
        compiler_params=pltpu.CompilerParams(
            dimension_semantics=("arbitrary", "arbitrary"), vmem_limit_bytes=VMEM_LIMIT),
    )(*_moe_args(moe), x, mod, gain, pw.astype(BF16), pb.reshape(1, d), ps.reshape(1, d), wr_hi, wr_lo, br)


def _gla_kernel(*refs, n_heads, n_groups, n_experts, fused, n_tiles):
    moe_refs, refs = (refs[:N_MOE_REFS], refs[N_MOE_REFS:]) if fused else (None, refs)
    (x_ref, mod_ref, gain_ref, wp_ref, wz_ref, wg_ref, bg_ref, ng_ref, wo_ref, wr_hi_ref, wr_lo_ref, br_ref,
     x1_ref, h2_ref, ids_ref, wts_ref, hist_ref, state_ref, q_s, k_s, g_s, v_s, r_s, o_s) = refs[:24]
    b, s = pl.program_id(0), pl.program_id(1)
    if fused:
        tile = b * pl.num_programs(1) + s
        issue_ahead, drain = _moe_combine(tile, moe_refs, x_ref, refs[24:])
        x = refs[26][...]
    else:
        x = x_ref[0]
    ts, d = x.shape
    dk_all = q_s.shape[1]
    dk = dk_all // n_heads
    dv = d // n_heads
    mod = mod_ref[0]
    h = _norm_mod(x, gain_ref[0:1], mod[1:2], mod[0:1]).astype(BF16)

    @pl.when(s == 0)
    def _():
        state_ref[...] = jnp.zeros(state_ref.shape, F32)

    if fused:
        issue_ahead()
    q_s[...] = _dot(h, wp_ref[:, 0:dk_all]) * (dk ** -0.5)
    k_s[...] = _dot(h, wp_ref[:, dk_all:2 * dk_all])
    v_s[...] = _dot(h, wp_ref[:, 2 * dk_all:2 * dk_all + d]).astype(BF16)
    r = _dot(h, wp_ref[:, 2 * dk_all + d:2 * dk_all + 2 * d])
    r_s[...] = r * (1.0 / (1.0 + jnp.exp(-r)))
    z = _dot(h, wz_ref[...]).astype(BF16)
    u = _dot(z, wg_ref[...]) + bg_ref[...]
    g_s[...] = (jnp.minimum(u, 0.0) - jnp.log(1.0 + jnp.exp(-jnp.abs(u)))) * (1.0 / GLA_GATE_TEMP)

    c = GLA_CHUNK
    ri = lax.broadcasted_iota(jnp.int32, (c, c), 0)
    ci = lax.broadcasted_iota(jnp.int32, (c, c), 1)
    causal = ri >= ci
    tril = jnp.where(causal, 1.0, 0.0).astype(BF16)
    ng = ng_ref[...]

    def chunk(n, carry):
        rows = pl.ds(pl.multiple_of(n * c, c), c)
        g_hi, g_lo = _split_bf16(g_s[rows, :])
        bc = _dot(tril, g_hi) + _dot(tril, g_lo)
        bl = bc[c - 1:c, :]
        q_e = (q_s[rows, :] * jnp.exp(bc)).astype(BF16)
        kk = k_s[rows, :]
        k_e = (kk * jnp.exp(-bc)).astype(BF16)
        k_d = (kk * jnp.exp(bl - bc)).astype(BF16)
        e_l = jnp.exp(bl)
        vv = v_s[rows, :]
        for hd in range(n_heads):
            ks = slice(hd * dk, (hd + 1) * dk)
            vs = slice(hd * dv, (hd + 1) * dv)
            sc = jnp.where(causal, _dot_nt(q_e[:, ks], k_e[:, ks]), 0.0).astype(BF16)
            st = state_ref[hd]
            o = _dot(sc, vv[:, vs]) + _dot_nt(q_e[:, ks], st.astype(BF16))
            state_ref[hd] = st * e_l[:, ks] + _dot_tn(vv[:, vs], k_d[:, ks])
            o = o * lax.rsqrt(jnp.mean(o * o, axis=-1, keepdims=True) + NORM_EPS) * ng
            o_s[rows, vs] = (o * r_s[rows, vs]).astype(BF16)
        return carry

    lax.fori_loop(0, ts // c, chunk, 0, unroll=4)
    x1 = x + mod[2:3] * _dot(o_s[...], wo_ref[...])
    x1_ref[0] = x1
    _router_epilogue(x1, mod, gain_ref[1:2], wr_hi_ref, wr_lo_ref, br_ref, n_groups, n_experts,
                     h2_ref, ids_ref, wts_ref, hist_ref, jnp.logical_and(b == 0, s == 0))
    if fused:
        pl.when(tile == n_tiles - 1)(drain)


def _gla_layer(x, moe, mod, gain, w_in, w_gate, b_gate, norm_g, w_out, wr_hi, wr_lo, br, n_groups, n_experts):
    b, s, d = x.shape
    ts = min(SEQ_TILE_GLA, s)
    nst = s // ts
    rank, dk_all = w_gate.shape
    dv = norm_g.shape[0]
    n_heads = d // dv
    n_proj = 2 * dk_all + 2 * d
    wp = w_in[:, :n_proj].astype(BF16)
    wz = jnp.pad(w_in[:, n_proj:], ((0, 0), (0, LANES - rank))).astype(BF16)
    wg = jnp.pad(w_gate, ((0, LANES - rank), (0, 0))).astype(BF16)
    r_in, r_out = _router_specs(d, ts, nst, n_experts)
    m_in, m_scratch = _moe_specs(moe, ts, nst, d) if moe else ([], [])
    const = lambda bb, ss: (0, 0)
    once = pl.Buffered(1)
    return pl.pallas_call(
        functools.partial(_gla_kernel, n_heads=n_heads, n_groups=n_groups, n_experts=n_experts, fused=bool(moe),
                          n_tiles=b * nst),
        grid=(b, nst),
        in_specs=m_in + [
            pl.BlockSpec((1, ts, d), lambda bb, ss: (bb, ss, 0)),
            pl.BlockSpec((1, 6, d), lambda bb, ss: (bb, 0, 0)),
            pl.BlockSpec((2, d), const),
            pl.BlockSpec((d, n_proj), const, pipeline_mode=once),
            pl.BlockSpec((d, LANES), const),
            pl.BlockSpec((LANES, dk_all), const),
            pl.BlockSpec((1, dk_all), const),
            pl.BlockSpec((1, dv), const),
            pl.BlockSpec((d, d), const, pipeline_mode=once),
        ] + r_in,
        out_specs=r_out,
        out_shape=_router_out_shapes(b, s, d, ts, n_experts),
        scratch_shapes=[
            pltpu.VMEM((n_heads, dv, dk_all // n_heads), F32),
            pltpu.VMEM((ts, dk_all), F32),
            pltpu.VMEM((ts, dk_all), F32),
            pltpu.VMEM((ts, dk_all), F32),
            pltpu.VMEM((ts, d), BF16),
            pltpu.VMEM((ts, d), F32),
            pltpu.VMEM((ts, d), BF16),
        ] + m_scratch,
        compiler_params=pltpu.CompilerParams(
            dimension_semantics=("arbitrary", "arbitrary"), vmem_limit_bytes=VMEM_LIMIT),
    )(*_moe_args(moe), x, mod, gain, wp, wz, wg, b_gate.reshape(1, dk_all), norm_g.reshape(1, dv), w_out.astype(BF16),
      wr_hi, wr_lo, br)


def _pos_kernel(ids_ref, off_ref, pos_ref, run_ref):
    i = pl.program_id(0)
    ids = ids_ref[...]
    tp = ids.shape[1]
    n_e = off_ref.shape[0]

    @pl.when(i == 0)
    def _():
        run_ref[...] = jnp.zeros(run_ref.shape, F32)

    ei = lax.broadcasted_iota(jnp.int32, (n_e, tp), 0)
    oh0 = ei == ids[0:1]
    oh1 = ei == ids[1:2]
    tri = jnp.where(lax.broadcasted_iota(jnp.int32, (tp, tp), 0) <= lax.broadcasted_iota(jnp.int32, (tp, tp), 1),
                    1.0, 0.0).astype(BF16)
    inc0 = _dot(jnp.where(oh0, 1.0, 0.0).astype(BF16), tri)
    inc1 = _dot(jnp.where(oh1, 1.0, 0.0).astype(BF16), tri)
    tot0 = inc0[:, tp - 1:tp]
    tot1 = inc1[:, tp - 1:tp]
    base = off_ref[...] + run_ref[...]
    p0 = jnp.sum(jnp.where(oh0, base + inc0 - 1.0, 0.0), axis=0, keepdims=True)
    p1 = jnp.sum(jnp.where(oh1, base + tot0 + inc1 - 1.0, 0.0), axis=0, keepdims=True)
    p = [pk[:, c * LANES:(c + 1) * LANES] for pk in (p0, p1) for c in range(tp // LANES)]
    pos_ref[0] = jnp.concatenate(p, axis=0).astype(jnp.int32)
    run_ref[...] += tot0 + tot1


def _positions(ids, offsets):
    t = ids.shape[1]
    tp = min(ROW_TILE, t)
    n_e = offsets.shape[0]
    return pl.pallas_call(
        _pos_kernel,
        grid=(t // tp,),
        in_specs=[pl.BlockSpec((MOE_TOP_E, tp), lambda i: (0, i)),
                  pl.BlockSpec((n_e, 1), lambda i: (0, 0))],
        out_specs=pl.BlockSpec((1, MOE_TOP_E * tp // LANES, LANES), lambda i: (i, 0, 0)),
        out_shape=jax.ShapeDtypeStruct((t // tp, MOE_TOP_E * tp // LANES, LANES), jnp.int32),
        scratch_shapes=[pltpu.VMEM((n_e, 1), F32)],
        compiler_params=pltpu.CompilerParams(dimension_semantics=("arbitrary",)),
    )(ids, offsets)


def _tile_rows(row):
    return pl.ds(pl.multiple_of(row * SUBLANES, SUBLANES), SUBLANES)


def _for_each_token(tp, fn):
    chunks = tp // LANES
    for c in range(chunks):
        for lane in range(LANES):
            for k in range(MOE_TOP_E):
                fn(k, k * chunks + c, lane, c * LANES + lane)


def _for_each_token_looped(tp, fn):
    chunks = tp // LANES
    for c in range(chunks):
        def body(lane, carry):
            for k in range(MOE_TOP_E):
                fn(k, k * chunks + c, lane, c * LANES + lane)
            return carry
        lax.fori_loop(0, LANES, body, 0)


def _dispatch_kernel(pos_ref, h_ref, xs_ref, sem):
    tp = h_ref.shape[0] // SUBLANES

    def issue(k, row, lane, token):
        pltpu.make_async_copy(h_ref.at[_tile_rows(token)], xs_ref.at[_tile_rows(pos_ref[0, row, lane])],
                              sem).start(priority=k)

    _for_each_token(tp, issue)
    for k in range(MOE_TOP_E):
        pltpu.make_async_copy(h_ref, xs_ref.at[pl.ds(0, tp * SUBLANES)], sem).wait()


def _dispatch(pos, h2, n_rows):
    nt, idx_rows, _ = pos.shape
    tp = idx_rows * LANES // MOE_TOP_E
    return pl.pallas_call(
        _dispatch_kernel,
        grid=(nt,),
        in_specs=[pl.BlockSpec((1, idx_rows, LANES), lambda i: (i, 0, 0), memory_space=pltpu.SMEM),
                  pl.BlockSpec((tp * SUBLANES, LANES), lambda i: (i, 0))],
        out_specs=pl.BlockSpec(memory_space=pl.ANY),
        out_shape=jax.ShapeDtypeStruct((n_rows * SUBLANES, LANES), F32),
        scratch_shapes=[pltpu.SemaphoreType.DMA(())],
        compiler_params=pltpu.CompilerParams(dimension_semantics=("arbitrary",), has_side_effects=True),
    )(pos, h2)


def _gmm_kernel(it_ref, iq_ref, lo_ref, hi_ref, qe_ref, n_ref, nq_ref, xs_hbm, wi_hbm, wo_hbm, ys_ref,
                wi_s, wo_s, a_s, acc_s, wi_f, wo_f, xs_buf, sem_i, sem_o, sem_x, *, layer):
    w = pl.program_id(0)
    f = wo_s.shape[0]
    tm = acc_s.shape[0]
    n = n_ref[0]

    def wi_copy(q, slot):
        return pltpu.make_async_copy(wi_hbm.at[layer, qe_ref[q]], wi_f.at[slot], sem_i.at[slot])

    def wo_copy(q, slot):
        return pltpu.make_async_copy(wo_hbm.at[layer, qe_ref[q]], wo_f.at[slot], sem_o.at[slot])

    def rows_copy(j):
        slot = lax.rem(j, XS_RING)
        src = xs_hbm.at[pl.ds(pl.multiple_of(it_ref[j] * (tm * SUBLANES), tm * SUBLANES), tm * SUBLANES)]
        return pltpu.make_async_copy(src, xs_buf.at[slot], sem_x.at[slot])

    @pl.when(w <= n)
    def _():
        ja = jnp.minimum(w, n - 1)
        jb = jnp.maximum(w - 1, 0)
        slot_a = lax.rem(w, 2)
        qa = iq_ref[ja]
        qb = iq_ref[jb]

        @pl.when(w == 0)
        def _():
            wi_copy(0, 0).start()
            wo_copy(0, 0).start()
            for j in range(XS_RING - 1):
                @pl.when(j < n)
                def _():
                    rows_copy(j).start()
            a_s[1] = jnp.zeros(a_s.shape[1:], BF16)
            acc_s[...] = jnp.zeros(acc_s.shape, F32)
            wo_s[...] = jnp.zeros(wo_s.shape, BF16)

        @pl.when(w + XS_RING - 1 < n)
        def _():
            rows_copy(w + XS_RING - 1).start()

        @pl.when(w < n)
        def _():
            rows_copy(w).wait()

        @pl.when(jnp.logical_or(w == 1, jnp.logical_and(w >= 2, qb != iq_ref[jnp.maximum(w - 2, 0)])))
        def _():
            slot = lax.rem(qb, 2)
            wo_copy(qb, slot).wait()
            wo_s[...] = wo_f[slot].astype(BF16)

        @pl.when(jnp.logical_or(w == 0, qa != qb))
        def _():
            @pl.when(qa + 1 < nq_ref[0])
            def _():
                wi_copy(qa + 1, lax.rem(qa + 1, 2)).start()
                wo_copy(qa + 1, lax.rem(qa + 1, 2)).start()

            slot = lax.rem(qa, 2)
            wi_copy(qa, slot).wait()
            wi_s[...] = wi_f[slot].astype(BF16)

        y = _dot(a_s[1 - slot_a], wo_s[...])
        row = lax.broadcasted_iota(jnp.int32, (tm, 1), 0)
        mine = jnp.logical_and(jnp.logical_and(row >= lo_ref[jb], row < hi_ref[jb]), w >= 1)
        merged = jnp.where(mine, y, acc_s[...])
        acc_s[...] = merged
        _store_token_tiles(ys_ref, merged)

        xs_tile = xs_buf.at[lax.rem(ja, XS_RING)]
        x = jnp.concatenate([_load_token_tiles(xs_tile, tm, s).astype(BF16) for s in range(SUBLANES)], axis=-1)
        gu = _dot(x, wi_s[...])
        gate, up = gu[:, :f], gu[:, f:]
        a_s[slot_a] = (gate * (1.0 / (1.0 + jnp.exp(-gate))) * up).astype(BF16)


def _grouped_mlp(plan, xs, w_in, w_out, layer):
    item_tile, item_seq, item_lo, item_hi, seq_expert, n_items, n_seq = plan
    d, f2 = w_in.shape[-2:]
    f = f2 // 2
    tm = GMM_TILE
    rows = tm * SUBLANES

    def second(w, n):
        return jnp.minimum(jnp.maximum(w - 1, 0), n[0] - 1)

    return pl.pallas_call(
        functools.partial(_gmm_kernel, layer=layer),
        grid_spec=pltpu.PrefetchScalarGridSpec(
            num_scalar_prefetch=7,
            grid=(item_tile.shape[0] + 1,),
            in_specs=[pl.BlockSpec(memory_space=pl.ANY)] * 3,
            out_specs=pl.BlockSpec((rows, LANES), lambda w, it, iq, lo, hi, qe, n, nq: (it[second(w, n)], 0)),
            scratch_shapes=[pltpu.VMEM((d, f2), BF16), pltpu.VMEM((f, d), BF16), pltpu.VMEM((2, tm, f), BF16),
                            pltpu.VMEM((tm, d), F32), pltpu.VMEM((2, d, f2), F32), pltpu.VMEM((2, f, d), F32),
                            pltpu.VMEM((XS_RING, rows, LANES), F32),
                            pltpu.SemaphoreType.DMA((2,)), pltpu.SemaphoreType.DMA((2,)),
                            pltpu.SemaphoreType.DMA((XS_RING,))],
        ),
        out_shape=jax.ShapeDtypeStruct(xs.shape, F32),
        compiler_params=pltpu.CompilerParams(
            dimension_semantics=("arbitrary",), vmem_limit_bytes=VMEM_LIMIT),
    )(item_tile, item_seq, item_lo, item_hi, seq_expert, n_items, n_seq, xs, w_in, w_out)


def _combine_kernel(pos0_ref, pos1_ref, pos2_ref, x1_ref, wt_ref, mod_ref, fg_ref, ys_ref, o_ref, *scratch, final):
    bufs, sems = scratch[:COMBINE_RING], scratch[COMBINE_RING]
    i = pl.program_id(0)
    n = pl.num_programs(0)
    tp = x1_ref.shape[0]

    def issue(idx_ref, slot):
        def one(k, row, lane, token):
            pltpu.make_async_copy(ys_ref.at[_tile_rows(idx_ref[0, row, lane])], bufs[slot].at[k, _tile_rows(token)],
                                  sems.at[slot]).start(priority=k)
        _for_each_token(tp, one)

    def compute(slot):
        wt = wt_ref[...]
        gate2 = mod_ref[0, 5:6]
        chunks = []
        for s in range(SUBLANES):
            rows = pl.ds(s, tp, stride=SUBLANES)
            cols = slice(s * LANES, (s + 1) * LANES)
            y = wt[:, 0:1] * bufs[slot][0, rows, :] + wt[:, 1:2] * bufs[slot][1, rows, :]
            chunks.append(x1_ref[:, cols] + gate2[:, cols] * y)
        x = jnp.concatenate(chunks, axis=-1)
        if final:
            x = x * lax.rsqrt(jnp.mean(x * x, axis=-1, keepdims=True) + NORM_EPS) * fg_ref[...]
        o_ref[...] = x

    @pl.when(i == 0)
    def _():
        issue(pos0_ref, 0)

        @pl.when(n > 1)
        def _():
            issue(pos1_ref, 1)

    for slot in range(COMBINE_RING):
        @pl.when(lax.rem(i, COMBINE_RING) == slot)
        def _():
            for k in range(MOE_TOP_E):
                pltpu.make_async_copy(ys_ref.at[pl.ds(0, tp * SUBLANES)], bufs[slot].at[k], sems.at[slot]).wait()

            @pl.when(i + 2 < n)
            def _():
                issue(pos2_ref, (slot + 2) % COMBINE_RING)
                compute(slot)

            @pl.when(i + 2 >= n)
            def _():
                compute(slot)


def _combine(pos, x1, wts_t, mod, final_g, ys, seq_len, final):
    nt, idx_rows, _ = pos.shape
    t, d = x1.shape
    tp = t // nt
    per_seq = seq_len // tp
    idx_spec = lambda ahead: pl.BlockSpec((1, idx_rows, LANES), lambda i: (jnp.minimum(i + ahead, nt - 1), 0, 0),
                                          memory_space=pltpu.SMEM)
    return pl.pallas_call(
        functools.partial(_combine_kernel, final=final),
        grid=(nt,),
        in_specs=[
            idx_spec(0), idx_spec(1), idx_spec(2),
            pl.BlockSpec((tp, d), lambda i: (i, 0)),
            pl.BlockSpec((tp, MOE_TOP_E), lambda i: (i, 0)),
            pl.BlockSpec((1, 6, d), lambda i: (i // per_seq, 0, 0)),
            pl.BlockSpec((1, d), lambda i: (0, 0)),
            pl.BlockSpec(memory_space=pl.ANY),
        ],
        out_specs=pl.BlockSpec((tp, d), lambda i: (i, 0)),
        out_shape=jax.ShapeDtypeStruct((t, d), F32),
        scratch_shapes=[pltpu.VMEM((MOE_TOP_E, tp * SUBLANES, LANES), F32)] * COMBINE_RING
        + [pltpu.SemaphoreType.DMA((COMBINE_RING,))],
        compiler_params=pltpu.CompilerParams(
            dimension_semantics=("arbitrary",), vmem_limit_bytes=VMEM_LIMIT),
    )(pos, pos, pos, x1, wts_t, mod, final_g, ys)


def _router_weights(w_group, b_group, w_expert, b_expert):
    d, g = w_group.shape
    e = w_expert.shape[1]
    w = jnp.concatenate([w_group, w_expert], axis=1).T
    w = jnp.pad(w, ((0, ROUTER_ROWS - g - e), (0, 0)))
    hi, lo = _split_bf16(w)
    bias = jnp.pad(jnp.concatenate([b_group, b_expert]), (0, ROUTER_ROWS - g - e)).reshape(ROUTER_ROWS, 1)
    return hi, lo, bias


def _tile_plan(hist, n_rows):
    n_e = hist.shape[0]
    tm = GMM_TILE
    counts = jnp.sum(hist, axis=1).astype(jnp.int32)
    ends = jnp.cumsum(counts)
    starts = ends - counts
    first_tile = starts // tm
    items_per = jnp.where(counts > 0, (ends - 1) // tm - first_tile + 1, 0)
    item_ends = jnp.cumsum(items_per)
    n_items = item_ends[-1:]
    max_items = n_rows // tm + n_e - 1
    w = jnp.minimum(jnp.arange(max_items, dtype=jnp.int32), n_items[0] - 1)
    item_expert = jnp.sum(w[:, None] >= item_ends[None, :], axis=1).astype(jnp.int32)
    item_tile = first_tile[item_expert] + w - (item_ends - items_per)[item_expert]
    item_lo = jnp.maximum(starts[item_expert] - item_tile * tm, 0)
    item_hi = jnp.minimum(ends[item_expert] - item_tile * tm, tm)
    offsets = starts.astype(F32).reshape(n_e, 1)
    used_ends = jnp.cumsum((counts > 0).astype(jnp.int32))
    n_seq = used_ends[-1:]
    q = jnp.minimum(jnp.arange(n_e, dtype=jnp.int32), n_seq[0] - 1)
    seq_expert = jnp.sum(q[:, None] >= used_ends[None, :], axis=1).astype(jnp.int32)
    item_seq = (used_ends - 1)[item_expert]
    return offsets, (item_tile.astype(jnp.int32), item_seq.astype(jnp.int32), item_lo.astype(jnp.int32),
                     item_hi.astype(jnp.int32), seq_expert, n_items.astype(jnp.int32), n_seq.astype(jnp.int32))


def kernel(x, c, norm_gain, w_mod, b_mod, pool_w, pool_b, pool_scale, gla_w_in, gla_w_gate, gla_b_gate, gla_norm_g, gla_w_out, moe_w_group, moe_b_group, moe_w_expert, moe_b_expert, moe_w_in, moe_w_out, final_norm_g):
    b, s, d = x.shape
    depth = w_mod.shape[0]
    t = b * s
    n_groups = moe_w_group.shape[-1]
    n_experts = moe_w_expert.shape[-1]
    n_rows = MOE_TOP_E * t
    mod_all = _modulation(c, w_mod, b_mod).reshape(depth, b, 6, d)
    fg = final_norm_g.reshape(1, d)
    moe = None
    for i in range(depth):
        mod = mod_all[i]
        wr_hi, wr_lo, br = _router_weights(moe_w_group[i], moe_b_group[i], moe_w_expert[i], moe_b_expert[i])
        j = i // 2
        if i % 2 == 0:
            x1, h2, ids, wts, hist = _pool_layer(x, moe, mod, norm_gain[i], pool_w[j], pool_b[j], pool_scale[j],
                                                 wr_hi, wr_lo, br, n_groups, n_experts)
        else:
            x1, h2, ids, wts, hist = _gla_layer(x, moe, mod, norm_gain[i], gla_w_in[j], gla_w_gate[j], gla_b_gate[j],
                                                gla_norm_g[j], gla_w_out[j], wr_hi, wr_lo, br, n_groups, n_experts)
        offsets, plan = _tile_plan(hist, n_rows)
        pos = _positions(ids, offsets)
        xs = _dispatch(pos, h2, n_rows)
        ys = _grouped_mlp(plan, xs, moe_w_in, moe_w_out, i)
        x, moe = x1, (pos, wts.T, mod, ys)
    pos, wts_t, mod, ys = moe
    x = _combine(pos, x.reshape(t, d), wts_t, mod, fg, ys, s, final=True).reshape(b, s, d)
    return x
```

```python
import functools

import jax
import jax.numpy as jnp
from jax import lax
from jax.experimental import pallas as pl
from jax.experimental.pallas import tpu as pltpu

F32 = jnp.float32
BF16 = jnp.bfloat16

NORM_EPS = 1e-6
POOL_WINDOWS = (2, 4, 8, 16)
POOL_HALO = 16
GLA_CHUNK = 64
GLA_GATE_TEMP = 16.0
GLA_CHUNK_GROUP = 4
MOE_TOP_E = 2
LANES = 128
SUBLANES = 8
ROUTER_ROWS = 128
VMEM_LIMIT = 56 * 1024 * 1024

SEQ_TILE_POOL = 512
SEQ_TILE_GLA = 512
GMM_TILE = 256
XS_RING = 3
COMBINE_RING = 3
ROW_TILE = 512
POS_TILES_PER_STEP = 4


def _dot(a, b):
    return jnp.dot(a, b, preferred_element_type=F32)


def _dot_nt(a, b):
    return lax.dot_general(a, b, (((1,), (1,)), ((), ())), preferred_element_type=F32)


def _dot_tn(a, b):
    return lax.dot_general(a, b, (((0,), (0,)), ((), ())), preferred_element_type=F32)


def _split_bf16(x):
    hi = x.astype(BF16)
    lo = (x - hi.astype(F32)).astype(BF16)
    return hi, lo


def _store_token_tiles(ref, x):
    n = x.shape[0]
    for s in range(SUBLANES):
        ref[pl.ds(s, n, stride=SUBLANES), :] = x[:, s * LANES:(s + 1) * LANES]


def _load_token_tiles(ref, n, s):
    return ref[pl.ds(s, n, stride=SUBLANES), :]


def _norm_mod(x, gain, scale, shift):
    ms = jnp.mean(x * x, axis=-1, keepdims=True)
    return x * lax.rsqrt(ms + NORM_EPS) * (gain * (1.0 + scale)) + shift


def _mod_kernel(c_ref, w_ref, b_ref, o_ref):
    c = c_ref[...]
    sc = (c * (1.0 / (1.0 + jnp.exp(-c)))).astype(BF16)
    o_ref[0] = _dot(sc, w_ref[0].astype(BF16)) + b_ref[0]


def _modulation(c, w_mod, b_mod):
    depth, d, n = w_mod.shape
    b = c.shape[0]
    tn = n // 4
    return pl.pallas_call(
        _mod_kernel,
        grid=(depth, n // tn),
        in_specs=[
            pl.BlockSpec((b, d), lambda i, j: (0, 0)),
            pl.BlockSpec((1, d, tn), lambda i, j: (i, 0, j)),
            pl.BlockSpec((1, 1, tn), lambda i, j: (i, 0, j)),
        ],
        out_specs=pl.BlockSpec((1, b, tn), lambda i, j: (i, 0, j)),
        out_shape=jax.ShapeDtypeStruct((depth, b, n), F32),
        compiler_params=pltpu.CompilerParams(vmem_limit_bytes=VMEM_LIMIT),
    )(c, w_mod, b_mod.reshape(depth, 1, n))


def _router_epilogue(x1, mod, gain2, wr_hi_ref, wr_lo_ref, br_ref, n_groups, n_experts,
                     h2_ref, ids_ref, wts_ref, hist_ref, first_step):
    h2 = _norm_mod(x1, gain2, mod[4:5], mod[3:4])
    _store_token_tiles(h2_ref, h2)
    hh, hl = _split_bf16(h2)
    wh = wr_hi_ref[0]
    lt = _dot_nt(wh, hh) + _dot_nt(wh, hl) + _dot_nt(wr_lo_ref[0], hh) + br_ref[0]
    ts = lt.shape[1]
    per = n_experts // n_groups
    lg = lt[0:n_groups]
    mg = jnp.max(lg, axis=0, keepdims=True)
    p_g = 1.0 / jnp.sum(jnp.exp(lg - mg), axis=0, keepdims=True)
    gi = lax.broadcasted_iota(jnp.int32, lg.shape, 0)
    g_idx = jnp.min(jnp.where(lg == mg, gi, n_groups), axis=0, keepdims=True)
    sel = jnp.zeros((per, ts), F32)
    for g in range(n_groups):
        sel = jnp.where(g_idx == g, lt[n_groups + g * per:n_groups + (g + 1) * per], sel)
    ei = lax.broadcasted_iota(jnp.int32, sel.shape, 0)
    m1 = jnp.max(sel, axis=0, keepdims=True)
    i1 = jnp.min(jnp.where(sel == m1, ei, per), axis=0, keepdims=True)
    rest = jnp.where(ei == i1, -jnp.inf, sel)
    m2 = jnp.max(rest, axis=0, keepdims=True)
    i2 = jnp.min(jnp.where(rest == m2, ei, per), axis=0, keepdims=True)
    e21 = jnp.exp(m2 - m1)
    w1 = p_g / (1.0 + e21)
    w2 = p_g * e21 / (1.0 + e21)
    id1 = g_idx * per + i1
    id2 = g_idx * per + i2
    ids_ref[...] = jnp.concatenate([id1, id2], axis=0)
    wts_ref[...] = jnp.concatenate([w1, w2], axis=0)
    xi = lax.broadcasted_iota(jnp.int32, (n_experts, ts), 0)
    cnt = (xi == id1).astype(F32) + (xi == id2).astype(F32)

    @pl.when(first_step)
    def _():
        hist_ref[...] = cnt

    @pl.when(jnp.logical_not(first_step))
    def _():
        hist_ref[...] += cnt


def _router_specs(d, ts, n_seq_tiles, n_experts, layer):
    in_specs = [
        pl.BlockSpec((1, ROUTER_ROWS, d), lambda b, s: (layer, 0, 0)),
        pl.BlockSpec((1, ROUTER_ROWS, d), lambda b, s: (layer, 0, 0)),
        pl.BlockSpec((1, ROUTER_ROWS, 1), lambda b, s: (layer, 0, 0)),
    ]
    out_specs = [
        pl.BlockSpec((1, ts, d), lambda b, s: (b, s, 0)),
        pl.BlockSpec((ts * SUBLANES, LANES), lambda b, s: (b * n_seq_tiles + s, 0)),
        pl.BlockSpec((MOE_TOP_E, ts), lambda b, s: (0, b * n_seq_tiles + s)),
        pl.BlockSpec((MOE_TOP_E, ts), lambda b, s: (0, b * n_seq_tiles + s)),
        pl.BlockSpec((n_experts, ts), lambda b, s: (0, 0)),
    ]
    return in_specs, out_specs


def _router_out_shapes(b, s, d, ts, n_experts):
    assert d == SUBLANES * LANES, "token-per-tile row layout needs d_model == one (8,128) tile"
    return [
        jax.ShapeDtypeStruct((b, s, d), F32),
        jax.ShapeDtypeStruct((b * s * SUBLANES, LANES), F32),
        jax.ShapeDtypeStruct((MOE_TOP_E, b * s), jnp.int32),
        jax.ShapeDtypeStruct((MOE_TOP_E, b * s), F32),
        jax.ShapeDtypeStruct((n_experts, ts), F32),
    ]


def _moe_specs(moe, ts, n_seq_tiles, d):
    pos = moe[0]
    nt, idx_rows, _ = pos.shape
    assert idx_rows * LANES // MOE_TOP_E == ts, "row tile of the index blocks must equal the mixer's sequence tile"
    tile = lambda bb, ss: bb * n_seq_tiles + ss
    idx_spec = lambda ahead: pl.BlockSpec(
        (1, idx_rows, LANES), lambda bb, ss: (jnp.minimum(tile(bb, ss) + ahead, nt - 1), 0, 0),
        memory_space=pltpu.SMEM)
    in_specs = [idx_spec(0), idx_spec(1),
                pl.BlockSpec((ts, MOE_TOP_E), lambda bb, ss: (tile(bb, ss), 0)),
                pl.BlockSpec((1, 6, d), lambda bb, ss: (bb, 0, 0)),
                pl.BlockSpec(memory_space=pl.ANY)]
    scratch = [pltpu.VMEM((MOE_TOP_E, ts * SUBLANES, LANES), F32),
               pltpu.SemaphoreType.DMA(()),
               pltpu.VMEM((ts, d), F32)]
    return in_specs, scratch


N_MOE_REFS = 5


def _moe_args(moe):
    return (moe[0],) * 2 + tuple(moe[1:]) if moe else ()


def _moe_combine(j, moe_refs, x1p_ref, scratch):
    pos0_ref, pos1_ref, wtp_ref, modp_ref, ysp_ref = moe_refs
    buf, sem, x_s = scratch
    tp = x_s.shape[0]

    def copy(idx_ref, k, row, lane, token):
        return pltpu.make_async_copy(ysp_ref.at[_tile_rows(idx_ref[0, row, lane])],
                                     buf.at[k, _tile_rows(token)], sem)

    def wait():
        for k in range(MOE_TOP_E):
            pltpu.make_async_copy(ysp_ref.at[pl.ds(0, tp * SUBLANES)], buf.at[k], sem).wait()

    @pl.when(j == 0)
    def _():
        _for_each_token_looped(tp, lambda k, row, lane, token: copy(pos0_ref, k, row, lane, token).start(priority=k))

    wait()
    wt = wtp_ref[...]
    gate2 = modp_ref[0, 5:6]
    for s in range(SUBLANES):
        rows = pl.ds(s, tp, stride=SUBLANES)
        cols = slice(s * LANES, (s + 1) * LANES)
        y = wt[:, 0:1] * buf[0, rows, :] + wt[:, 1:2] * buf[1, rows, :]
        x_s[:, cols] = x1p_ref[0, :, cols] + gate2[:, cols] * y

    def issue_ahead():
        _for_each_token(tp, lambda k, row, lane, token: copy(pos1_ref, k, row, lane, token).start(priority=k))

    return issue_ahead, wait


def _pool_kernel(*refs, n_groups, n_experts, fused, n_tiles):
    moe_refs, refs = (refs[:N_MOE_REFS], refs[N_MOE_REFS:]) if fused else (None, refs)
    (x_ref, mod_ref, gain_ref, pw_ref, pb_ref, ps_ref, wr_hi_ref, wr_lo_ref, br_ref,
     x1_ref, h2_ref, ids_ref, wts_ref, hist_ref, hbuf_ref, sa_ref, sb_ref) = refs[:17]
    b, s = pl.program_id(0), pl.program_id(1)
    if fused:
        tile = b * pl.num_programs(1) + s
        issue_ahead, drain = _moe_combine(tile, moe_refs, x_ref, refs[17:])
        x = refs[19][...]
    else:
        x = x_ref[0]
    ts, d = x.shape
    mod = mod_ref[0]
    h = _norm_mod(x, gain_ref[0:1], mod[1:2], mod[0:1])

    first = SUBLANES + POOL_HALO
    n = first + ts

    @pl.when(s == 0)
    def _():
        hbuf_ref[0:first] = jnp.zeros((first, d), F32)
        sa_ref[0:SUBLANES] = jnp.zeros((SUBLANES, d), F32)
        sb_ref[0:SUBLANES] = jnp.zeros((SUBLANES, d), F32)

    if fused:
        issue_ahead()
    hbuf_ref[first:n] = h
    pos = (s * ts + 1 + lax.broadcasted_iota(jnp.int32, (ts, 1), 0)).astype(F32)
    cg = d // len(POOL_WINDOWS)
    src, dst = hbuf_ref, sa_ref
    ys = []
    for gi, win in enumerate(POOL_WINDOWS):
        half = win // 2
        assert win == 2 ** (gi + 1) and win <= POOL_HALO
        lo = gi * cg
        dst[SUBLANES:n, lo:] = src[SUBLANES:n, lo:] + src[SUBLANES - half:n - half, lo:]
        cols = slice(lo, lo + cg)
        dgi = dst[first:n, cols] / jnp.minimum(pos, float(win)) - h[:, cols]
        ys.append(_dot(dgi.astype(BF16), pw_ref[gi]))
        src, dst = dst, (sb_ref if dst is sa_ref else sa_ref)
    hbuf_ref[SUBLANES:first] = hbuf_ref[n - POOL_HALO:n]
    x1 = x + (jnp.concatenate(ys, axis=-1) + pb_ref[...]) * (ps_ref[...] * mod[2:3])
    x1_ref[0] = x1
    _router_epilogue(x1, mod, gain_ref[1:2], wr_hi_ref, wr_lo_ref, br_ref, n_groups, n_experts,
                     h2_ref, ids_ref, wts_ref, hist_ref, jnp.logical_and(b == 0, s == 0))
    if fused:
        pl.when(tile == n_tiles - 1)(drain)


def _pool_layer(x, moe, mod, gain, pw, pb, ps, router, layer, n_groups, n_experts):
    b, s, d = x.shape
    ts = min(SEQ_TILE_POOL, s)
    nst = s // ts
    g, cg = pw.shape[0], pw.shape[1]
    r_in, r_out = _router_specs(d, ts, nst, n_experts, layer)
    m_in, m_scratch = _moe_specs(moe, ts, nst, d) if moe else ([], [])
    return pl.pallas_call(
        functools.partial(_pool_kernel, n_groups=n_groups, n_experts=n_experts, fused=bool(moe), n_tiles=b * nst),
        grid=(b, nst),
        in_specs=m_in + [
            pl.BlockSpec((1, ts, d), lambda bb, ss: (bb, ss, 0)),
            pl.BlockSpec((1, 6, d), lambda bb, ss: (bb, 0, 0)),
            pl.BlockSpec((2, d), lambda bb, ss: (0, 0)),
            pl.BlockSpec((g, cg, cg), lambda bb, ss: (0, 0, 0)),
            pl.BlockSpec((1, d), lambda bb, ss: (0, 0)),
            pl.BlockSpec((1, d), lambda bb, ss: (0, 0)),
        ] + r_in,
        out_specs=r_out,
        out_shape=_router_out_shapes(b, s, d, ts, n_experts),
        scratch_shapes=[pltpu.VMEM((SUBLANES + POOL_HALO + ts, d), F32)] * 3 + m_scratch,
        compiler_params=pltpu.CompilerParams(
            dimension_semantics=("arbitrary", "arbitrary"), vmem_limit_bytes=VMEM_LIMIT),
    )(*_moe_args(moe), x, mod, gain, pw.astype(BF16), pb.reshape(1, d), ps.reshape(1, d), *router)


N_GLA_REFS = 28


def _gla_kernel(*refs, n_heads, n_groups, n_experts, fused, n_tiles):
    moe_refs, refs = (refs[:N_MOE_REFS], refs[N_MOE_REFS:]) if fused else (None, refs)
    (x_ref, mod_ref, gain_ref, wp_ref, wz_ref, wg_ref, bg_ref, ng_ref, wo_ref, wr_hi_ref, wr_lo_ref, br_ref,
     x1_ref, h2_ref, ids_ref, wts_ref, hist_ref, state_ref, q_s, k_s, g_s, v_s, r_s, o_s,
     qe_s, kd_s, el_s, oi_s) = refs[:N_GLA_REFS]
    b, s = pl.program_id(0), pl.program_id(1)
    if fused:
        tile = b * pl.num_programs(1) + s
        issue_ahead, drain = _moe_combine(tile, moe_refs, x_ref, refs[N_GLA_REFS:])
        x = refs[N_GLA_REFS + 2][...]
    else:
        x = x_ref[0]
    ts, d = x.shape
    dk_all = q_s.shape[1]
    dk = dk_all // n_heads
    dv = d // n_heads
    mod = mod_ref[0]
    h = _norm_mod(x, gain_ref[0:1], mod[1:2], mod[0:1]).astype(BF16)

    @pl.when(s == 0)
    def _():
        state_ref[...] = jnp.zeros(state_ref.shape, F32)

    if fused:
        issue_ahead()
    q_s[...] = _dot(h, wp_ref[:, 0:dk_all]) * (dk ** -0.5)
    k_s[...] = _dot(h, wp_ref[:, dk_all:2 * dk_all])
    v_s[...] = _dot(h, wp_ref[:, 2 * dk_all:2 * dk_all + d]).astype(BF16)
    r = _dot(h, wp_ref[:, 2 * dk_all + d:2 * dk_all + 2 * d])
    r_s[...] = r * (1.0 / (1.0 + jnp.exp(-r)))
    z = _dot(h, wz_ref[...]).astype(BF16)
    u = _dot(z, wg_ref[...]) + bg_ref[...]
    g_s[...] = (jnp.minimum(u, 0.0) - jnp.log(1.0 + jnp.exp(-jnp.abs(u)))) * (1.0 / GLA_GATE_TEMP)

    c = GLA_CHUNK
    ri = lax.broadcasted_iota(jnp.int32, (c, c), 0)
    ci = lax.broadcasted_iota(jnp.int32, (c, c), 1)
    causal = ri >= ci
    tril = jnp.where(causal, 1.0, 0.0).astype(BF16)
    ng = ng_ref[...]

    heads = [(slice(hd * dk, (hd + 1) * dk), slice(hd * dv, (hd + 1) * dv)) for hd in range(n_heads)]
    group = min(GLA_CHUNK_GROUP, ts // c)

    def intra(m, carry):
        ns = [m * group + i for i in range(group)]
        rows = [pl.ds(pl.multiple_of(n * c, c), c) for n in ns]
        gs = [_split_bf16(g_s[r, :]) for r in rows]
        bcs = [_dot(tril, hi) + _dot(tril, lo) for hi, lo in gs]
        qes, kes = [], []
        for n, r, bc in zip(ns, rows, bcs):
            bl = bc[c - 1:c, :]
            kk = k_s[r, :]
            qes.append((q_s[r, :] * jnp.exp(bc)).astype(BF16))
            kes.append((kk * jnp.exp(-bc)).astype(BF16))
            qe_s[r, :] = qes[-1]
            kd_s[r, :] = (kk * jnp.exp(bl - bc)).astype(BF16)
            el_s[pl.ds(n, 1), :] = jnp.exp(bl)
        scs = [[_dot_nt(q_e[:, ks], k_e[:, ks]) for ks, _ in heads] for q_e, k_e in zip(qes, kes)]
        scs = [[jnp.where(causal, sc, 0.0).astype(BF16) for sc in per_chunk] for per_chunk in scs]
        for r, per_chunk in zip(rows, scs):
            vv = v_s[r, :]
            for sc, (_, vs) in zip(per_chunk, heads):
                oi_s[r, vs] = _dot(sc, vv[:, vs])
        return carry

    lax.fori_loop(0, ts // c // group, intra, 0)

    def inter(m, carry):
        ns = [m * group + i for i in range(group)]
        rows = [pl.ds(pl.multiple_of(n * c, c), c) for n in ns]
        incs = []
        for r in rows:
            vv, k_d = v_s[r, :], kd_s[r, :]
            incs.append([_dot_tn(vv[:, vs], k_d[:, ks]) for ks, vs in heads])
        for n, r, inc in zip(ns, rows, incs):
            q_e = qe_s[r, :]
            e_l = el_s[pl.ds(n, 1), :]
            sts = [state_ref[hd] for hd in range(n_heads)]
            carried = [_dot_nt(q_e[:, ks], st.astype(BF16)) for (ks, _), st in zip(heads, sts)]
            for hd, (ks, _) in enumerate(heads):
                state_ref[hd] = sts[hd] * e_l[:, ks] + inc[hd]
            for o_c, (_, vs) in zip(carried, heads):
                o = oi_s[r, vs] + o_c
                o = o * lax.rsqrt(jnp.mean(o * o, axis=-1, keepdims=True) + NORM_EPS) * ng
                o_s[r, vs] = (o * r_s[r, vs]).astype(BF16)
        return carry

    lax.fori_loop(0, ts // c // group, inter, 0)
    x1 = x + mod[2:3] * _dot(o_s[...], wo_ref[...])
    x1_ref[0] = x1
    _router_epilogue(x1, mod, gain_ref[1:2], wr_hi_ref, wr_lo_ref, br_ref, n_groups, n_experts,
                     h2_ref, ids_ref, wts_ref, hist_ref, jnp.logical_and(b == 0, s == 0))
    if fused:
        pl.when(tile == n_tiles - 1)(drain)


def _gla_layer(x, moe, mod, gain, w_in, w_gate, b_gate, norm_g, w_out, router, layer, n_groups, n_experts):
    b, s, d = x.shape
    ts = min(SEQ_TILE_GLA, s)
    nst = s // ts
    rank, dk_all = w_gate.shape
    dv = norm_g.shape[0]
    n_heads = d // dv
    n_proj = 2 * dk_all + 2 * d
    wp = w_in[:, :n_proj].astype(BF16)
    wz = jnp.pad(w_in[:, n_proj:], ((0, 0), (0, LANES - rank))).astype(BF16)
    wg = jnp.pad(w_gate, ((0, LANES - rank), (0, 0))).astype(BF16)
    r_in, r_out = _router_specs(d, ts, nst, n_experts, layer)
    m_in, m_scratch = _moe_specs(moe, ts, nst, d) if moe else ([], [])
    const = lambda bb, ss: (0, 0)
    once = pl.Buffered(1)
    return pl.pallas_call(
        functools.partial(_gla_kernel, n_heads=n_heads, n_groups=n_groups, n_experts=n_experts, fused=bool(moe),
                          n_tiles=b * nst),
        grid=(b, nst),
        in_specs=m_in + [
            pl.BlockSpec((1, ts, d), lambda bb, ss: (bb, ss, 0)),
            pl.BlockSpec((1, 6, d), lambda bb, ss: (bb, 0, 0)),
            pl.BlockSpec((2, d), const),
            pl.BlockSpec((d, n_proj), const, pipeline_mode=once),
            pl.BlockSpec((d, LANES), const),
            pl.BlockSpec((LANES, dk_all), const),
            pl.BlockSpec((1, dk_all), const),
            pl.BlockSpec((1, dv), const),
            pl.BlockSpec((d, d), const, pipeline_mode=once),
        ] + r_in,
        out_specs=r_out,
        out_shape=_router_out_shapes(b, s, d, ts, n_experts),
        scratch_shapes=[
            pltpu.VMEM((n_heads, dv, dk_all // n_heads), F32),
            pltpu.VMEM((ts, dk_all), F32),
            pltpu.VMEM((ts, dk_all), F32),
            pltpu.VMEM((ts, dk_all), F32),
            pltpu.VMEM((ts, d), BF16),
            pltpu.VMEM((ts, d), F32),
            pltpu.VMEM((ts, d), BF16),
            pltpu.VMEM((ts, dk_all), BF16),
            pltpu.VMEM((ts, dk_all), BF16),
            pltpu.VMEM((ts // GLA_CHUNK, dk_all), F32),
            pltpu.VMEM((ts, d), F32),
        ] + m_scratch,
        compiler_params=pltpu.CompilerParams(
            dimension_semantics=("arbitrary", "arbitrary"), vmem_limit_bytes=VMEM_LIMIT),
    )(*_moe_args(moe), x, mod, gain, wp, wz, wg, b_gate.reshape(1, dk_all), norm_g.reshape(1, dv), w_out.astype(BF16),
      *router)


def _pos_kernel(ids_ref, off_ref, pos_ref, run_ref):
    i = pl.program_id(0)
    tiles, idx_rows, _ = pos_ref.shape
    tp = idx_rows * LANES // MOE_TOP_E
    n_e = off_ref.shape[0]

    @pl.when(i == 0)
    def _():
        run_ref[...] = jnp.zeros(run_ref.shape, F32)

    ei = lax.broadcasted_iota(jnp.int32, (n_e, tp), 0)
    tri = jnp.where(lax.broadcasted_iota(jnp.int32, (tp, tp), 0) <= lax.broadcasted_iota(jnp.int32, (tp, tp), 1),
                    1.0, 0.0).astype(BF16)
    base = off_ref[...] + run_ref[...]
    for u in range(tiles):
        ids = ids_ref[:, u * tp:(u + 1) * tp]
        oh0 = ei == ids[0:1]
        oh1 = ei == ids[1:2]
        inc0 = _dot(jnp.where(oh0, 1.0, 0.0).astype(BF16), tri)
        inc1 = _dot(jnp.where(oh1, 1.0, 0.0).astype(BF16), tri)
        tot0 = inc0[:, tp - 1:tp]
        tot1 = inc1[:, tp - 1:tp]
        p0 = jnp.sum(jnp.where(oh0, base + inc0 - 1.0, 0.0), axis=0, keepdims=True)
        p1 = jnp.sum(jnp.where(oh1, base + tot0 + inc1 - 1.0, 0.0), axis=0, keepdims=True)
        p = [pk[:, c * LANES:(c + 1) * LANES] for pk in (p0, p1) for c in range(tp // LANES)]
        pos_ref[u] = jnp.concatenate(p, axis=0).astype(jnp.int32)
        base = base + tot0 + tot1
    run_ref[...] = base - off_ref[...]


def _positions(ids, offsets):
    t = ids.shape[1]
    tp = min(ROW_TILE, t)
    tiles = min(POS_TILES_PER_STEP, t // tp)
    n_e = offsets.shape[0]
    return pl.pallas_call(
        _pos_kernel,
        grid=(t // (tp * tiles),),
        in_specs=[pl.BlockSpec((MOE_TOP_E, tp * tiles), lambda i: (0, i)),
                  pl.BlockSpec((n_e, 1), lambda i: (0, 0))],
        out_specs=pl.BlockSpec((tiles, MOE_TOP_E * tp // LANES, LANES), lambda i: (i, 0, 0)),
        out_shape=jax.ShapeDtypeStruct((t // tp, MOE_TOP_E * tp // LANES, LANES), jnp.int32),
        scratch_shapes=[pltpu.VMEM((n_e, 1), F32)],
        compiler_params=pltpu.CompilerParams(dimension_semantics=("arbitrary",)),
    )(ids, offsets)


def _tile_rows(row):
    return pl.ds(pl.multiple_of(row * SUBLANES, SUBLANES), SUBLANES)


def _for_each_token(tp, fn):
    chunks = tp // LANES
    for c in range(chunks):
        for lane in range(LANES):
            for k in range(MOE_TOP_E):
                fn(k, k * chunks + c, lane, c * LANES + lane)


def _for_each_token_looped(tp, fn):
    chunks = tp // LANES
    for c in range(chunks):
        def body(lane, carry):
            for k in range(MOE_TOP_E):
                fn(k, k * chunks + c, lane, c * LANES + lane)
            return carry
        lax.fori_loop(0, LANES, body, 0)


def _dispatch_kernel(pos_ref, h_ref, xs_ref, sem):
    tp = h_ref.shape[0] // SUBLANES

    def issue(k, row, lane, token):
        pltpu.make_async_copy(h_ref.at[_tile_rows(token)], xs_ref.at[_tile_rows(pos_ref[0, row, lane])],
                              sem).start(priority=k)

    _for_each_token(tp, issue)
    for k in range(MOE_TOP_E):
        pltpu.make_async_copy(h_ref, xs_ref.at[pl.ds(0, tp * SUBLANES)], sem).wait()


def _dispatch(pos, h2, n_rows):
    nt, idx_rows, _ = pos.shape
    tp = idx_rows * LANES // MOE_TOP_E
    return pl.pallas_call(
        _dispatch_kernel,
        grid=(nt,),
        in_specs=[pl.BlockSpec((1, idx_rows, LANES), lambda i: (i, 0, 0), memory_space=pltpu.SMEM),
                  pl.BlockSpec((tp * SUBLANES, LANES), lambda i: (i, 0))],
        out_specs=pl.BlockSpec(memory_space=pl.ANY),
        out_shape=jax.ShapeDtypeStruct((n_rows * SUBLANES, LANES), F32),
        scratch_shapes=[pltpu.SemaphoreType.DMA(())],
        compiler_params=pltpu.CompilerParams(dimension_semantics=("arbitrary",), has_side_effects=True),
    )(pos, h2)


def _gmm_kernel(it_ref, iq_ref, lo_ref, hi_ref, qe_ref, n_ref, nq_ref, xs_hbm, wi_hbm, wo_hbm, ys_ref,
                wi_s, wo_s, a_s, acc_s, wi_f, wo_f, xs_buf, sem_i, sem_o, sem_x, *, layer):
    w = pl.program_id(0)
    f = wo_s.shape[0]
    tm = acc_s.shape[0]
    n = n_ref[0]

    def wi_copy(q, slot):
        return pltpu.make_async_copy(wi_hbm.at[layer, qe_ref[q]], wi_f.at[slot], sem_i.at[slot])

    def wo_copy(q, slot):
        return pltpu.make_async_copy(wo_hbm.at[layer, qe_ref[q]], wo_f.at[slot], sem_o.at[slot])

    def rows_copy(j):
        slot = lax.rem(j, XS_RING)
        src = xs_hbm.at[pl.ds(pl.multiple_of(it_ref[j] * (tm * SUBLANES), tm * SUBLANES), tm * SUBLANES)]
        return pltpu.make_async_copy(src, xs_buf.at[slot], sem_x.at[slot])

    @pl.when(w <= n)
    def _():
        ja = jnp.minimum(w, n - 1)
        jb = jnp.maximum(w - 1, 0)
        slot_a = lax.rem(w, 2)
        qa = iq_ref[ja]
        qb = iq_ref[jb]

        @pl.when(w == 0)
        def _():
            wi_copy(0, 0).start()
            wo_copy(0, 0).start()
            for j in range(XS_RING - 1):
                @pl.when(j < n)
                def _():
                    rows_copy(j).start()
            a_s[1] = jnp.zeros(a_s.shape[1:], BF16)
            acc_s[...] = jnp.zeros(acc_s.shape, F32)
            wo_s[...] = jnp.zeros(wo_s.shape, BF16)

        @pl.when(w + XS_RING - 1 < n)
        def _():
            rows_copy(w + XS_RING - 1).start()

        @pl.when(w < n)
        def _():
            rows_copy(w).wait()

        @pl.when(jnp.logical_or(w == 1, jnp.logical_and(w >= 2, qb != iq_ref[jnp.maximum(w - 2, 0)])))
        def _():
            slot = lax.rem(qb, 2)
            wo_copy(qb, slot).wait()
            wo_s[...] = wo_f[slot].astype(BF16)

        @pl.when(jnp.logical_or(w == 0, qa != qb))
        def _():
            @pl.when(qa + 1 < nq_ref[0])
            def _():
                wi_copy(qa + 1, lax.rem(qa + 1, 2)).start()
                wo_copy(qa + 1, lax.rem(qa + 1, 2)).start()

            slot = lax.rem(qa, 2)
            wi_copy(qa, slot).wait()
            wi_s[...] = wi_f[slot].astype(BF16)

        y = _dot(a_s[1 - slot_a], wo_s[...])
        row = lax.broadcasted_iota(jnp.int32, (tm, 1), 0)
        mine = jnp.logical_and(jnp.logical_and(row >= lo_ref[jb], row < hi_ref[jb]), w >= 1)
        merged = jnp.where(mine, y, acc_s[...])
        acc_s[...] = merged
        _store_token_tiles(ys_ref, merged)

        xs_tile = xs_buf.at[lax.rem(ja, XS_RING)]
        x = jnp.concatenate([_load_token_tiles(xs_tile, tm, s).astype(BF16) for s in range(SUBLANES)], axis=-1)
        gu = _dot(x, wi_s[...])
        gate, up = gu[:, :f], gu[:, f:]
        a_s[slot_a] = (gate * (1.0 / (1.0 + jnp.exp(-gate))) * up).astype(BF16)


def _grouped_mlp(plan, xs, w_in, w_out, layer):
    item_tile, item_seq, item_lo, item_hi, seq_expert, n_items, n_seq = plan
    d, f2 = w_in.shape[-2:]
    f = f2 // 2
    tm = GMM_TILE
    rows = tm * SUBLANES

    def second(w, n):
        return jnp.minimum(jnp.maximum(w - 1, 0), n[0] - 1)

    return pl.pallas_call(
        functools.partial(_gmm_kernel, layer=layer),
        grid_spec=pltpu.PrefetchScalarGridSpec(
            num_scalar_prefetch=7,
            grid=(item_tile.shape[0] + 1,),
            in_specs=[pl.BlockSpec(memory_space=pl.ANY)] * 3,
            out_specs=pl.BlockSpec((rows, LANES), lambda w, it, iq, lo, hi, qe, n, nq: (it[second(w, n)], 0)),
            scratch_shapes=[pltpu.VMEM((d, f2), BF16), pltpu.VMEM((f, d), BF16), pltpu.VMEM((2, tm, f), BF16),
                            pltpu.VMEM((tm, d), F32), pltpu.VMEM((2, d, f2), F32), pltpu.VMEM((2, f, d), F32),
                            pltpu.VMEM((XS_RING, rows, LANES), F32),
                            pltpu.SemaphoreType.DMA((2,)), pltpu.SemaphoreType.DMA((2,)),
                            pltpu.SemaphoreType.DMA((XS_RING,))],
        ),
        out_shape=jax.ShapeDtypeStruct(xs.shape, F32),
        compiler_params=pltpu.CompilerParams(
            dimension_semantics=("arbitrary",), vmem_limit_bytes=VMEM_LIMIT),
    )(item_tile, item_seq, item_lo, item_hi, seq_expert, n_items, n_seq, xs, w_in, w_out)


def _combine_kernel(pos0_ref, pos1_ref, pos2_ref, x1_ref, wt_ref, mod_ref, fg_ref, ys_ref, o_ref, *scratch, final):
    bufs, sems = scratch[:COMBINE_RING], scratch[COMBINE_RING]
    i = pl.program_id(0)
    n = pl.num_programs(0)
    tp = x1_ref.shape[0]

    def issue(idx_ref, slot):
        def one(k, row, lane, token):
            pltpu.make_async_copy(ys_ref.at[_tile_rows(idx_ref[0, row, lane])], bufs[slot].at[k, _tile_rows(token)],
                                  sems.at[slot]).start(priority=k)
        _for_each_token(tp, one)

    def compute(slot):
        wt = wt_ref[...]
        gate2 = mod_ref[0, 5:6]
        chunks = []
        for s in range(SUBLANES):
            rows = pl.ds(s, tp, stride=SUBLANES)
            cols = slice(s * LANES, (s + 1) * LANES)
            y = wt[:, 0:1] * bufs[slot][0, rows, :] + wt[:, 1:2] * bufs[slot][1, rows, :]
            chunks.append(x1_ref[:, cols] + gate2[:, cols] * y)
        x = jnp.concatenate(chunks, axis=-1)
        if final:
            x = x * lax.rsqrt(jnp.mean(x * x, axis=-1, keepdims=True) + NORM_EPS) * fg_ref[...]
        o_ref[...] = x

    @pl.when(i == 0)
    def _():
        issue(pos0_ref, 0)

        @pl.when(n > 1)
        def _():
            issue(pos1_ref, 1)

    for slot in range(COMBINE_RING):
        @pl.when(lax.rem(i, COMBINE_RING) == slot)
        def _():
            for k in range(MOE_TOP_E):
                pltpu.make_async_copy(ys_ref.at[pl.ds(0, tp * SUBLANES)], bufs[slot].at[k], sems.at[slot]).wait()

            @pl.when(i + 2 < n)
            def _():
                issue(pos2_ref, (slot + 2) % COMBINE_RING)
                compute(slot)

            @pl.when(i + 2 >= n)
            def _():
                compute(slot)


def _combine(pos, x1, wts_t, mod, final_g, ys, seq_len, final):
    nt, idx_rows, _ = pos.shape
    t, d = x1.shape
    tp = t // nt
    per_seq = seq_len // tp
    idx_spec = lambda ahead: pl.BlockSpec((1, idx_rows, LANES), lambda i: (jnp.minimum(i + ahead, nt - 1), 0, 0),
                                          memory_space=pltpu.SMEM)
    return pl.pallas_call(
        functools.partial(_combine_kernel, final=final),
        grid=(nt,),
        in_specs=[
            idx_spec(0), idx_spec(1), idx_spec(2),
            pl.BlockSpec((tp, d), lambda i: (i, 0)),
            pl.BlockSpec((tp, MOE_TOP_E), lambda i: (i, 0)),
            pl.BlockSpec((1, 6, d), lambda i: (i // per_seq, 0, 0)),
            pl.BlockSpec((1, d), lambda i: (0, 0)),
            pl.BlockSpec(memory_space=pl.ANY),
        ],
        out_specs=pl.BlockSpec((tp, d), lambda i: (i, 0)),
        out_shape=jax.ShapeDtypeStruct((t, d), F32),
        scratch_shapes=[pltpu.VMEM((MOE_TOP_E, tp * SUBLANES, LANES), F32)] * COMBINE_RING
        + [pltpu.SemaphoreType.DMA((COMBINE_RING,))],
        compiler_params=pltpu.CompilerParams(
            dimension_semantics=("arbitrary",), vmem_limit_bytes=VMEM_LIMIT),
    )(pos, pos, pos, x1, wts_t, mod, final_g, ys)


def _router_weights(w_group, b_group, w_expert, b_expert):
    g, e = w_group.shape[-1], w_expert.shape[-1]
    pad = ROUTER_ROWS - g - e
    w = jnp.swapaxes(jnp.concatenate([w_group, w_expert], axis=2), 1, 2)
    hi, lo = _split_bf16(jnp.pad(w, ((0, 0), (0, pad), (0, 0))))
    bias = jnp.pad(jnp.concatenate([b_group, b_expert], axis=1), ((0, 0), (0, pad)))[:, :, None]
    return hi, lo, bias


def _tile_plan(hist, n_rows):
    n_e = hist.shape[0]
    tm = GMM_TILE
    counts = jnp.sum(hist, axis=1).astype(jnp.int32)
    ends = jnp.cumsum(counts)
    starts = ends - counts
    first_tile = starts // tm
    items_per = jnp.where(counts > 0, (ends - 1) // tm - first_tile + 1, 0)
    item_ends = jnp.cumsum(items_per)
    n_items = item_ends[-1:]
    max_items = n_rows // tm + n_e - 1
    w = jnp.minimum(jnp.arange(max_items, dtype=jnp.int32), n_items[0] - 1)
    item_expert = jnp.sum(w[:, None] >= item_ends[None, :], axis=1).astype(jnp.int32)
    own = item_expert[:, None] == jnp.arange(n_e, dtype=jnp.int32)[None, :]
    pick = lambda per_expert: jnp.sum(jnp.where(own, per_expert[None, :], 0), axis=1)
    item_tile = pick(first_tile) + w - pick(item_ends - items_per)
    item_lo = jnp.maximum(pick(starts) - item_tile * tm, 0)
    item_hi = jnp.minimum(pick(ends) - item_tile * tm, tm)
    offsets = starts.astype(F32).reshape(n_e, 1)
    used_ends = jnp.cumsum((counts > 0).astype(jnp.int32))
    n_seq = used_ends[-1:]
    q = jnp.minimum(jnp.arange(n_e, dtype=jnp.int32), n_seq[0] - 1)
    seq_expert = jnp.sum(q[:, None] >= used_ends[None, :], axis=1).astype(jnp.int32)
    item_seq = pick(used_ends - 1)
    return offsets, (item_tile.astype(jnp.int32), item_seq.astype(jnp.int32), item_lo.astype(jnp.int32),
                     item_hi.astype(jnp.int32), seq_expert, n_items.astype(jnp.int32), n_seq.astype(jnp.int32))


def kernel(x, c, norm_gain, w_mod, b_mod, pool_w, pool_b, pool_scale, gla_w_in, gla_w_gate, gla_b_gate, gla_norm_g, gla_w_out, moe_w_group, moe_b_group, moe_w_expert, moe_b_expert, moe_w_in, moe_w_out, final_norm_g):
    b, s, d = x.shape
    depth = w_mod.shape[0]
    t = b * s
    n_groups = moe_w_group.shape[-1]
    n_experts = moe_w_expert.shape[-1]
    n_rows = MOE_TOP_E * t
    mod_all = _modulation(c, w_mod, b_mod).reshape(depth, b, 6, d)
    fg = final_norm_g.reshape(1, d)
    router = _router_weights(moe_w_group, moe_b_group, moe_w_expert, moe_b_expert)
    moe = None
    for i in range(depth):
        mod = mod_all[i]
        j = i // 2
        if i % 2 == 0:
            x1, h2, ids, wts, hist = _pool_layer(x, moe, mod, norm_gain[i], pool_w[j], pool_b[j], pool_scale[j],
                                                 router, i, n_groups, n_experts)
        else:
            x1, h2, ids, wts, hist = _gla_layer(x, moe, mod, norm_gain[i], gla_w_in[j], gla_w_gate[j], gla_b_gate[j],
                                                gla_norm_g[j], gla_w_out[j], router, i, n_groups, n_experts)
        offsets, plan = _tile_plan(hist, n_rows)
        pos = _positions(ids, offsets)
        xs = _dispatch(pos, h2, n_rows)
        ys = _grouped_mlp(plan, xs, moe_w_in, moe_w_out, i)
        x, moe = x1, (pos, wts.T, mod, ys)
    pos, wts_t, mod, ys = moe
    x = _combine(pos, x.reshape(t, d), wts_t, mod, fg, ys, s, final=True).reshape(b, s, d)
    return x
```

```python
import functools

import jax
import jax.numpy as jnp
from jax import lax
from jax.experimental import pallas as pl
from jax.experimental.pallas import tpu as pltpu

F32 = jnp.float32
BF16 = jnp.bfloat16

NORM_EPS = 1e-6
POOL_WINDOWS = (2, 4, 8, 16)
POOL_HALO = 16
GLA_CHUNK = 64
GLA_GATE_TEMP = 16.0
GLA_CHUNK_GROUP = 4
MOE_TOP_E = 2
LANES = 128
SUBLANES = 8
ROUTER_ROWS = 128
VMEM_LIMIT = 56 * 1024 * 1024

SEQ_TILE_POOL = 512
SEQ_TILE_GLA = 512
GMM_TILE = 256
XS_RING = 3
COMBINE_RING = 3
ROW_TILE = 512
POS_TILES_PER_STEP = 4


def _dot(a, b):
    return jnp.dot(a, b, preferred_element_type=F32)


def _dot_nt(a, b):
    return lax.dot_general(a, b, (((1,), (1,)), ((), ())), preferred_element_type=F32)


def _dot_tn(a, b):
    return lax.dot_general(a, b, (((0,), (0,)), ((), ())), preferred_element_type=F32)


def _split_bf16(x):
    hi = x.astype(BF16)
    lo = (x - hi.astype(F32)).astype(BF16)
    return hi, lo


def _store_token_tiles(ref, x):
    n = x.shape[0]
    for s in range(SUBLANES):
        ref[pl.ds(s, n, stride=SUBLANES), :] = x[:, s * LANES:(s + 1) * LANES]


def _load_token_tiles(ref, n, s):
    return ref[pl.ds(s, n, stride=SUBLANES), :]


def _norm_mod(x, gain, scale, shift):
    ms = jnp.mean(x * x, axis=-1, keepdims=True)
    return x * lax.rsqrt(ms + NORM_EPS) * (gain * (1.0 + scale)) + shift


def _mod_kernel(c_ref, w_ref, b_ref, o_ref):
    c = c_ref[...]
    sc = (c * (1.0 / (1.0 + jnp.exp(-c)))).astype(BF16)
    o_ref[0] = _dot(sc, w_ref[0].astype(BF16)) + b_ref[0]


def _modulation(c, w_mod, b_mod):
    depth, d, n = w_mod.shape
    b = c.shape[0]
    tn = n // 4
    return pl.pallas_call(
        _mod_kernel,
        grid=(depth, n // tn),
        in_specs=[
            pl.BlockSpec((b, d), lambda i, j: (0, 0)),
            pl.BlockSpec((1, d, tn), lambda i, j: (i, 0, j)),
            pl.BlockSpec((1, 1, tn), lambda i, j: (i, 0, j)),
        ],
        out_specs=pl.BlockSpec((1, b, tn), lambda i, j: (i, 0, j)),
        out_shape=jax.ShapeDtypeStruct((depth, b, n), F32),
        compiler_params=pltpu.CompilerParams(vmem_limit_bytes=VMEM_LIMIT),
    )(c, w_mod, b_mod.reshape(depth, 1, n))


def _router_epilogue(x1, mod, gain2, wr_hi_ref, wr_lo_ref, br_ref, n_groups, n_experts,
                     h2_ref, ids_ref, wts_ref, hist_ref, first_step):
    h2 = _norm_mod(x1, gain2, mod[4:5], mod[3:4])
    _store_token_tiles(h2_ref, h2)
    hh, hl = _split_bf16(h2)
    wh = wr_hi_ref[0]
    lt = _dot_nt(wh, hh) + _dot_nt(wh, hl) + _dot_nt(wr_lo_ref[0], hh) + br_ref[0]
    ts = lt.shape[1]
    per = n_experts // n_groups
    lg = lt[0:n_groups]
    mg = jnp.max(lg, axis=0, keepdims=True)
    p_g = 1.0 / jnp.sum(jnp.exp(lg - mg), axis=0, keepdims=True)
    gi = lax.broadcasted_iota(jnp.int32, lg.shape, 0)
    g_idx = jnp.min(jnp.where(lg == mg, gi, n_groups), axis=0, keepdims=True)
    sel = jnp.zeros((per, ts), F32)
    for g in range(n_groups):
        sel = jnp.where(g_idx == g, lt[n_groups + g * per:n_groups + (g + 1) * per], sel)
    ei = lax.broadcasted_iota(jnp.int32, sel.shape, 0)
    m1 = jnp.max(sel, axis=0, keepdims=True)
    i1 = jnp.min(jnp.where(sel == m1, ei, per), axis=0, keepdims=True)
    rest = jnp.where(ei == i1, -jnp.inf, sel)
    m2 = jnp.max(rest, axis=0, keepdims=True)
    i2 = jnp.min(jnp.where(rest == m2, ei, per), axis=0, keepdims=True)
    e21 = jnp.exp(m2 - m1)
    w1 = p_g / (1.0 + e21)
    w2 = p_g * e21 / (1.0 + e21)
    id1 = g_idx * per + i1
    id2 = g_idx * per + i2
    ids_ref[...] = jnp.concatenate([id1, id2], axis=0)
    wts_ref[...] = jnp.concatenate([w1, w2], axis=0)
    xi = lax.broadcasted_iota(jnp.int32, (n_experts, ts), 0)
    cnt = (xi == id1).astype(F32) + (xi == id2).astype(F32)

    @pl.when(first_step)
    def _():
        hist_ref[...] = cnt

    @pl.when(jnp.logical_not(first_step))
    def _():
        hist_ref[...] += cnt


def _router_specs(d, ts, n_seq_tiles, n_experts, layer):
    in_specs = [
        pl.BlockSpec((1, ROUTER_ROWS, d), lambda b, s: (layer, 0, 0)),
        pl.BlockSpec((1, ROUTER_ROWS, d), lambda b, s: (layer, 0, 0)),
        pl.BlockSpec((1, ROUTER_ROWS, 1), lambda b, s: (layer, 0, 0)),
    ]
    out_specs = [
        pl.BlockSpec((1, ts, d), lambda b, s: (b, s, 0)),
        pl.BlockSpec((ts * SUBLANES, LANES), lambda b, s: (b * n_seq_tiles + s, 0)),
        pl.BlockSpec((MOE_TOP_E, ts), lambda b, s: (0, b * n_seq_tiles + s)),
        pl.BlockSpec((MOE_TOP_E, ts), lambda b, s: (0, b * n_seq_tiles + s)),
        pl.BlockSpec((n_experts, ts), lambda b, s: (0, 0)),
    ]
    return in_specs, out_specs


def _router_out_shapes(b, s, d, ts, n_experts):
    assert d == SUBLANES * LANES, "token-per-tile row layout needs d_model == one (8,128) tile"
    return [
        jax.ShapeDtypeStruct((b, s, d), F32),
        jax.ShapeDtypeStruct((b * s * SUBLANES, LANES), F32),
        jax.ShapeDtypeStruct((MOE_TOP_E, b * s), jnp.int32),
        jax.ShapeDtypeStruct((MOE_TOP_E, b * s), F32),
        jax.ShapeDtypeStruct((n_experts, ts), F32),
    ]


def _moe_specs(moe, ts, n_seq_tiles, d):
    pos = moe[0]
    nt, idx_rows, _ = pos.shape
    assert idx_rows * LANES // MOE_TOP_E == ts, "row tile of the index blocks must equal the mixer's sequence tile"
    tile = lambda bb, ss: bb * n_seq_tiles + ss
    idx_spec = lambda ahead: pl.BlockSpec(
        (1, idx_rows, LANES), lambda bb, ss: (jnp.minimum(tile(bb, ss) + ahead, nt - 1), 0, 0),
        memory_space=pltpu.SMEM)
    in_specs = [idx_spec(0), idx_spec(1),
                pl.BlockSpec((ts, MOE_TOP_E), lambda bb, ss: (tile(bb, ss), 0)),
                pl.BlockSpec((1, 6, d), lambda bb, ss: (bb, 0, 0)),
                pl.BlockSpec(memory_space=pl.ANY)]
    scratch = [pltpu.VMEM((MOE_TOP_E, ts * SUBLANES, LANES), F32),
               pltpu.SemaphoreType.DMA(()),
               pltpu.VMEM((ts, d), F32)]
    return in_specs, scratch


N_MOE_REFS = 5


def _moe_args(moe):
    return (moe[0],) * 2 + tuple(moe[1:]) if moe else ()


def _moe_combine(j, moe_refs, x1p_ref, scratch):
    pos0_ref, pos1_ref, wtp_ref, modp_ref, ysp_ref = moe_refs
    buf, sem, x_s = scratch
    tp = x_s.shape[0]

    def copy(idx_ref, k, row, lane, token):
        return pltpu.make_async_copy(ysp_ref.at[_tile_rows(idx_ref[0, row, lane])],
                                     buf.at[k, _tile_rows(token)], sem)

    def wait():
        for k in range(MOE_TOP_E):
            pltpu.make_async_copy(ysp_ref.at[pl.ds(0, tp * SUBLANES)], buf.at[k], sem).wait()

    @pl.when(j == 0)
    def _():
        _for_each_token_looped(tp, lambda k, row, lane, token: copy(pos0_ref, k, row, lane, token).start(priority=k))

    wait()
    wt = wtp_ref[...]
    gate2 = modp_ref[0, 5:6]
    for s in range(SUBLANES):
        rows = pl.ds(s, tp, stride=SUBLANES)
        cols = slice(s * LANES, (s + 1) * LANES)
        y = wt[:, 0:1] * buf[0, rows, :] + wt[:, 1:2] * buf[1, rows, :]
        x_s[:, cols] = x1p_ref[0, :, cols] + gate2[:, cols] * y

    def issue_ahead():
        _for_each_token(tp, lambda k, row, lane, token: copy(pos1_ref, k, row, lane, token).start(priority=k))

    return issue_ahead, wait


def _pool_kernel(*refs, n_groups, n_experts, fused, n_tiles):
    moe_refs, refs = (refs[:N_MOE_REFS], refs[N_MOE_REFS:]) if fused else (None, refs)
    (x_ref, mod_ref, gain_ref, pw_ref, pb_ref, ps_ref, wr_hi_ref, wr_lo_ref, br_ref,
     x1_ref, h2_ref, ids_ref, wts_ref, hist_ref, hbuf_ref, sa_ref, sb_ref) = refs[:17]
    b, s = pl.program_id(0), pl.program_id(1)
    if fused:
        tile = b * pl.num_programs(1) + s
        issue_ahead, drain = _moe_combine(tile, moe_refs, x_ref, refs[17:])
        x = refs[19][...]
    else:
        x = x_ref[0]
    ts, d = x.shape
    mod = mod_ref[0]
    h = _norm_mod(x, gain_ref[0:1], mod[1:2], mod[0:1])

    first = SUBLANES + POOL_HALO
    n = first + ts

    @pl.when(s == 0)
    def _():
        hbuf_ref[0:first] = jnp.zeros((first, d), F32)
        sa_ref[0:SUBLANES] = jnp.zeros((SUBLANES, d), F32)
        sb_ref[0:SUBLANES] = jnp.zeros((SUBLANES, d), F32)

    if fused:
        issue_ahead()
    hbuf_ref[first:n] = h
    pos = (s * ts + 1 + lax.broadcasted_iota(jnp.int32, (ts, 1), 0)).astype(F32)
    cg = d // len(POOL_WINDOWS)
    src, dst = hbuf_ref, sa_ref
    ys = []
    for gi, win in enumerate(POOL_WINDOWS):
        half = win // 2
        assert win == 2 ** (gi + 1) and win <= POOL_HALO
        lo = gi * cg
        dst[SUBLANES:n, lo:] = src[SUBLANES:n, lo:] + src[SUBLANES - half:n - half, lo:]
        cols = slice(lo, lo + cg)
        dgi = dst[first:n, cols] / jnp.minimum(pos, float(win)) - h[:, cols]
        ys.append(_dot(dgi.astype(BF16), pw_ref[gi]))
        src, dst = dst, (sb_ref if dst is sa_ref else sa_ref)
    hbuf_ref[SUBLANES:first] = hbuf_ref[n - POOL_HALO:n]
    x1 = x + (jnp.concatenate(ys, axis=-1) + pb_ref[...]) * (ps_ref[...] * mod[2:3])
    x1_ref[0] = x1
    _router_epilogue(x1, mod, gain_ref[1:2], wr_hi_ref, wr_lo_ref, br_ref, n_groups, n_experts,
                     h2_ref, ids_ref, wts_ref, hist_ref, jnp.logical_and(b == 0, s == 0))
    if fused:
        pl.when(tile == n_tiles - 1)(drain)


def _pool_layer(x, moe, mod, gain, pw, pb, ps, router, layer, n_groups, n_experts):
    b, s, d = x.shape
    ts = min(SEQ_TILE_POOL, s)
    nst = s // ts
    g, cg = pw.shape[0], pw.shape[1]
    r_in, r_out = _router_specs(d, ts, nst, n_experts, layer)
    m_in, m_scratch = _moe_specs(moe, ts, nst, d) if moe else ([], [])
    return pl.pallas_call(
        functools.partial(_pool_kernel, n_groups=n_groups, n_experts=n_experts, fused=bool(moe), n_tiles=b * nst),
        grid=(b, nst),
        in_specs=m_in + [
            pl.BlockSpec((1, ts, d), lambda bb, ss: (bb, ss, 0)),
            pl.BlockSpec((1, 6, d), lambda bb, ss: (bb, 0, 0)),
            pl.BlockSpec((2, d), lambda bb, ss: (0, 0)),
            pl.BlockSpec((g, cg, cg), lambda bb, ss: (0, 0, 0)),
            pl.BlockSpec((1, d), lambda bb, ss: (0, 0)),
            pl.BlockSpec((1, d), lambda bb, ss: (0, 0)),
        ] + r_in,
        out_specs=r_out,
        out_shape=_router_out_shapes(b, s, d, ts, n_experts),
        scratch_shapes=[pltpu.VMEM((SUBLANES + POOL_HALO + ts, d), F32)] * 3 + m_scratch,
        compiler_params=pltpu.CompilerParams(
            dimension_semantics=("arbitrary", "arbitrary"), vmem_limit_bytes=VMEM_LIMIT),
    )(*_moe_args(moe), x, mod, gain, pw.astype(BF16), pb.reshape(1, d), ps.reshape(1, d), *router)


N_GLA_REFS = 28


def _gla_kernel(*refs, n_heads, n_groups, n_experts, fused, n_tiles):
    moe_refs, refs = (refs[:N_MOE_REFS], refs[N_MOE_REFS:]) if fused else (None, refs)
    (x_ref, mod_ref, gain_ref, wp_ref, wz_ref, wg_ref, bg_ref, ng_ref, wo_ref, wr_hi_ref, wr_lo_ref, br_ref,
     x1_ref, h2_ref, ids_ref, wts_ref, hist_ref, state_ref, q_s, k_s, g_s, v_s, r_s, o_s,
     qe_s, kd_s, el_s, oi_s) = refs[:N_GLA_REFS]
    b, s = pl.program_id(0), pl.program_id(1)
    if fused:
        tile = b * pl.num_programs(1) + s
        issue_ahead, drain = _moe_combine(tile, moe_refs, x_ref, refs[N_GLA_REFS:])
        x = refs[N_GLA_REFS + 2][...]
    else:
        x = x_ref[0]
    ts, d = x.shape
    dk_all = q_s.shape[1]
    dk = dk_all // n_heads
    dv = d // n_heads
    mod = mod_ref[0]
    h = _norm_mod(x, gain_ref[0:1], mod[1:2], mod[0:1]).astype(BF16)

    @pl.when(s == 0)
    def _():
        state_ref[...] = jnp.zeros(state_ref.shape, F32)

    if fused:
        issue_ahead()
    q_s[...] = _dot(h, wp_ref[:, 0:dk_all]) * (dk ** -0.5)
    k_s[...] = _dot(h, wp_ref[:, dk_all:2 * dk_all])
    v_s[...] = _dot(h, wp_ref[:, 2 * dk_all:2 * dk_all + d]).astype(BF16)
    r = _dot(h, wp_ref[:, 2 * dk_all + d:2 * dk_all + 2 * d])
    r_s[...] = r * (1.0 / (1.0 + jnp.exp(-r)))
    z = _dot(h, wz_ref[...]).astype(BF16)
    u = _dot(z, wg_ref[...]) + bg_ref[...]
    g_s[...] = (jnp.minimum(u, 0.0) - jnp.log(1.0 + jnp.exp(-jnp.abs(u)))) * (1.0 / GLA_GATE_TEMP)

    c = GLA_CHUNK
    ri = lax.broadcasted_iota(jnp.int32, (c, c), 0)
    ci = lax.broadcasted_iota(jnp.int32, (c, c), 1)
    causal = ri >= ci
    tril = jnp.where(causal, 1.0, 0.0).astype(BF16)
    ng = ng_ref[...]

    heads = [(slice(hd * dk, (hd + 1) * dk), slice(hd * dv, (hd + 1) * dv)) for hd in range(n_heads)]
    group = min(GLA_CHUNK_GROUP, ts // c)

    def intra(m, carry):
        ns = [m * group + i for i in range(group)]
        rows = [pl.ds(pl.multiple_of(n * c, c), c) for n in ns]
        gs = [_split_bf16(g_s[r, :]) for r in rows]
        bcs = [_dot(tril, hi) + _dot(tril, lo) for hi, lo in gs]
        qes, kes = [], []
        for n, r, bc in zip(ns, rows, bcs):
            bl = bc[c - 1:c, :]
            kk = k_s[r, :]
            qes.append((q_s[r, :] * jnp.exp(bc)).astype(BF16))
            kes.append((kk * jnp.exp(-bc)).astype(BF16))
            qe_s[r, :] = qes[-1]
            kd_s[r, :] = (kk * jnp.exp(bl - bc)).astype(BF16)
            el_s[pl.ds(n, 1), :] = jnp.exp(bl)
        scs = [[_dot_nt(q_e[:, ks], k_e[:, ks]) for ks, _ in heads] for q_e, k_e in zip(qes, kes)]
        scs = [[jnp.where(causal, sc, 0.0).astype(BF16) for sc in per_chunk] for per_chunk in scs]
        for r, per_chunk in zip(rows, scs):
            vv = v_s[r, :]
            for sc, (_, vs) in zip(per_chunk, heads):
                oi_s[r, vs] = _dot(sc, vv[:, vs])
        return carry

    lax.fori_loop(0, ts // c // group, intra, 0)

    def inter(m, carry):
        ns = [m * group + i for i in range(group)]
        rows = [pl.ds(pl.multiple_of(n * c, c), c) for n in ns]
        incs = []
        for r in rows:
            vv, k_d = v_s[r, :], kd_s[r, :]
            incs.append([_dot_tn(vv[:, vs], k_d[:, ks]) for ks, vs in heads])
        for n, r, inc in zip(ns, rows, incs):
            q_e = qe_s[r, :]
            e_l = el_s[pl.ds(n, 1), :]
            sts = [state_ref[hd] for hd in range(n_heads)]
            carried = [_dot_nt(q_e[:, ks], st.astype(BF16)) for (ks, _), st in zip(heads, sts)]
            for hd, (ks, _) in enumerate(heads):
                state_ref[hd] = sts[hd] * e_l[:, ks] + inc[hd]
            for o_c, (_, vs) in zip(carried, heads):
                o = oi_s[r, vs] + o_c
                o = o * lax.rsqrt(jnp.mean(o * o, axis=-1, keepdims=True) + NORM_EPS) * ng
                o_s[r, vs] = (o * r_s[r, vs]).astype(BF16)
        return carry

    lax.fori_loop(0, ts // c // group, inter, 0)
    x1 = x + mod[2:3] * _dot(o_s[...], wo_ref[...])
    x1_ref[0] = x1
    _router_epilogue(x1, mod, gain_ref[1:2], wr_hi_ref, wr_lo_ref, br_ref, n_groups, n_experts,
                     h2_ref, ids_ref, wts_ref, hist_ref, jnp.logical_and(b == 0, s == 0))
    if fused:
        pl.when(tile == n_tiles - 1)(drain)


def _gla_layer(x, moe, mod, gain, w_in, w_gate, b_gate, norm_g, w_out, router, layer, n_groups, n_experts):
    b, s, d = x.shape
    ts = min(SEQ_TILE_GLA, s)
    nst = s // ts
    rank, dk_all = w_gate.shape
    dv = norm_g.shape[0]
    n_heads = d // dv
    n_proj = 2 * dk_all + 2 * d
    wp = w_in[:, :n_proj].astype(BF16)
    wz = jnp.pad(w_in[:, n_proj:], ((0, 0), (0, LANES - rank))).astype(BF16)
    wg = jnp.pad(w_gate, ((0, LANES - rank), (0, 0))).astype(BF16)
    r_in, r_out = _router_specs(d, ts, nst, n_experts, layer)
    m_in, m_scratch = _moe_specs(moe, ts, nst, d) if moe else ([], [])
    const = lambda bb, ss: (0, 0)
    once = pl.Buffered(1)
    return pl.pallas_call(
        functools.partial(_gla_kernel, n_heads=n_heads, n_groups=n_groups, n_experts=n_experts, fused=bool(moe),
                          n_tiles=b * nst),
        grid=(b, nst),
        in_specs=m_in + [
            pl.BlockSpec((1, ts, d), lambda bb, ss: (bb, ss, 0)),
            pl.BlockSpec((1, 6, d), lambda bb, ss: (bb, 0, 0)),
            pl.BlockSpec((2, d), const),
            pl.BlockSpec((d, n_proj), const, pipeline_mode=once),
            pl.BlockSpec((d, LANES), const),
            pl.BlockSpec((LANES, dk_all), const),
            pl.BlockSpec((1, dk_all), const),
            pl.BlockSpec((1, dv), const),
            pl.BlockSpec((d, d), const, pipeline_mode=once),
        ] + r_in,
        out_specs=r_out,
        out_shape=_router_out_shapes(b, s, d, ts, n_experts),
        scratch_shapes=[
            pltpu.VMEM((n_heads, dv, dk_all // n_heads), F32),
            pltpu.VMEM((ts, dk_all), F32),
            pltpu.VMEM((ts, dk_all), F32),
            pltpu.VMEM((ts, dk_all), F32),
            pltpu.VMEM((ts, d), BF16),
            pltpu.VMEM((ts, d), F32),
            pltpu.VMEM((ts, d), BF16),
            pltpu.VMEM((ts, dk_all), BF16),
            pltpu.VMEM((ts, dk_all), BF16),
            pltpu.VMEM((ts // GLA_CHUNK, dk_all), F32),
            pltpu.VMEM((ts, d), F32),
        ] + m_scratch,
        compiler_params=pltpu.CompilerParams(
            dimension_semantics=("arbitrary", "arbitrary"), vmem_limit_bytes=VMEM_LIMIT),
    )(*_moe_args(moe), x, mod, gain, wp, wz, wg, b_gate.reshape(1, dk_all), norm_g.reshape(1, dv), w_out.astype(BF16),
      *router)


def _pos_kernel(ids_ref, off_ref, pos_ref, run_ref):
    i = pl.program_id(0)
    tiles, idx_rows, _ = pos_ref.shape
    tp = idx_rows * LANES // MOE_TOP_E
    n_e = off_ref.shape[0]

    @pl.when(i == 0)
    def _():
        run_ref[...] = jnp.zeros(run_ref.shape, F32)

    ei = lax.broadcasted_iota(jnp.int32, (n_e, tp), 0)
    tri = jnp.where(lax.broadcasted_iota(jnp.int32, (tp, tp), 0) <= lax.broadcasted_iota(jnp.int32, (tp, tp), 1),
                    1.0, 0.0).astype(BF16)
    base = off_ref[...] + run_ref[...]
    for u in range(tiles):
        ids = ids_ref[:, u * tp:(u + 1) * tp]
        oh0 = ei == ids[0:1]
        oh1 = ei == ids[1:2]
        inc0 = _dot(jnp.where(oh0, 1.0, 0.0).astype(BF16), tri)
        inc1 = _dot(jnp.where(oh1, 1.0, 0.0).astype(BF16), tri)
        tot0 = inc0[:, tp - 1:tp]
        tot1 = inc1[:, tp - 1:tp]
        p0 = jnp.sum(jnp.where(oh0, base + inc0 - 1.0, 0.0), axis=0, keepdims=True)
        p1 = jnp.sum(jnp.where(oh1, base + tot0 + inc1 - 1.0, 0.0), axis=0, keepdims=True)
        p = [pk[:, c * LANES:(c + 1) * LANES] for pk in (p0, p1) for c in range(tp // LANES)]
        pos_ref[u] = jnp.concatenate(p, axis=0).astype(jnp.int32)
        base = base + tot0 + tot1
    run_ref[...] = base - off_ref[...]


def _positions(ids, offsets):
    t = ids.shape[1]
    tp = min(ROW_TILE, t)
    tiles = min(POS_TILES_PER_STEP, t // tp)
    n_e = offsets.shape[0]
    return pl.pallas_call(
        _pos_kernel,
        grid=(t // (tp * tiles),),
        in_specs=[pl.BlockSpec((MOE_TOP_E, tp * tiles), lambda i: (0, i)),
                  pl.BlockSpec((n_e, 1), lambda i: (0, 0))],
        out_specs=pl.BlockSpec((tiles, MOE_TOP_E * tp // LANES, LANES), lambda i: (i, 0, 0)),
        out_shape=jax.ShapeDtypeStruct((t // tp, MOE_TOP_E * tp // LANES, LANES), jnp.int32),
        scratch_shapes=[pltpu.VMEM((n_e, 1), F32)],
        compiler_params=pltpu.CompilerParams(dimension_semantics=("arbitrary",)),
    )(ids, offsets)


def _tile_rows(row):
    return pl.ds(pl.multiple_of(row * SUBLANES, SUBLANES), SUBLANES)


def _for_each_token(tp, fn):
    chunks = tp // LANES
    for c in range(chunks):
        for lane in range(LANES):
            for k in range(MOE_TOP_E):
                fn(k, k * chunks + c, lane, c * LANES + lane)


def _for_each_token_looped(tp, fn):
    chunks = tp // LANES
    for c in range(chunks):
        def body(lane, carry):
            for k in range(MOE_TOP_E):
                fn(k, k * chunks + c, lane, c * LANES + lane)
            return carry
        lax.fori_loop(0, LANES, body, 0)


def _inverse_kernel(pos_ref, inv_ref):
    tp = pos_ref.shape[1] * LANES // MOE_TOP_E
    base = pl.program_id(0) * tp

    def put(k, row, lane, token):
        inv_ref[pos_ref[0, row, lane]] = base + token

    _for_each_token(tp, put)


def _inverse_positions(pos, n_rows):
    nt, idx_rows, _ = pos.shape
    return pl.pallas_call(
        _inverse_kernel,
        grid=(nt,),
        in_specs=[pl.BlockSpec((1, idx_rows, LANES), lambda i: (i, 0, 0), memory_space=pltpu.SMEM)],
        out_specs=pl.BlockSpec(memory_space=pltpu.SMEM),
        out_shape=jax.ShapeDtypeStruct((n_rows,), jnp.int32),
        compiler_params=pltpu.CompilerParams(dimension_semantics=("arbitrary",)),
    )(pos)


def _gmm_kernel(it_ref, iq_ref, lo_ref, hi_ref, qe_ref, n_ref, nq_ref, tok0_ref, tok1_ref, tok2_ref,
                h_hbm, wi_hbm, wo_hbm, ys_ref, wi_s, wo_s, a_s, acc_s, wi_f, wo_f, xb0, xb1, xb2, sem_i, sem_o, sem_x,
                *, layer):
    w = pl.program_id(0)
    f = wo_s.shape[0]
    tm = acc_s.shape[0]
    n = n_ref[0]
    xbufs = (xb0, xb1, xb2)

    def wi_copy(q, slot):
        return pltpu.make_async_copy(wi_hbm.at[layer, qe_ref[q]], wi_f.at[slot], sem_i.at[slot])

    def wo_copy(q, slot):
        return pltpu.make_async_copy(wo_hbm.at[layer, qe_ref[q]], wo_f.at[slot], sem_o.at[slot])

    def row_copy(tok_ref, half, lane, slot):
        return pltpu.make_async_copy(h_hbm.at[_tile_rows(tok_ref[0, half, lane])],
                                     xbufs[slot].at[_tile_rows(half * LANES + lane)], sem_x.at[slot])

    def wait_rows(slot):
        pltpu.make_async_copy(h_hbm.at[pl.ds(0, tm * SUBLANES)], xbufs[slot], sem_x.at[slot]).wait()

    def gather(tok_ref, slot, unrolled):
        for half in range(tm // LANES):
            if unrolled:
                for lane in range(LANES):
                    row_copy(tok_ref, half, lane, slot).start(priority=lane % 2)
            else:
                def body(lane, carry):
                    row_copy(tok_ref, half, lane, slot).start()
                    return carry
                lax.fori_loop(0, LANES, body, 0)

    @pl.when(w <= n)
    def _():
        ja = jnp.minimum(w, n - 1)
        jb = jnp.maximum(w - 1, 0)
        slot_a = lax.rem(w, 2)
        qa = iq_ref[ja]
        qb = iq_ref[jb]

        @pl.when(w == 0)
        def _():
            wi_copy(0, 0).start()
            wo_copy(0, 0).start()
            gather(tok0_ref, 0, False)
            gather(tok1_ref, 1, False)
            a_s[1] = jnp.zeros(a_s.shape[1:], BF16)
            acc_s[...] = jnp.zeros(acc_s.shape, F32)
            wo_s[...] = jnp.zeros(wo_s.shape, BF16)

        @pl.when(jnp.logical_or(w == 1, jnp.logical_and(w >= 2, qb != iq_ref[jnp.maximum(w - 2, 0)])))
        def _():
            slot = lax.rem(qb, 2)
            wo_copy(qb, slot).wait()
            wo_s[...] = wo_f[slot].astype(BF16)

        @pl.when(jnp.logical_or(w == 0, qa != qb))
        def _():
            @pl.when(qa + 1 < nq_ref[0])
            def _():
                wi_copy(qa + 1, lax.rem(qa + 1, 2)).start()
                wo_copy(qa + 1, lax.rem(qa + 1, 2)).start()

            slot = lax.rem(qa, 2)
            wi_copy(qa, slot).wait()
            wi_s[...] = wi_f[slot].astype(BF16)

        def second_stage():
            y = _dot(a_s[1 - slot_a], wo_s[...])
            row = lax.broadcasted_iota(jnp.int32, (tm, 1), 0)
            mine = jnp.logical_and(jnp.logical_and(row >= lo_ref[jb], row < hi_ref[jb]), w >= 1)
            merged = jnp.where(mine, y, acc_s[...])
            acc_s[...] = merged
            _store_token_tiles(ys_ref, merged)

        for slot in range(XS_RING):
            @pl.when(jnp.logical_and(lax.rem(w, XS_RING) == slot, w < n))
            def _():
                wait_rows(slot)
                x = jnp.concatenate(
                    [_load_token_tiles(xbufs[slot], tm, s).astype(BF16) for s in range(SUBLANES)], axis=-1)
                gather(tok2_ref, (slot + 2) % XS_RING, True)
                second_stage()
                gu = _dot(x, wi_s[...])
                gate, up = gu[:, :f], gu[:, f:]
                a_s[slot_a] = (gate * (1.0 / (1.0 + jnp.exp(-gate))) * up).astype(BF16)

            @pl.when(jnp.logical_and(lax.rem(w, XS_RING) == slot, w == n))
            def _():
                second_stage()
                wait_rows(slot)
                wait_rows((slot + 1) % XS_RING)


def _grouped_mlp(plan, h2, inv, w_in, w_out, layer):
    item_tile, item_seq, item_lo, item_hi, seq_expert, n_items, n_seq = plan
    d, f2 = w_in.shape[-2:]
    f = f2 // 2
    tm = GMM_TILE
    rows = tm * SUBLANES
    tok = inv.reshape(-1, tm // LANES, LANES)

    def item(ahead):
        return lambda w, it, iq, lo, hi, qe, n, nq: (it[jnp.minimum(w + ahead, n[0] - 1)], 0, 0)

    def second(w, n):
        return jnp.minimum(jnp.maximum(w - 1, 0), n[0] - 1)

    tok_spec = lambda ahead: pl.BlockSpec((1, tm // LANES, LANES), item(ahead), memory_space=pltpu.SMEM)
    return pl.pallas_call(
        functools.partial(_gmm_kernel, layer=layer),
        grid_spec=pltpu.PrefetchScalarGridSpec(
            num_scalar_prefetch=7,
            grid=(item_tile.shape[0] + 1,),
            in_specs=[tok_spec(0), tok_spec(1), tok_spec(2)] + [pl.BlockSpec(memory_space=pl.ANY)] * 3,
            out_specs=pl.BlockSpec((rows, LANES), lambda w, it, iq, lo, hi, qe, n, nq: (it[second(w, n)], 0)),
            scratch_shapes=[pltpu.VMEM((d, f2), BF16), pltpu.VMEM((f, d), BF16), pltpu.VMEM((2, tm, f), BF16),
                            pltpu.VMEM((tm, d), F32), pltpu.VMEM((2, d, f2), F32), pltpu.VMEM((2, f, d), F32)]
            + [pltpu.VMEM((rows, LANES), F32)] * XS_RING
            + [pltpu.SemaphoreType.DMA((2,)), pltpu.SemaphoreType.DMA((2,)), pltpu.SemaphoreType.DMA((XS_RING,))],
        ),
        out_shape=jax.ShapeDtypeStruct((inv.shape[0] * SUBLANES, LANES), F32),
        compiler_params=pltpu.CompilerParams(
            dimension_semantics=("arbitrary",), vmem_limit_bytes=VMEM_LIMIT),
    )(item_tile, item_seq, item_lo, item_hi, seq_expert, n_items, n_seq, tok, tok, tok, h2, w_in, w_out)


def _combine_kernel(pos0_ref, pos1_ref, pos2_ref, x1_ref, wt_ref, mod_ref, fg_ref, ys_ref, o_ref, *scratch, final):
    bufs, sems = scratch[:COMBINE_RING], scratch[COMBINE_RING]
    i = pl.program_id(0)
    n = pl.num_programs(0)
    tp = x1_ref.shape[0]

    def issue(idx_ref, slot):
        def one(k, row, lane, token):
            pltpu.make_async_copy(ys_ref.at[_tile_rows(idx_ref[0, row, lane])], bufs[slot].at[k, _tile_rows(token)],
                                  sems.at[slot]).start(priority=k)
        _for_each_token(tp, one)

    def compute(slot):
        wt = wt_ref[...]
        gate2 = mod_ref[0, 5:6]
        chunks = []
        for s in range(SUBLANES):
            rows = pl.ds(s, tp, stride=SUBLANES)
            cols = slice(s * LANES, (s + 1) * LANES)
            y = wt[:, 0:1] * bufs[slot][0, rows, :] + wt[:, 1:2] * bufs[slot][1, rows, :]
            chunks.append(x1_ref[:, cols] + gate2[:, cols] * y)
        x = jnp.concatenate(chunks, axis=-1)
        if final:
            x = x * lax.rsqrt(jnp.mean(x * x, axis=-1, keepdims=True) + NORM_EPS) * fg_ref[...]
        o_ref[...] = x

    @pl.when(i == 0)
    def _():
        issue(pos0_ref, 0)

        @pl.when(n > 1)
        def _():
            issue(pos1_ref, 1)

    for slot in range(COMBINE_RING):
        @pl.when(lax.rem(i, COMBINE_RING) == slot)
        def _():
            for k in range(MOE_TOP_E):
                pltpu.make_async_copy(ys_ref.at[pl.ds(0, tp * SUBLANES)], bufs[slot].at[k], sems.at[slot]).wait()

            @pl.when(i + 2 < n)
            def _():
                issue(pos2_ref, (slot + 2) % COMBINE_RING)
                compute(slot)

            @pl.when(i + 2 >= n)
            def _():
                compute(slot)


def _combine(pos, x1, wts_t, mod, final_g, ys, seq_len, final):
    nt, idx_rows, _ = pos.shape
    t, d = x1.shape
    tp = t // nt
    per_seq = seq_len // tp
    idx_spec = lambda ahead: pl.BlockSpec((1, idx_rows, LANES), lambda i: (jnp.minimum(i + ahead, nt - 1), 0, 0),
                                          memory_space=pltpu.SMEM)
    return pl.pallas_call(
        functools.partial(_combine_kernel, final=final),
        grid=(nt,),
        in_specs=[
            idx_spec(0), idx_spec(1), idx_spec(2),
            pl.BlockSpec((tp, d), lambda i: (i, 0)),
            pl.BlockSpec((tp, MOE_TOP_E), lambda i: (i, 0)),
            pl.BlockSpec((1, 6, d), lambda i: (i // per_seq, 0, 0)),
            pl.BlockSpec((1, d), lambda i: (0, 0)),
            pl.BlockSpec(memory_space=pl.ANY),
        ],
        out_specs=pl.BlockSpec((tp, d), lambda i: (i, 0)),
        out_shape=jax.ShapeDtypeStruct((t, d), F32),
        scratch_shapes=[pltpu.VMEM((MOE_TOP_E, tp * SUBLANES, LANES), F32)] * COMBINE_RING
        + [pltpu.SemaphoreType.DMA((COMBINE_RING,))],
        compiler_params=pltpu.CompilerParams(
            dimension_semantics=("arbitrary",), vmem_limit_bytes=VMEM_LIMIT),
    )(pos, pos, pos, x1, wts_t, mod, final_g, ys)


def _router_weights(w_group, b_group, w_expert, b_expert):
    g, e = w_group.shape[-1], w_expert.shape[-1]
    pad = ROUTER_ROWS - g - e
    w = jnp.swapaxes(jnp.concatenate([w_group, w_expert], axis=2), 1, 2)
    hi, lo = _split_bf16(jnp.pad(w, ((0, 0), (0, pad), (0, 0))))
    bias = jnp.pad(jnp.concatenate([b_group, b_expert], axis=1), ((0, 0), (0, pad)))[:, :, None]
    return hi, lo, bias


def _tile_plan(hist, n_rows):
    n_e = hist.shape[0]
    tm = GMM_TILE
    counts = jnp.sum(hist, axis=1).astype(jnp.int32)
    ends = jnp.cumsum(counts)
    starts = ends - counts
    first_tile = starts // tm
    items_per = jnp.where(counts > 0, (ends - 1) // tm - first_tile + 1, 0)
    item_ends = jnp.cumsum(items_per)
    n_items = item_ends[-1:]
    max_items = n_rows // tm + n_e - 1
    w = jnp.minimum(jnp.arange(max_items, dtype=jnp.int32), n_items[0] - 1)
    item_expert = jnp.sum(w[:, None] >= item_ends[None, :], axis=1).astype(jnp.int32)
    own = item_expert[:, None] == jnp.arange(n_e, dtype=jnp.int32)[None, :]
    pick = lambda per_expert: jnp.sum(jnp.where(own, per_expert[None, :], 0), axis=1)
    item_tile = pick(first_tile) + w - pick(item_ends - items_per)
    item_lo = jnp.maximum(pick(starts) - item_tile * tm, 0)
    item_hi = jnp.minimum(pick(ends) - item_tile * tm, tm)
    offsets = starts.astype(F32).reshape(n_e, 1)
    used_ends = jnp.cumsum((counts > 0).astype(jnp.int32))
    n_seq = used_ends[-1:]
    q = jnp.minimum(jnp.arange(n_e, dtype=jnp.int32), n_seq[0] - 1)
    seq_expert = jnp.sum(q[:, None] >= used_ends[None, :], axis=1).astype(jnp.int32)
    item_seq = pick(used_ends - 1)
    return offsets, (item_tile.astype(jnp.int32), item_seq.astype(jnp.int32), item_lo.astype(jnp.int32),
                     item_hi.astype(jnp.int32), seq_expert, n_items.astype(jnp.int32), n_seq.astype(jnp.int32))


def kernel(x, c, norm_gain, w_mod, b_mod, pool_w, pool_b, pool_scale, gla_w_in, gla_w_gate, gla_b_gate, gla_norm_g, gla_w_out, moe_w_group, moe_b_group, moe_w_expert, moe_b_expert, moe_w_in, moe_w_out, final_norm_g):
    b, s, d = x.shape
    depth = w_mod.shape[0]
    t = b * s
    n_groups = moe_w_group.shape[-1]
    n_experts = moe_w_expert.shape[-1]
    n_rows = MOE_TOP_E * t
    mod_all = _modulation(c, w_mod, b_mod).reshape(depth, b, 6, d)
    fg = final_norm_g.reshape(1, d)
    router = _router_weights(moe_w_group, moe_b_group, moe_w_expert, moe_b_expert)
    moe = None
    for i in range(depth):
        mod = mod_all[i]
        j = i // 2
        if i % 2 == 0:
            x1, h2, ids, wts, hist = _pool_layer(x, moe, mod, norm_gain[i], pool_w[j], pool_b[j], pool_scale[j],
                                                 router, i, n_groups, n_experts)
        else:
            x1, h2, ids, wts, hist = _gla_layer(x, moe, mod, norm_gain[i], gla_w_in[j], gla_w_gate[j], gla_b_gate[j],
                                                gla_norm_g[j], gla_w_out[j], router, i, n_groups, n_experts)
        offsets, plan = _tile_plan(hist, n_rows)
        pos = _positions(ids, offsets)
        inv = _inverse_positions(pos, n_rows)
        ys = _grouped_mlp(plan, h2, inv, moe_w_in, moe_w_out, i)
        x, moe = x1, (pos, wts.T, mod, ys)
    pos, wts_t, mod, ys = moe
    x = _combine(pos, x.reshape(t, d), wts_t, mod, fg, ys, s, final=True).reshape(b, s, d)
    return x
```

```python
import functools

import jax
import jax.numpy as jnp
from jax import lax
from jax.experimental import pallas as pl
from jax.experimental.pallas import tpu as pltpu

F32 = jnp.float32
BF16 = jnp.bfloat16

NORM_EPS = 1e-6
POOL_WINDOWS = (2, 4, 8, 16)
POOL_HALO = 16
GLA_CHUNK = 64
GLA_GATE_TEMP = 16.0
GLA_CHUNK_GROUP = 4
MOE_TOP_E = 2
LANES = 128
SUBLANES = 8
ROUTER_ROWS = 128
VMEM_LIMIT = 56 * 1024 * 1024

SEQ_TILE_POOL = 512
SEQ_TILE_GLA = 512
GMM_TILE = 256
XS_RING = 3
COPY_PACING_ROUNDS = 1
COMBINE_RING = 3
ROW_TILE = 512
POS_TILES_PER_STEP = 4


def _dot(a, b):
    return jnp.dot(a, b, preferred_element_type=F32)


def _dot_nt(a, b):
    return lax.dot_general(a, b, (((1,), (1,)), ((), ())), preferred_element_type=F32)


def _dot_tn(a, b):
    return lax.dot_general(a, b, (((0,), (0,)), ((), ())), preferred_element_type=F32)


def _split_bf16(x):
    hi = x.astype(BF16)
    lo = (x - hi.astype(F32)).astype(BF16)
    return hi, lo


def _store_token_tiles(ref, x):
    n = x.shape[0]
    for s in range(SUBLANES):
        ref[pl.ds(s, n, stride=SUBLANES), :] = x[:, s * LANES:(s + 1) * LANES]


def _paced(row):
    for _ in range(COPY_PACING_ROUNDS):
        row = row + lax.shift_right_arithmetic(row, 31)
    return row


def _load_token_tiles(ref, n, s):
    return ref[pl.ds(s, n, stride=SUBLANES), :]


def _norm_mod(x, gain, scale, shift):
    ms = jnp.mean(x * x, axis=-1, keepdims=True)
    return x * lax.rsqrt(ms + NORM_EPS) * (gain * (1.0 + scale)) + shift


def _mod_kernel(c_ref, w_ref, b_ref, o_ref):
    c = c_ref[...]
    sc = (c * (1.0 / (1.0 + jnp.exp(-c)))).astype(BF16)
    o_ref[0] = _dot(sc, w_ref[0].astype(BF16)) + b_ref[0]


def _modulation(c, w_mod, b_mod):
    depth, d, n = w_mod.shape
    b = c.shape[0]
    tn = n // 4
    return pl.pallas_call(
        _mod_kernel,
        grid=(depth, n // tn),
        in_specs=[
            pl.BlockSpec((b, d), lambda i, j: (0, 0)),
            pl.BlockSpec((1, d, tn), lambda i, j: (i, 0, j)),
            pl.BlockSpec((1, 1, tn), lambda i, j: (i, 0, j)),
        ],
        out_specs=pl.BlockSpec((1, b, tn), lambda i, j: (i, 0, j)),
        out_shape=jax.ShapeDtypeStruct((depth, b, n), F32),
        compiler_params=pltpu.CompilerParams(vmem_limit_bytes=VMEM_LIMIT),
    )(c, w_mod, b_mod.reshape(depth, 1, n))


def _router_epilogue(x1, mod, gain2, wr_hi_ref, wr_lo_ref, br_ref, n_groups, n_experts,
                     h2_ref, ids_ref, wts_ref, hist_ref, first_step):
    h2 = _norm_mod(x1, gain2, mod[4:5], mod[3:4])
    _store_token_tiles(h2_ref, h2)
    hh, hl = _split_bf16(h2)
    wh = wr_hi_ref[0]
    lt = _dot_nt(wh, hh) + _dot_nt(wh, hl) + _dot_nt(wr_lo_ref[0], hh) + br_ref[0]
    ts = lt.shape[1]
    per = n_experts // n_groups
    lg = lt[0:n_groups]
    mg = jnp.max(lg, axis=0, keepdims=True)
    p_g = 1.0 / jnp.sum(jnp.exp(lg - mg), axis=0, keepdims=True)
    gi = lax.broadcasted_iota(jnp.int32, lg.shape, 0)
    g_idx = jnp.min(jnp.where(lg == mg, gi, n_groups), axis=0, keepdims=True)
    sel = jnp.zeros((per, ts), F32)
    for g in range(n_groups):
        sel = jnp.where(g_idx == g, lt[n_groups + g * per:n_groups + (g + 1) * per], sel)
    ei = lax.broadcasted_iota(jnp.int32, sel.shape, 0)
    m1 = jnp.max(sel, axis=0, keepdims=True)
    i1 = jnp.min(jnp.where(sel == m1, ei, per), axis=0, keepdims=True)
    rest = jnp.where(ei == i1, -jnp.inf, sel)
    m2 = jnp.max(rest, axis=0, keepdims=True)
    i2 = jnp.min(jnp.where(rest == m2, ei, per), axis=0, keepdims=True)
    e21 = jnp.exp(m2 - m1)
    w1 = p_g / (1.0 + e21)
    w2 = p_g * e21 / (1.0 + e21)
    id1 = g_idx * per + i1
    id2 = g_idx * per + i2
    ids_ref[...] = jnp.concatenate([id1, id2], axis=0)
    wts_ref[...] = jnp.concatenate([w1, w2], axis=0)
    xi = lax.broadcasted_iota(jnp.int32, (n_experts, ts), 0)
    cnt = (xi == id1).astype(F32) + (xi == id2).astype(F32)

    @pl.when(first_step)
    def _():
        hist_ref[...] = cnt

    @pl.when(jnp.logical_not(first_step))
    def _():
        hist_ref[...] += cnt


def _router_specs(d, ts, n_seq_tiles, n_experts, layer):
    in_specs = [
        pl.BlockSpec((1, ROUTER_ROWS, d), lambda b, s: (layer, 0, 0)),
        pl.BlockSpec((1, ROUTER_ROWS, d), lambda b, s: (layer, 0, 0)),
        pl.BlockSpec((1, ROUTER_ROWS, 1), lambda b, s: (layer, 0, 0)),
    ]
    out_specs = [
        pl.BlockSpec((1, ts, d), lambda b, s: (b, s, 0)),
        pl.BlockSpec((ts * SUBLANES, LANES), lambda b, s: (b * n_seq_tiles + s, 0)),
        pl.BlockSpec((MOE_TOP_E, ts), lambda b, s: (0, b * n_seq_tiles + s)),
        pl.BlockSpec((MOE_TOP_E, ts), lambda b, s: (0, b * n_seq_tiles + s)),
        pl.BlockSpec((n_experts, ts), lambda b, s: (0, 0)),
    ]
    return in_specs, out_specs


def _router_out_shapes(b, s, d, ts, n_experts):
    assert d == SUBLANES * LANES, "token-per-tile row layout needs d_model == one (8,128) tile"
    return [
        jax.ShapeDtypeStruct((b, s, d), F32),
        jax.ShapeDtypeStruct((b * s * SUBLANES, LANES), F32),
        jax.ShapeDtypeStruct((MOE_TOP_E, b * s), jnp.int32),
        jax.ShapeDtypeStruct((MOE_TOP_E, b * s), F32),
        jax.ShapeDtypeStruct((n_experts, ts), F32),
    ]


def _moe_specs(moe, ts, n_seq_tiles, d):
    pos = moe[0]
    nt, idx_rows, _ = pos.shape
    assert idx_rows * LANES // MOE_TOP_E == ts, "row tile of the index blocks must equal the mixer's sequence tile"
    tile = lambda bb, ss: bb * n_seq_tiles + ss
    idx_spec = lambda ahead: pl.BlockSpec(
        (1, idx_rows, LANES), lambda bb, ss: (jnp.minimum(tile(bb, ss) + ahead, nt - 1), 0, 0),
        memory_space=pltpu.SMEM)
    in_specs = [idx_spec(0), idx_spec(1),
                pl.BlockSpec((ts, MOE_TOP_E), lambda bb, ss: (tile(bb, ss), 0)),
                pl.BlockSpec((1, 6, d), lambda bb, ss: (bb, 0, 0)),
                pl.BlockSpec(memory_space=pl.ANY)]
    scratch = [pltpu.VMEM((MOE_TOP_E, ts * SUBLANES, LANES), F32),
               pltpu.SemaphoreType.DMA(()),
               pltpu.VMEM((ts, d), F32)]
    return in_specs, scratch


N_MOE_REFS = 5


def _moe_args(moe):
    return (moe[0],) * 2 + tuple(moe[1:]) if moe else ()


def _moe_combine(j, moe_refs, x1p_ref, scratch):
    pos0_ref, pos1_ref, wtp_ref, modp_ref, ysp_ref = moe_refs
    buf, sem, x_s = scratch
    tp = x_s.shape[0]

    def copy(idx_ref, k, row, lane, token, paced=False):
        src = idx_ref[0, row, lane]
        return pltpu.make_async_copy(ysp_ref.at[_tile_rows(_paced(src) if paced else src)],
                                     buf.at[k, _tile_rows(token)], sem)

    def wait():
        for k in range(MOE_TOP_E):
            pltpu.make_async_copy(ysp_ref.at[pl.ds(0, tp * SUBLANES)], buf.at[k], sem).wait()

    @pl.when(j == 0)
    def _():
        _for_each_token_looped(tp, lambda k, row, lane, token: copy(pos0_ref, k, row, lane, token).start(priority=k))

    wait()
    wt = wtp_ref[...]
    gate2 = modp_ref[0, 5:6]
    for s in range(SUBLANES):
        rows = pl.ds(s, tp, stride=SUBLANES)
        cols = slice(s * LANES, (s + 1) * LANES)
        y = wt[:, 0:1] * buf[0, rows, :] + wt[:, 1:2] * buf[1, rows, :]
        x_s[:, cols] = x1p_ref[0, :, cols] + gate2[:, cols] * y

    def issue_ahead():
        _for_each_token(
            tp, lambda k, row, lane, token: copy(pos1_ref, k, row, lane, token, paced=True).start(priority=k))

    return issue_ahead, wait


def _pool_kernel(*refs, n_groups, n_experts, fused, n_tiles):
    moe_refs, refs = (refs[:N_MOE_REFS], refs[N_MOE_REFS:]) if fused else (None, refs)
    (x_ref, mod_ref, gain_ref, pw_ref, pb_ref, ps_ref, wr_hi_ref, wr_lo_ref, br_ref,
     x1_ref, h2_ref, ids_ref, wts_ref, hist_ref, hbuf_ref, sa_ref, sb_ref) = refs[:17]
    b, s = pl.program_id(0), pl.program_id(1)
    if fused:
        tile = b * pl.num_programs(1) + s
        issue_ahead, drain = _moe_combine(tile, moe_refs, x_ref, refs[17:])
        x = refs[19][...]
    else:
        x = x_ref[0]
    ts, d = x.shape
    mod = mod_ref[0]
    h = _norm_mod(x, gain_ref[0:1], mod[1:2], mod[0:1])

    first = SUBLANES + POOL_HALO
    n = first + ts

    @pl.when(s == 0)
    def _():
        hbuf_ref[0:first] = jnp.zeros((first, d), F32)
        sa_ref[0:SUBLANES] = jnp.zeros((SUBLANES, d), F32)
        sb_ref[0:SUBLANES] = jnp.zeros((SUBLANES, d), F32)

    if fused:
        issue_ahead()
    hbuf_ref[first:n] = h
    pos = (s * ts + 1 + lax.broadcasted_iota(jnp.int32, (ts, 1), 0)).astype(F32)
    cg = d // len(POOL_WINDOWS)
    src, dst = hbuf_ref, sa_ref
    ys = []
    for gi, win in enumerate(POOL_WINDOWS):
        half = win // 2
        assert win == 2 ** (gi + 1) and win <= POOL_HALO
        lo = gi * cg
        dst[SUBLANES:n, lo:] = src[SUBLANES:n, lo:] + src[SUBLANES - half:n - half, lo:]
        cols = slice(lo, lo + cg)
        dgi = dst[first:n, cols] / jnp.minimum(pos, float(win)) - h[:, cols]
        ys.append(_dot(dgi.astype(BF16), pw_ref[gi]))
        src, dst = dst, (sb_ref if dst is sa_ref else sa_ref)
    hbuf_ref[SUBLANES:first] = hbuf_ref[n - POOL_HALO:n]
    x1 = x + (jnp.concatenate(ys, axis=-1) + pb_ref[...]) * (ps_ref[...] * mod[2:3])
    x1_ref[0] = x1
    _router_epilogue(x1, mod, gain_ref[1:2], wr_hi_ref, wr_lo_ref, br_ref, n_groups, n_experts,
                     h2_ref, ids_ref, wts_ref, hist_ref, jnp.logical_and(b == 0, s == 0))
    if fused:
        pl.when(tile == n_tiles - 1)(drain)


def _pool_layer(x, moe, mod, gain, pw, pb, ps, router, layer, n_groups, n_experts):
    b, s, d = x.shape
    ts = min(SEQ_TILE_POOL, s)
    nst = s // ts
    g, cg = pw.shape[0], pw.shape[1]
    r_in, r_out = _router_specs(d, ts, nst, n_experts, layer)
    m_in, m_scratch = _moe_specs(moe, ts, nst, d) if moe else ([], [])
    return pl.pallas_call(
        functools.partial(_pool_kernel, n_groups=n_groups, n_experts=n_experts, fused=bool(moe), n_tiles=b * nst),
        grid=(b, nst),
        in_specs=m_in + [
            pl.BlockSpec((1, ts, d), lambda bb, ss: (bb, ss, 0)),
            pl.BlockSpec((1, 6, d), lambda bb, ss: (bb, 0, 0)),
            pl.BlockSpec((2, d), lambda bb, ss: (0, 0)),
            pl.BlockSpec((g, cg, cg), lambda bb, ss: (0, 0, 0)),
            pl.BlockSpec((1, d), lambda bb, ss: (0, 0)),
            pl.BlockSpec((1, d), lambda bb, ss: (0, 0)),
        ] + r_in,
        out_specs=r_out,
        out_shape=_router_out_shapes(b, s, d, ts, n_experts),
        scratch_shapes=[pltpu.VMEM((SUBLANES + POOL_HALO + ts, d), F32)] * 3 + m_scratch,
        compiler_params=pltpu.CompilerParams(
            dimension_semantics=("arbitrary", "arbitrary"), vmem_limit_bytes=VMEM_LIMIT),
    )(*_moe_args(moe), x, mod, gain, pw.astype(BF16), pb.reshape(1, d), ps.reshape(1, d), *router)


N_GLA_REFS = 28


def _gla_kernel(*refs, n_heads, n_groups, n_experts, fused, n_tiles):
    moe_refs, refs = (refs[:N_MOE_REFS], refs[N_MOE_REFS:]) if fused else (None, refs)
    (x_ref, mod_ref, gain_ref, wp_ref, wz_ref, wg_ref, bg_ref, ng_ref, wo_ref, wr_hi_ref, wr_lo_ref, br_ref,
     x1_ref, h2_ref, ids_ref, wts_ref, hist_ref, state_ref, q_s, k_s, g_s, v_s, r_s, o_s,
     qe_s, kd_s, el_s, oi_s) = refs[:N_GLA_REFS]
    b, s = pl.program_id(0), pl.program_id(1)
    if fused:
        tile = b * pl.num_programs(1) + s
        issue_ahead, drain = _moe_combine(tile, moe_refs, x_ref, refs[N_GLA_REFS:])
        x = refs[N_GLA_REFS + 2][...]
    else:
        x = x_ref[0]
    ts, d = x.shape
    dk_all = q_s.shape[1]
    dk = dk_all // n_heads
    dv = d // n_heads
    mod = mod_ref[0]
    h = _norm_mod(x, gain_ref[0:1], mod[1:2], mod[0:1]).astype(BF16)

    @pl.when(s == 0)
    def _():
        state_ref[...] = jnp.zeros(state_ref.shape, F32)

    if fused:
        issue_ahead()
    q_s[...] = _dot(h, wp_ref[:, 0:dk_all]) * (dk ** -0.5)
    k_s[...] = _dot(h, wp_ref[:, dk_all:2 * dk_all])
    v_s[...] = _dot(h, wp_ref[:, 2 * dk_all:2 * dk_all + d]).astype(BF16)
    r = _dot(h, wp_ref[:, 2 * dk_all + d:2 * dk_all + 2 * d])
    r_s[...] = r * (1.0 / (1.0 + jnp.exp(-r)))
    z = _dot(h, wz_ref[...]).astype(BF16)
    u = _dot(z, wg_ref[...]) + bg_ref[...]
    g_s[...] = (jnp.minimum(u, 0.0) - jnp.log(1.0 + jnp.exp(-jnp.abs(u)))) * (1.0 / GLA_GATE_TEMP)

    c = GLA_CHUNK
    ri = lax.broadcasted_iota(jnp.int32, (c, c), 0)
    ci = lax.broadcasted_iota(jnp.int32, (c, c), 1)
    causal = ri >= ci
    tril = jnp.where(causal, 1.0, 0.0).astype(BF16)
    ng = ng_ref[...]

    heads = [(slice(hd * dk, (hd + 1) * dk), slice(hd * dv, (hd + 1) * dv)) for hd in range(n_heads)]
    group = min(GLA_CHUNK_GROUP, ts // c)

    def intra(m, carry):
        ns = [m * group + i for i in range(group)]
        rows = [pl.ds(pl.multiple_of(n * c, c), c) for n in ns]
        gs = [_split_bf16(g_s[r, :]) for r in rows]
        bcs = [_dot(tril, hi) + _dot(tril, lo) for hi, lo in gs]
        qes, kes = [], []
        for n, r, bc in zip(ns, rows, bcs):
            bl = bc[c - 1:c, :]
            kk = k_s[r, :]
            qes.append((q_s[r, :] * jnp.exp(bc)).astype(BF16))
            kes.append((kk * jnp.exp(-bc)).astype(BF16))
            qe_s[r, :] = qes[-1]
            kd_s[r, :] = (kk * jnp.exp(bl - bc)).astype(BF16)
            el_s[pl.ds(n, 1), :] = jnp.exp(bl)
        scs = [[_dot_nt(q_e[:, ks], k_e[:, ks]) for ks, _ in heads] for q_e, k_e in zip(qes, kes)]
        scs = [[jnp.where(causal, sc, 0.0).astype(BF16) for sc in per_chunk] for per_chunk in scs]
        for r, per_chunk in zip(rows, scs):
            vv = v_s[r, :]
            for sc, (_, vs) in zip(per_chunk, heads):
                oi_s[r, vs] = _dot(sc, vv[:, vs])
        return carry

    lax.fori_loop(0, ts // c // group, intra, 0)

    def inter(m, carry):
        ns = [m * group + i for i in range(group)]
        rows = [pl.ds(pl.multiple_of(n * c, c), c) for n in ns]
        incs = []
        for r in rows:
            vv, k_d = v_s[r, :], kd_s[r, :]
            incs.append([_dot_tn(vv[:, vs], k_d[:, ks]) for ks, vs in heads])
        for n, r, inc in zip(ns, rows, incs):
            q_e = qe_s[r, :]
            e_l = el_s[pl.ds(n, 1), :]
            sts = [state_ref[hd] for hd in range(n_heads)]
            carried = [_dot_nt(q_e[:, ks], st.astype(BF16)) for (ks, _), st in zip(heads, sts)]
            for hd, (ks, _) in enumerate(heads):
                state_ref[hd] = sts[hd] * e_l[:, ks] + inc[hd]
            for o_c, (_, vs) in zip(carried, heads):
                o = oi_s[r, vs] + o_c
                o = o * lax.rsqrt(jnp.mean(o * o, axis=-1, keepdims=True) + NORM_EPS) * ng
                o_s[r, vs] = (o * r_s[r, vs]).astype(BF16)
        return carry

    lax.fori_loop(0, ts // c // group, inter, 0)
    x1 = x + mod[2:3] * _dot(o_s[...], wo_ref[...])
    x1_ref[0] = x1
    _router_epilogue(x1, mod, gain_ref[1:2], wr_hi_ref, wr_lo_ref, br_ref, n_groups, n_experts,
                     h2_ref, ids_ref, wts_ref, hist_ref, jnp.logical_and(b == 0, s == 0))
    if fused:
        pl.when(tile == n_tiles - 1)(drain)


def _gla_layer(x, moe, mod, gain, w_in, w_gate, b_gate, norm_g, w_out, router, layer, n_groups, n_experts):
    b, s, d = x.shape
    ts = min(SEQ_TILE_GLA, s)
    nst = s // ts
    rank, dk_all = w_gate.shape
    dv = norm_g.shape[0]
    n_heads = d // dv
    n_proj = 2 * dk_all + 2 * d
    wp = w_in[:, :n_proj].astype(BF16)
    wz = jnp.pad(w_in[:, n_proj:], ((0, 0), (0, LANES - rank))).astype(BF16)
    wg = jnp.pad(w_gate, ((0, LANES - rank), (0, 0))).astype(BF16)
    r_in, r_out = _router_specs(d, ts, nst, n_experts, layer)
    m_in, m_scratch = _moe_specs(moe, ts, nst, d) if moe else ([], [])
    const = lambda bb, ss: (0, 0)
    once = pl.Buffered(1)
    return pl.pallas_call(
        functools.partial(_gla_kernel, n_heads=n_heads, n_groups=n_groups, n_experts=n_experts, fused=bool(moe),
                          n_tiles=b * nst),
        grid=(b, nst),
        in_specs=m_in + [
            pl.BlockSpec((1, ts, d), lambda bb, ss: (bb, ss, 0)),
            pl.BlockSpec((1, 6, d), lambda bb, ss: (bb, 0, 0)),
            pl.BlockSpec((2, d), const),
            pl.BlockSpec((d, n_proj), const, pipeline_mode=once),
            pl.BlockSpec((d, LANES), const),
            pl.BlockSpec((LANES, dk_all), const),
            pl.BlockSpec((1, dk_all), const),
            pl.BlockSpec((1, dv), const),
            pl.BlockSpec((d, d), const, pipeline_mode=once),
        ] + r_in,
        out_specs=r_out,
        out_shape=_router_out_shapes(b, s, d, ts, n_experts),
        scratch_shapes=[
            pltpu.VMEM((n_heads, dv, dk_all // n_heads), F32),
            pltpu.VMEM((ts, dk_all), F32),
            pltpu.VMEM((ts, dk_all), F32),
            pltpu.VMEM((ts, dk_all), F32),
            pltpu.VMEM((ts, d), BF16),
            pltpu.VMEM((ts, d), F32),
            pltpu.VMEM((ts, d), BF16),
            pltpu.VMEM((ts, dk_all), BF16),
            pltpu.VMEM((ts, dk_all), BF16),
            pltpu.VMEM((ts // GLA_CHUNK, dk_all), F32),
            pltpu.VMEM((ts, d), F32),
        ] + m_scratch,
        compiler_params=pltpu.CompilerParams(
            dimension_semantics=("arbitrary", "arbitrary"), vmem_limit_bytes=VMEM_LIMIT),
    )(*_moe_args(moe), x, mod, gain, wp, wz, wg, b_gate.reshape(1, dk_all), norm_g.reshape(1, dv), w_out.astype(BF16),
      *router)


def _pos_kernel(ids_ref, off_ref, pos_ref, run_ref):
    i = pl.program_id(0)
    tiles, idx_rows, _ = pos_ref.shape
    tp = idx_rows * LANES // MOE_TOP_E
    n_e = off_ref.shape[0]

    @pl.when(i == 0)
    def _():
        run_ref[...] = jnp.zeros(run_ref.shape, F32)

    ei = lax.broadcasted_iota(jnp.int32, (n_e, tp), 0)
    tri = jnp.where(lax.broadcasted_iota(jnp.int32, (tp, tp), 0) <= lax.broadcasted_iota(jnp.int32, (tp, tp), 1),
                    1.0, 0.0).astype(BF16)
    base = off_ref[...] + run_ref[...]
    for u in range(tiles):
        ids = ids_ref[:, u * tp:(u + 1) * tp]
        oh0 = ei == ids[0:1]
        oh1 = ei == ids[1:2]
        inc0 = _dot(jnp.where(oh0, 1.0, 0.0).astype(BF16), tri)
        inc1 = _dot(jnp.where(oh1, 1.0, 0.0).astype(BF16), tri)
        tot0 = inc0[:, tp - 1:tp]
        tot1 = inc1[:, tp - 1:tp]
        p0 = jnp.sum(jnp.where(oh0, base + inc0 - 1.0, 0.0), axis=0, keepdims=True)
        p1 = jnp.sum(jnp.where(oh1, base + tot0 + inc1 - 1.0, 0.0), axis=0, keepdims=True)
        p = [pk[:, c * LANES:(c + 1) * LANES] for pk in (p0, p1) for c in range(tp // LANES)]
        pos_ref[u] = jnp.concatenate(p, axis=0).astype(jnp.int32)
        base = base + tot0 + tot1
    run_ref[...] = base - off_ref[...]


def _positions(ids, offsets):
    t = ids.shape[1]
    tp = min(ROW_TILE, t)
    tiles = min(POS_TILES_PER_STEP, t // tp)
    n_e = offsets.shape[0]
    return pl.pallas_call(
        _pos_kernel,
        grid=(t // (tp * tiles),),
        in_specs=[pl.BlockSpec((MOE_TOP_E, tp * tiles), lambda i: (0, i)),
                  pl.BlockSpec((n_e, 1), lambda i: (0, 0))],
        out_specs=pl.BlockSpec((tiles, MOE_TOP_E * tp // LANES, LANES), lambda i: (i, 0, 0)),
        out_shape=jax.ShapeDtypeStruct((t // tp, MOE_TOP_E * tp // LANES, LANES), jnp.int32),
        scratch_shapes=[pltpu.VMEM((n_e, 1), F32)],
        compiler_params=pltpu.CompilerParams(dimension_semantics=("arbitrary",)),
    )(ids, offsets)


def _tile_rows(row):
    return pl.ds(pl.multiple_of(row * SUBLANES, SUBLANES), SUBLANES)


def _for_each_token(tp, fn):
    chunks = tp // LANES
    for c in range(chunks):
        for lane in range(LANES):
            for k in range(MOE_TOP_E):
                fn(k, k * chunks + c, lane, c * LANES + lane)


def _for_each_token_looped(tp, fn):
    chunks = tp // LANES
    for c in range(chunks):
        def body(lane, carry):
            for k in range(MOE_TOP_E):
                fn(k, k * chunks + c, lane, c * LANES + lane)
            return carry
        lax.fori_loop(0, LANES, body, 0)


def _inverse_kernel(pos_ref, inv_ref):
    tp = pos_ref.shape[1] * LANES // MOE_TOP_E
    base = pl.program_id(0) * tp

    def put(k, row, lane, token):
        inv_ref[pos_ref[0, row, lane]] = base + token

    _for_each_token(tp, put)


def _inverse_positions(pos, n_rows):
    nt, idx_rows, _ = pos.shape
    return pl.pallas_call(
        _inverse_kernel,
        grid=(nt,),
        in_specs=[pl.BlockSpec((1, idx_rows, LANES), lambda i: (i, 0, 0), memory_space=pltpu.SMEM)],
        out_specs=pl.BlockSpec(memory_space=pltpu.SMEM),
        out_shape=jax.ShapeDtypeStruct((n_rows,), jnp.int32),
        compiler_params=pltpu.CompilerParams(dimension_semantics=("arbitrary",)),
    )(pos)


def _gmm_kernel(it_ref, iq_ref, lo_ref, hi_ref, qe_ref, n_ref, nq_ref, tok0_ref, tok1_ref, tok2_ref,
                h_hbm, wi_hbm, wo_hbm, ys_ref, wi_s, wo_s, a_s, acc_s, wi_f, wo_f, xb0, xb1, xb2, sem_i, sem_o, sem_x,
                *, layer):
    w = pl.program_id(0)
    f = wo_s.shape[0]
    tm = acc_s.shape[0]
    n = n_ref[0]
    xbufs = (xb0, xb1, xb2)

    def wi_copy(q, slot):
        return pltpu.make_async_copy(wi_hbm.at[layer, qe_ref[q]], wi_f.at[slot], sem_i.at[slot])

    def wo_copy(q, slot):
        return pltpu.make_async_copy(wo_hbm.at[layer, qe_ref[q]], wo_f.at[slot], sem_o.at[slot])

    def row_copy(tok_ref, half, lane, slot, paced=False):
        tok = tok_ref[0, half, lane]
        return pltpu.make_async_copy(h_hbm.at[_tile_rows(_paced(tok) if paced else tok)],
                                     xbufs[slot].at[_tile_rows(half * LANES + lane)], sem_x.at[slot])

    def wait_rows(slot):
        pltpu.make_async_copy(h_hbm.at[pl.ds(0, tm * SUBLANES)], xbufs[slot], sem_x.at[slot]).wait()

    def gather(tok_ref, slot, unrolled):
        for half in range(tm // LANES):
            if unrolled:
                for lane in range(LANES):
                    row_copy(tok_ref, half, lane, slot, paced=True).start(priority=lane % 2)
            else:
                def body(lane, carry):
                    row_copy(tok_ref, half, lane, slot).start()
                    return carry
                lax.fori_loop(0, LANES, body, 0)

    @pl.when(w <= n)
    def _():
        ja = jnp.minimum(w, n - 1)
        jb = jnp.maximum(w - 1, 0)
        slot_a = lax.rem(w, 2)
        qa = iq_ref[ja]
        qb = iq_ref[jb]

        @pl.when(w == 0)
        def _():
            wi_copy(0, 0).start()
            wo_copy(0, 0).start()
            gather(tok0_ref, 0, False)
            gather(tok1_ref, 1, False)
            a_s[1] = jnp.zeros(a_s.shape[1:], BF16)
            acc_s[...] = jnp.zeros(acc_s.shape, F32)
            wo_s[...] = jnp.zeros(wo_s.shape, BF16)

        @pl.when(jnp.logical_or(w == 1, jnp.logical_and(w >= 2, qb != iq_ref[jnp.maximum(w - 2, 0)])))
        def _():
            slot = lax.rem(qb, 2)
            wo_copy(qb, slot).wait()
            wo_s[...] = wo_f[slot].astype(BF16)

        @pl.when(jnp.logical_or(w == 0, qa != qb))
        def _():
            @pl.when(qa + 1 < nq_ref[0])
            def _():
                wi_copy(qa + 1, lax.rem(qa + 1, 2)).start()
                wo_copy(qa + 1, lax.rem(qa + 1, 2)).start()

            slot = lax.rem(qa, 2)
            wi_copy(qa, slot).wait()
            wi_s[...] = wi_f[slot].astype(BF16)

        def second_stage():
            y = _dot(a_s[1 - slot_a], wo_s[...])
            row = lax.broadcasted_iota(jnp.int32, (tm, 1), 0)
            mine = jnp.logical_and(jnp.logical_and(row >= lo_ref[jb], row < hi_ref[jb]), w >= 1)
            merged = jnp.where(mine, y, acc_s[...])
            acc_s[...] = merged
            _store_token_tiles(ys_ref, merged)

        for slot in range(XS_RING):
            @pl.when(jnp.logical_and(lax.rem(w, XS_RING) == slot, w < n))
            def _():
                wait_rows(slot)
                x = jnp.concatenate(
                    [_load_token_tiles(xbufs[slot], tm, s).astype(BF16) for s in range(SUBLANES)], axis=-1)
                gather(tok2_ref, (slot + 2) % XS_RING, True)
                second_stage()
                gu = _dot(x, wi_s[...])
                gate, up = gu[:, :f], gu[:, f:]
                a_s[slot_a] = (gate * (1.0 / (1.0 + jnp.exp(-gate))) * up).astype(BF16)

            @pl.when(jnp.logical_and(lax.rem(w, XS_RING) == slot, w == n))
            def _():
                second_stage()
                wait_rows(slot)
                wait_rows((slot + 1) % XS_RING)


def _grouped_mlp(plan, h2, inv, w_in, w_out, layer):
    item_tile, item_seq, item_lo, item_hi, seq_expert, n_items, n_seq = plan
    d, f2 = w_in.shape[-2:]
    f = f2 // 2
    tm = GMM_TILE
    rows = tm * SUBLANES
    tok = inv.reshape(-1, tm // LANES, LANES)

    def item(ahead):
        return lambda w, it, iq, lo, hi, qe, n, nq: (it[jnp.minimum(w + ahead, n[0] - 1)], 0, 0)

    def second(w, n):
        return jnp.minimum(jnp.maximum(w - 1, 0), n[0] - 1)

    tok_spec = lambda ahead: pl.BlockSpec((1, tm // LANES, LANES), item(ahead), memory_space=pltpu.SMEM)
    return pl.pallas_call(
        functools.partial(_gmm_kernel, layer=layer),
        grid_spec=pltpu.PrefetchScalarGridSpec(
            num_scalar_prefetch=7,
            grid=(item_tile.shape[0] + 1,),
            in_specs=[tok_spec(0), tok_spec(1), tok_spec(2)] + [pl.BlockSpec(memory_space=pl.ANY)] * 3,
            out_specs=pl.BlockSpec((rows, LANES), lambda w, it, iq, lo, hi, qe, n, nq: (it[second(w, n)], 0)),
            scratch_shapes=[pltpu.VMEM((d, f2), BF16), pltpu.VMEM((f, d), BF16), pltpu.VMEM((2, tm, f), BF16),
                            pltpu.VMEM((tm, d), F32), pltpu.VMEM((2, d, f2), F32), pltpu.VMEM((2, f, d), F32)]
            + [pltpu.VMEM((rows, LANES), F32)] * XS_RING
            + [pltpu.SemaphoreType.DMA((2,)), pltpu.SemaphoreType.DMA((2,)), pltpu.SemaphoreType.DMA((XS_RING,))],
        ),
        out_shape=jax.ShapeDtypeStruct((inv.shape[0] * SUBLANES, LANES), F32),
        compiler_params=pltpu.CompilerParams(
            dimension_semantics=("arbitrary",), vmem_limit_bytes=VMEM_LIMIT),
    )(item_tile, item_seq, item_lo, item_hi, seq_expert, n_items, n_seq, tok, tok, tok, h2, w_in, w_out)


def _combine_kernel(pos0_ref, pos1_ref, pos2_ref, x1_ref, wt_ref, mod_ref, fg_ref, ys_ref, o_ref, *scratch, final):
    bufs, sems = scratch[:COMBINE_RING], scratch[COMBINE_RING]
    i = pl.program_id(0)
    n = pl.num_programs(0)
    tp = x1_ref.shape[0]

    def issue(idx_ref, slot):
        def one(k, row, lane, token):
            pltpu.make_async_copy(ys_ref.at[_tile_rows(idx_ref[0, row, lane])], bufs[slot].at[k, _tile_rows(token)],
                                  sems.at[slot]).start(priority=k)
        _for_each_token(tp, one)

    def compute(slot):
        wt = wt_ref[...]
        gate2 = mod_ref[0, 5:6]
        chunks = []
        for s in range(SUBLANES):
            rows = pl.ds(s, tp, stride=SUBLANES)
            cols = slice(s * LANES, (s + 1) * LANES)
            y = wt[:, 0:1] * bufs[slot][0, rows, :] + wt[:, 1:2] * bufs[slot][1, rows, :]
            chunks.append(x1_ref[:, cols] + gate2[:, cols] * y)
        x = jnp.concatenate(chunks, axis=-1)
        if final:
            x = x * lax.rsqrt(jnp.mean(x * x, axis=-1, keepdims=True) + NORM_EPS) * fg_ref[...]
        o_ref[...] = x

    @pl.when(i == 0)
    def _():
        issue(pos0_ref, 0)

        @pl.when(n > 1)
        def _():
            issue(pos1_ref, 1)

    for slot in range(COMBINE_RING):
        @pl.when(lax.rem(i, COMBINE_RING) == slot)
        def _():
            for k in range(MOE_TOP_E):
                pltpu.make_async_copy(ys_ref.at[pl.ds(0, tp * SUBLANES)], bufs[slot].at[k], sems.at[slot]).wait()

            @pl.when(i + 2 < n)
            def _():
                issue(pos2_ref, (slot + 2) % COMBINE_RING)
                compute(slot)

            @pl.when(i + 2 >= n)
            def _():
                compute(slot)


def _combine(pos, x1, wts_t, mod, final_g, ys, seq_len, final):
    nt, idx_rows, _ = pos.shape
    t, d = x1.shape
    tp = t // nt
    per_seq = seq_len // tp
    idx_spec = lambda ahead: pl.BlockSpec((1, idx_rows, LANES), lambda i: (jnp.minimum(i + ahead, nt - 1), 0, 0),
                                          memory_space=pltpu.SMEM)
    return pl.pallas_call(
        functools.partial(_combine_kernel, final=final),
        grid=(nt,),
        in_specs=[
            idx_spec(0), idx_spec(1), idx_spec(2),
            pl.BlockSpec((tp, d), lambda i: (i, 0)),
            pl.BlockSpec((tp, MOE_TOP_E), lambda i: (i, 0)),
            pl.BlockSpec((1, 6, d), lambda i: (i // per_seq, 0, 0)),
            pl.BlockSpec((1, d), lambda i: (0, 0)),
            pl.BlockSpec(memory_space=pl.ANY),
        ],
        out_specs=pl.BlockSpec((tp, d), lambda i: (i, 0)),
        out_shape=jax.ShapeDtypeStruct((t, d), F32),
        scratch_shapes=[pltpu.VMEM((MOE_TOP_E, tp * SUBLANES, LANES), F32)] * COMBINE_RING
        + [pltpu.SemaphoreType.DMA((COMBINE_RING,))],
        compiler_params=pltpu.CompilerParams(
            dimension_semantics=("arbitrary",), vmem_limit_bytes=VMEM_LIMIT),
    )(pos, pos, pos, x1, wts_t, mod, final_g, ys)


def _router_weights(w_group, b_group, w_expert, b_expert):
    g, e = w_group.shape[-1], w_expert.shape[-1]
    pad = ROUTER_ROWS - g - e
    w = jnp.swapaxes(jnp.concatenate([w_group, w_expert], axis=2), 1, 2)
    hi, lo = _split_bf16(jnp.pad(w, ((0, 0), (0, pad), (0, 0))))
    bias = jnp.pad(jnp.concatenate([b_group, b_expert], axis=1), ((0, 0), (0, pad)))[:, :, None]
    return hi, lo, bias


def _tile_plan(hist, n_rows):
    n_e = hist.shape[0]
    tm = GMM_TILE
    counts = jnp.sum(hist, axis=1).astype(jnp.int32)
    ends = jnp.cumsum(counts)
    starts = ends - counts
    first_tile = starts // tm
    items_per = jnp.where(counts > 0, (ends - 1) // tm - first_tile + 1, 0)
    item_ends = jnp.cumsum(items_per)
    n_items = item_ends[-1:]
    max_items = n_rows // tm + n_e - 1
    w = jnp.minimum(jnp.arange(max_items, dtype=jnp.int32), n_items[0] - 1)
    item_expert = jnp.sum(w[:, None] >= item_ends[None, :], axis=1).astype(jnp.int32)
    own = item_expert[:, None] == jnp.arange(n_e, dtype=jnp.int32)[None, :]
    pick = lambda per_expert: jnp.sum(jnp.where(own, per_expert[None, :], 0), axis=1)
    item_tile = pick(first_tile) + w - pick(item_ends - items_per)
    item_lo = jnp.maximum(pick(starts) - item_tile * tm, 0)
    item_hi = jnp.minimum(pick(ends) - item_tile * tm, tm)
    offsets = starts.astype(F32).reshape(n_e, 1)
    used_ends = jnp.cumsum((counts > 0).astype(jnp.int32))
    n_seq = used_ends[-1:]
    q = jnp.minimum(jnp.arange(n_e, dtype=jnp.int32), n_seq[0] - 1)
    seq_expert = jnp.sum(q[:, None] >= used_ends[None, :], axis=1).astype(jnp.int32)
    item_seq = pick(used_ends - 1)
    return offsets, (item_tile.astype(jnp.int32), item_seq.astype(jnp.int32), item_lo.astype(jnp.int32),
                     item_hi.astype(jnp.int32), seq_expert, n_items.astype(jnp.int32), n_seq.astype(jnp.int32))


def kernel(x, c, norm_gain, w_mod, b_mod, pool_w, pool_b, pool_scale, gla_w_in, gla_w_gate, gla_b_gate, gla_norm_g, gla_w_out, moe_w_group, moe_b_group, moe_w_expert, moe_b_expert, moe_w_in, moe_w_out, final_norm_g):
    b, s, d = x.shape
    depth = w_mod.shape[0]
    t = b * s
    n_groups = moe_w_group.shape[-1]
    n_experts = moe_w_expert.shape[-1]
    n_rows = MOE_TOP_E * t
    mod_all = _modulation(c, w_mod, b_mod).reshape(depth, b, 6, d)
    fg = final_norm_g.reshape(1, d)
    router = _router_weights(moe_w_group, moe_b_group, moe_w_expert, moe_b_expert)
    moe = None
    for i in range(depth):
        mod = mod_all[i]
        j = i // 2
        if i % 2 == 0:
            x1, h2, ids, wts, hist = _pool_layer(x, moe, mod, norm_gain[i], pool_w[j], pool_b[j], pool_scale[j],
                                                 router, i, n_groups, n_experts)
        else:
            x1, h2, ids, wts, hist = _gla_layer(x, moe, mod, norm_gain[i], gla_w_in[j], gla_w_gate[j], gla_b_gate[j],
                                                gla_norm_g[j], gla_w_out[j], router, i, n_groups, n_experts)
        offsets, plan = _tile_plan(hist, n_rows)
        pos = _positions(ids, offsets)
        inv = _inverse_positions(pos, n_rows)
        ys = _grouped_mlp(plan, h2, inv, moe_w_in, moe_w_out, i)
        x, moe = x1, (pos, wts.T, mod, ys)
    pos, wts_t, mod, ys = moe
    x = _combine(pos, x.reshape(t, d), wts_t, mod, fg, ys, s, final=True).reshape(b, s, d)
    return x
```

```python
import functools

import jax
import jax.numpy as jnp
from jax import lax
from jax.experimental import pallas as pl
from jax.experimental.pallas import tpu as pltpu

F32 = jnp.float32
BF16 = jnp.bfloat16

NORM_EPS = 1e-6
POOL_WINDOWS = (2, 4, 8, 16)
POOL_HALO = 16
GLA_CHUNK = 64
GLA_GATE_TEMP = 16.0
GLA_CHUNK_GROUP = 8
MOE_TOP_E = 2
LANES = 128
SUBLANES = 8
ROUTER_ROWS = 128
VMEM_LIMIT = 56 * 1024 * 1024

SEQ_TILE_POOL = 512
SEQ_TILE_GLA = 512
GMM_TILE = 256
XS_RING = 3
COMBINE_RING = 3
ROW_TILE = 512
POS_TILES_PER_STEP = 4


def _dot(a, b):
    return jnp.dot(a, b, preferred_element_type=F32)


def _dot_nt(a, b):
    return lax.dot_general(a, b, (((1,), (1,)), ((), ())), preferred_element_type=F32)


def _dot_tn(a, b):
    return lax.dot_general(a, b, (((0,), (0,)), ((), ())), preferred_element_type=F32)


def _split_bf16(x):
    hi = x.astype(BF16)
    lo = (x - hi.astype(F32)).astype(BF16)
    return hi, lo


def _store_token_tiles(ref, x):
    n = x.shape[0]
    for s in range(SUBLANES):
        ref[pl.ds(s, n, stride=SUBLANES), :] = x[:, s * LANES:(s + 1) * LANES]


def _load_token_tiles(ref, n, s):
    return ref[pl.ds(s, n, stride=SUBLANES), :]


def _norm_mod(x, gain, scale, shift):
    ms = jnp.mean(x * x, axis=-1, keepdims=True)
    return x * lax.rsqrt(ms + NORM_EPS) * (gain * (1.0 + scale)) + shift


def _mod_kernel(c_ref, w_ref, b_ref, o_ref):
    c = c_ref[...]
    sc = (c * (1.0 / (1.0 + jnp.exp(-c)))).astype(BF16)
    o_ref[0] = _dot(sc, w_ref[0].astype(BF16)) + b_ref[0]


def _modulation(c, w_mod, b_mod):
    depth, d, n = w_mod.shape
    b = c.shape[0]
    tn = n // 4
    return pl.pallas_call(
        _mod_kernel,
        grid=(depth, n // tn),
        in_specs=[
            pl.BlockSpec((b, d), lambda i, j: (0, 0)),
            pl.BlockSpec((1, d, tn), lambda i, j: (i, 0, j)),
            pl.BlockSpec((1, 1, tn), lambda i, j: (i, 0, j)),
        ],
        out_specs=pl.BlockSpec((1, b, tn), lambda i, j: (i, 0, j)),
        out_shape=jax.ShapeDtypeStruct((depth, b, n), F32),
        compiler_params=pltpu.CompilerParams(vmem_limit_bytes=VMEM_LIMIT),
    )(c, w_mod, b_mod.reshape(depth, 1, n))


def _router_epilogue(x1, mod, gain2, wr_hi_ref, wr_lo_ref, br_ref, n_groups, n_experts,
                     h2_ref, ids_ref, wts_ref, hist_ref, first_step):
    h2 = _norm_mod(x1, gain2, mod[4:5], mod[3:4])
    _store_token_tiles(h2_ref, h2)
    hh, hl = _split_bf16(h2)
    wh = wr_hi_ref[0]
    lt = _dot_nt(wh, hh) + _dot_nt(wh, hl) + _dot_nt(wr_lo_ref[0], hh) + br_ref[0]
    ts = lt.shape[1]
    per = n_experts // n_groups
    lg = lt[0:n_groups]
    mg = jnp.max(lg, axis=0, keepdims=True)
    p_g = 1.0 / jnp.sum(jnp.exp(lg - mg), axis=0, keepdims=True)
    gi = lax.broadcasted_iota(jnp.int32, lg.shape, 0)
    g_idx = jnp.min(jnp.where(lg == mg, gi, n_groups), axis=0, keepdims=True)
    sel = jnp.zeros((per, ts), F32)
    for g in range(n_groups):
        sel = jnp.where(g_idx == g, lt[n_groups + g * per:n_groups + (g + 1) * per], sel)
    ei = lax.broadcasted_iota(jnp.int32, sel.shape, 0)
    m1 = jnp.max(sel, axis=0, keepdims=True)
    i1 = jnp.min(jnp.where(sel == m1, ei, per), axis=0, keepdims=True)
    rest = jnp.where(ei == i1, -jnp.inf, sel)
    m2 = jnp.max(rest, axis=0, keepdims=True)
    i2 = jnp.min(jnp.where(rest == m2, ei, per), axis=0, keepdims=True)
    e21 = jnp.exp(m2 - m1)
    w1 = p_g / (1.0 + e21)
    w2 = p_g * e21 / (1.0 + e21)
    id1 = g_idx * per + i1
    id2 = g_idx * per + i2
    ids_ref[...] = jnp.concatenate([id1, id2], axis=0)
    wts_ref[...] = jnp.concatenate([w1, w2], axis=0)
    xi = lax.broadcasted_iota(jnp.int32, (n_experts, ts), 0)
    cnt = (xi == id1).astype(F32) + (xi == id2).astype(F32)

    @pl.when(first_step)
    def _():
        hist_ref[...] = cnt

    @pl.when(jnp.logical_not(first_step))
    def _():
        hist_ref[...] += cnt


def _router_specs(d, ts, n_seq_tiles, n_experts, layer):
    in_specs = [
        pl.BlockSpec((1, ROUTER_ROWS, d), lambda b, s: (layer, 0, 0)),
        pl.BlockSpec((1, ROUTER_ROWS, d), lambda b, s: (layer, 0, 0)),
        pl.BlockSpec((1, ROUTER_ROWS, 1), lambda b, s: (layer, 0, 0)),
    ]
    out_specs = [
        pl.BlockSpec((1, ts, d), lambda b, s: (b, s, 0)),
        pl.BlockSpec((ts * SUBLANES, LANES), lambda b, s: (b * n_seq_tiles + s, 0)),
        pl.BlockSpec((MOE_TOP_E, ts), lambda b, s: (0, b * n_seq_tiles + s)),
        pl.BlockSpec((MOE_TOP_E, ts), lambda b, s: (0, b * n_seq_tiles + s)),
        pl.BlockSpec((n_experts, ts), lambda b, s: (0, 0)),
    ]
    return in_specs, out_specs


def _router_out_shapes(b, s, d, ts, n_experts):
    assert d == SUBLANES * LANES, "token-per-tile row layout needs d_model == one (8,128) tile"
    return [
        jax.ShapeDtypeStruct((b, s, d), F32),
        jax.ShapeDtypeStruct((b * s * SUBLANES, LANES), F32),
        jax.ShapeDtypeStruct((MOE_TOP_E, b * s), jnp.int32),
        jax.ShapeDtypeStruct((MOE_TOP_E, b * s), F32),
        jax.ShapeDtypeStruct((n_experts, ts), F32),
    ]


def _moe_specs(moe, ts, n_seq_tiles, d):
    pos = moe[0]
    nt, idx_rows, _ = pos.shape
    assert idx_rows * LANES // MOE_TOP_E == ts, "row tile of the index blocks must equal the mixer's sequence tile"
    tile = lambda bb, ss: bb * n_seq_tiles + ss
    idx_spec = lambda ahead: pl.BlockSpec(
        (1, idx_rows, LANES), lambda bb, ss: (jnp.minimum(tile(bb, ss) + ahead, nt - 1), 0, 0),
        memory_space=pltpu.SMEM)
    in_specs = [idx_spec(0), idx_spec(1),
                pl.BlockSpec((ts, MOE_TOP_E), lambda bb, ss: (tile(bb, ss), 0)),
                pl.BlockSpec((1, 6, d), lambda bb, ss: (bb, 0, 0)),
                pl.BlockSpec(memory_space=pl.ANY)]
    scratch = [pltpu.VMEM((MOE_TOP_E, ts * SUBLANES, LANES), F32),
               pltpu.SemaphoreType.DMA(()),
               pltpu.VMEM((ts, d), F32)]
    return in_specs, scratch


N_MOE_REFS = 5


def _moe_args(moe):
    return (moe[0],) * 2 + tuple(moe[1:]) if moe else ()


def _moe_combine(j, moe_refs, x1p_ref, scratch):
    pos0_ref, pos1_ref, wtp_ref, modp_ref, ysp_ref = moe_refs
    buf, sem, x_s = scratch
    tp = x_s.shape[0]

    def copy(idx_ref, k, row, lane, token):
        return pltpu.make_async_copy(ysp_ref.at[_tile_rows(idx_ref[0, row, lane])],
                                     buf.at[k, _tile_rows(token)], sem)

    def wait():
        for k in range(MOE_TOP_E):
            pltpu.make_async_copy(ysp_ref.at[pl.ds(0, tp * SUBLANES)], buf.at[k], sem).wait()

    @pl.when(j == 0)
    def _():
        _for_each_token_looped(tp, lambda k, row, lane, token: copy(pos0_ref, k, row, lane, token).start(priority=k))

    wait()
    wt = wtp_ref[...]
    gate2 = modp_ref[0, 5:6]
    for s in range(SUBLANES):
        rows = pl.ds(s, tp, stride=SUBLANES)
        cols = slice(s * LANES, (s + 1) * LANES)
        y = wt[:, 0:1] * buf[0, rows, :] + wt[:, 1:2] * buf[1, rows, :]
        x_s[:, cols] = x1p_ref[0, :, cols] + gate2[:, cols] * y

    def issue_ahead():
        _for_each_token(tp, lambda k, row, lane, token: copy(pos1_ref, k, row, lane, token).start(priority=k))

    return issue_ahead, wait


def _pool_kernel(*refs, n_groups, n_experts, fused, n_tiles):
    moe_refs, refs = (refs[:N_MOE_REFS], refs[N_MOE_REFS:]) if fused else (None, refs)
    (x_ref, mod_ref, gain_ref, pw_ref, pb_ref, ps_ref, wr_hi_ref, wr_lo_ref, br_ref,
     x1_ref, h2_ref, ids_ref, wts_ref, hist_ref, hbuf_ref, sa_ref, sb_ref) = refs[:17]
    b, s = pl.program_id(0), pl.program_id(1)
    if fused:
        tile = b * pl.num_programs(1) + s
        issue_ahead, drain = _moe_combine(tile, moe_refs, x_ref, refs[17:])
        x = refs[19][...]
    else:
        x = x_ref[0]
    ts, d = x.shape
    mod = mod_ref[0]
    h = _norm_mod(x, gain_ref[0:1], mod[1:2], mod[0:1])

    first = SUBLANES + POOL_HALO
    n = first + ts

    @pl.when(s == 0)
    def _():
        hbuf_ref[0:first] = jnp.zeros((first, d), F32)
        sa_ref[0:SUBLANES] = jnp.zeros((SUBLANES, d), F32)
        sb_ref[0:SUBLANES] = jnp.zeros((SUBLANES, d), F32)

    if fused:
        issue_ahead()
    hbuf_ref[first:n] = h
    pos = (s * ts + 1 + lax.broadcasted_iota(jnp.int32, (ts, 1), 0)).astype(F32)
    cg = d // len(POOL_WINDOWS)
    src, dst = hbuf_ref, sa_ref
    ys = []
    for gi, win in enumerate(POOL_WINDOWS):
        half = win // 2
        assert win == 2 ** (gi + 1) and win <= POOL_HALO
        lo = gi * cg
        dst[SUBLANES:n, lo:] = src[SUBLANES:n, lo:] + src[SUBLANES - half:n - half, lo:]
        cols = slice(lo, lo + cg)
        dgi = dst[first:n, cols] / jnp.minimum(pos, float(win)) - h[:, cols]
        ys.append(_dot(dgi.astype(BF16), pw_ref[gi]))
        src, dst = dst, (sb_ref if dst is sa_ref else sa_ref)
    hbuf_ref[SUBLANES:first] = hbuf_ref[n - POOL_HALO:n]
    x1 = x + (jnp.concatenate(ys, axis=-1) + pb_ref[...]) * (ps_ref[...] * mod[2:3])
    x1_ref[0] = x1
    _router_epilogue(x1, mod, gain_ref[1:2], wr_hi_ref, wr_lo_ref, br_ref, n_groups, n_experts,
                     h2_ref, ids_ref, wts_ref, hist_ref, jnp.logical_and(b == 0, s == 0))
    if fused:
        pl.when(tile == n_tiles - 1)(drain)


def _pool_layer(x, moe, mod, gain, pw, pb, ps, router, layer, n_groups, n_experts):
    b, s, d = x.shape
    ts = min(SEQ_TILE_POOL, s)
    nst = s // ts
    g, cg = pw.shape[0], pw.shape[1]
    r_in, r_out = _router_specs(d, ts, nst, n_experts, layer)
    m_in, m_scratch = _moe_specs(moe, ts, nst, d) if moe else ([], [])
    return pl.pallas_call(
        functools.partial(_pool_kernel, n_groups=n_groups, n_experts=n_experts, fused=bool(moe), n_tiles=b * nst),
        grid=(b, nst),
        in_specs=m_in + [
            pl.BlockSpec((1, ts, d), lambda bb, ss: (bb, ss, 0)),
            pl.BlockSpec((1, 6, d), lambda bb, ss: (bb, 0, 0)),
            pl.BlockSpec((2, d), lambda bb, ss: (0, 0)),
            pl.BlockSpec((g, cg, cg), lambda bb, ss: (0, 0, 0)),
            pl.BlockSpec((1, d), lambda bb, ss: (0, 0)),
            pl.BlockSpec((1, d), lambda bb, ss: (0, 0)),
        ] + r_in,
        out_specs=r_out,
        out_shape=_router_out_shapes(b, s, d, ts, n_experts),
        scratch_shapes=[pltpu.VMEM((SUBLANES + POOL_HALO + ts, d), F32)] * 3 + m_scratch,
        compiler_params=pltpu.CompilerParams(
            dimension_semantics=("arbitrary", "arbitrary"), vmem_limit_bytes=VMEM_LIMIT),
    )(*_moe_args(moe), x, mod, gain, pw.astype(BF16), pb.reshape(1, d), ps.reshape(1, d), *router)


N_GLA_REFS = 28


def _gla_kernel(*refs, n_heads, n_groups, n_experts, fused, n_tiles):
    moe_refs, refs = (refs[:N_MOE_REFS], refs[N_MOE_REFS:]) if fused else (None, refs)
    (x_ref, mod_ref, gain_ref, wp_ref, wz_ref, wg_ref, bg_ref, ng_ref, wo_ref, wr_hi_ref, wr_lo_ref, br_ref,
     x1_ref, h2_ref, ids_ref, wts_ref, hist_ref, state_ref, q_s, k_s, g_s, v_s, r_s, o_s,
     qe_s, kd_s, el_s, oi_s) = refs[:N_GLA_REFS]
    b, s = pl.program_id(0), pl.program_id(1)
    if fused:
        tile = b * pl.num_programs(1) + s
        issue_ahead, drain = _moe_combine(tile, moe_refs, x_ref, refs[N_GLA_REFS:])
        x = refs[N_GLA_REFS + 2][...]
    else:
        x = x_ref[0]
    ts, d = x.shape
    dk_all = q_s.shape[1]
    dk = dk_all // n_heads
    dv = d // n_heads
    mod = mod_ref[0]
    h = _norm_mod(x, gain_ref[0:1], mod[1:2], mod[0:1]).astype(BF16)

    @pl.when(s == 0)
    def _():
        state_ref[...] = jnp.zeros(state_ref.shape, F32)

    if fused:
        issue_ahead()
    q_s[...] = _dot(h, wp_ref[:, 0:dk_all]) * (dk ** -0.5)
    k_s[...] = _dot(h, wp_ref[:, dk_all:2 * dk_all])
    v_s[...] = _dot(h, wp_ref[:, 2 * dk_all:2 * dk_all + d]).astype(BF16)
    r = _dot(h, wp_ref[:, 2 * dk_all + d:2 * dk_all + 2 * d])
    r_s[...] = r * (1.0 / (1.0 + jnp.exp(-r))) * jnp.concatenate([ng_ref[...]] * n_heads, axis=-1)
    z = _dot(h, wz_ref[...]).astype(BF16)
    u = _dot(z, wg_ref[...]) + bg_ref[...]
    g_s[...] = (jnp.minimum(u, 0.0) - jnp.log(1.0 + jnp.exp(-jnp.abs(u)))) * (1.0 / GLA_GATE_TEMP)

    c = GLA_CHUNK
    ri = lax.broadcasted_iota(jnp.int32, (c, c), 0)
    ci = lax.broadcasted_iota(jnp.int32, (c, c), 1)
    causal = ri >= ci
    tril = jnp.where(causal, 1.0, 0.0).astype(BF16)

    heads = [(slice(hd * dk, (hd + 1) * dk), slice(hd * dv, (hd + 1) * dv)) for hd in range(n_heads)]
    group = min(GLA_CHUNK_GROUP, ts // c)

    def intra(m, carry):
        ns = [m * group + i for i in range(group)]
        rows = [pl.ds(pl.multiple_of(n * c, c), c) for n in ns]
        gs = [_split_bf16(g_s[r, :]) for r in rows]
        bcs = [_dot(tril, hi) + _dot(tril, lo) for hi, lo in gs]
        qes, kes = [], []
        for n, r, bc in zip(ns, rows, bcs):
            bl = bc[c - 1:c, :]
            kk = k_s[r, :]
            qes.append((q_s[r, :] * jnp.exp(bc)).astype(BF16))
            kes.append((kk * jnp.exp(-bc)).astype(BF16))
            qe_s[r, :] = qes[-1]
            kd_s[r, :] = (kk * jnp.exp(bl - bc)).astype(BF16)
            el_s[pl.ds(n, 1), :] = jnp.exp(bl)
        scs = [[_dot_nt(q_e[:, ks], k_e[:, ks]) for ks, _ in heads] for q_e, k_e in zip(qes, kes)]
        scs = [[jnp.where(causal, sc, 0.0).astype(BF16) for sc in per_chunk] for per_chunk in scs]
        for r, per_chunk in zip(rows, scs):
            vv = v_s[r, :]
            for sc, (_, vs) in zip(per_chunk, heads):
                oi_s[r, vs] = _dot(sc, vv[:, vs])
        return carry

    lax.fori_loop(0, ts // c // group, intra, 0)

    def inter(m, carry):
        ns = [m * group + i for i in range(group)]
        rows = [pl.ds(pl.multiple_of(n * c, c), c) for n in ns]
        incs = []
        for r in rows:
            vv, k_d = v_s[r, :], kd_s[r, :]
            incs.append([_dot_tn(vv[:, vs], k_d[:, ks]) for ks, vs in heads])
        for n, r, inc in zip(ns, rows, incs):
            q_e = qe_s[r, :]
            e_l = el_s[pl.ds(n, 1), :]
            sts = [state_ref[hd] for hd in range(n_heads)]
            carried = [_dot_nt(q_e[:, ks], st.astype(BF16)) for (ks, _), st in zip(heads, sts)]
            for hd, (ks, _) in enumerate(heads):
                state_ref[hd] = sts[hd] * e_l[:, ks] + inc[hd]
            for o_c, (_, vs) in zip(carried, heads):
                o = oi_s[r, vs] + o_c
                o = o * lax.rsqrt(jnp.mean(o * o, axis=-1, keepdims=True) + NORM_EPS)
                o_s[r, vs] = (o * r_s[r, vs]).astype(BF16)
        return carry

    lax.fori_loop(0, ts // c // group, inter, 0)
    x1 = x + mod[2:3] * _dot(o_s[...], wo_ref[...])
    x1_ref[0] = x1
    _router_epilogue(x1, mod, gain_ref[1:2], wr_hi_ref, wr_lo_ref, br_ref, n_groups, n_experts,
                     h2_ref, ids_ref, wts_ref, hist_ref, jnp.logical_and(b == 0, s == 0))
    if fused:
        pl.when(tile == n_tiles - 1)(drain)


def _gla_layer(x, moe, mod, gain, w_in, w_gate, b_gate, norm_g, w_out, router, layer, n_groups, n_experts):
    b, s, d = x.shape
    ts = min(SEQ_TILE_GLA, s)
    nst = s // ts
    rank, dk_all = w_gate.shape
    dv = norm_g.shape[0]
    n_heads = d // dv
    n_proj = 2 * dk_all + 2 * d
    wp = w_in[:, :n_proj].astype(BF16)
    wz = jnp.pad(w_in[:, n_proj:], ((0, 0), (0, LANES - rank))).astype(BF16)
    wg = jnp.pad(w_gate, ((0, LANES - rank), (0, 0))).astype(BF16)
    r_in, r_out = _router_specs(d, ts, nst, n_experts, layer)
    m_in, m_scratch = _moe_specs(moe, ts, nst, d) if moe else ([], [])
    const = lambda bb, ss: (0, 0)
    once = pl.Buffered(1)
    return pl.pallas_call(
        functools.partial(_gla_kernel, n_heads=n_heads, n_groups=n_groups, n_experts=n_experts, fused=bool(moe),
                          n_tiles=b * nst),
        grid=(b, nst),
        in_specs=m_in + [
            pl.BlockSpec((1, ts, d), lambda bb, ss: (bb, ss, 0)),
            pl.BlockSpec((1, 6, d), lambda bb, ss: (bb, 0, 0)),
            pl.BlockSpec((2, d), const),
            pl.BlockSpec((d, n_proj), const, pipeline_mode=once),
            pl.BlockSpec((d, LANES), const),
            pl.BlockSpec((LANES, dk_all), const),
            pl.BlockSpec((1, dk_all), const),
            pl.BlockSpec((1, dv), const),
            pl.BlockSpec((d, d), const, pipeline_mode=once),
        ] + r_in,
        out_specs=r_out,
        out_shape=_router_out_shapes(b, s, d, ts, n_experts),
        scratch_shapes=[
            pltpu.VMEM((n_heads, dv, dk_all // n_heads), F32),
            pltpu.VMEM((ts, dk_all), F32),
            pltpu.VMEM((ts, dk_all), F32),
            pltpu.VMEM((ts, dk_all), F32),
            pltpu.VMEM((ts, d), BF16),
            pltpu.VMEM((ts, d), F32),
            pltpu.VMEM((ts, d), BF16),
            pltpu.VMEM((ts, dk_all), BF16),
            pltpu.VMEM((ts, dk_all), BF16),
            pltpu.VMEM((ts // GLA_CHUNK, dk_all), F32),
            pltpu.VMEM((ts, d), F32),
        ] + m_scratch,
        compiler_params=pltpu.CompilerParams(
            dimension_semantics=("arbitrary", "arbitrary"), vmem_limit_bytes=VMEM_LIMIT),
    )(*_moe_args(moe), x, mod, gain, wp, wz, wg, b_gate.reshape(1, dk_all), norm_g.reshape(1, dv), w_out.astype(BF16),
      *router)


def _pos_kernel(ids_ref, off_ref, pos_ref, run_ref):
    i = pl.program_id(0)
    tiles, idx_rows, _ = pos_ref.shape
    tp = idx_rows * LANES // MOE_TOP_E
    n_e = off_ref.shape[0]

    @pl.when(i == 0)
    def _():
        run_ref[...] = jnp.zeros(run_ref.shape, F32)

    ei = lax.broadcasted_iota(jnp.int32, (n_e, tp), 0)
    tri = jnp.where(lax.broadcasted_iota(jnp.int32, (tp, tp), 0) <= lax.broadcasted_iota(jnp.int32, (tp, tp), 1),
                    1.0, 0.0).astype(BF16)
    base = off_ref[...] + run_ref[...]
    for u in range(tiles):
        ids = ids_ref[:, u * tp:(u + 1) * tp]
        oh0 = ei == ids[0:1]
        oh1 = ei == ids[1:2]
        inc0 = _dot(jnp.where(oh0, 1.0, 0.0).astype(BF16), tri)
        inc1 = _dot(jnp.where(oh1, 1.0, 0.0).astype(BF16), tri)
        tot0 = inc0[:, tp - 1:tp]
        tot1 = inc1[:, tp - 1:tp]
        p0 = jnp.sum(jnp.where(oh0, base + inc0 - 1.0, 0.0), axis=0, keepdims=True)
        p1 = jnp.sum(jnp.where(oh1, base + tot0 + inc1 - 1.0, 0.0), axis=0, keepdims=True)
        p = [pk[:, c * LANES:(c + 1) * LANES] for pk in (p0, p1) for c in range(tp // LANES)]
        pos_ref[u] = jnp.concatenate(p, axis=0).astype(jnp.int32)
        base = base + tot0 + tot1
    run_ref[...] = base - off_ref[...]


def _positions(ids, offsets):
    t = ids.shape[1]
    tp = min(ROW_TILE, t)
    tiles = min(POS_TILES_PER_STEP, t // tp)
    n_e = offsets.shape[0]
    return pl.pallas_call(
        _pos_kernel,
        grid=(t // (tp * tiles),),
        in_specs=[pl.BlockSpec((MOE_TOP_E, tp * tiles), lambda i: (0, i)),
                  pl.BlockSpec((n_e, 1), lambda i: (0, 0))],
        out_specs=pl.BlockSpec((tiles, MOE_TOP_E * tp // LANES, LANES), lambda i: (i, 0, 0)),
        out_shape=jax.ShapeDtypeStruct((t // tp, MOE_TOP_E * tp // LANES, LANES), jnp.int32),
        scratch_shapes=[pltpu.VMEM((n_e, 1), F32)],
        compiler_params=pltpu.CompilerParams(dimension_semantics=("arbitrary",)),
    )(ids, offsets)


def _tile_rows(row):
    return pl.ds(pl.multiple_of(row * SUBLANES, SUBLANES), SUBLANES)


def _for_each_token(tp, fn):
    chunks = tp // LANES
    for c in range(chunks):
        for lane in range(LANES):
            for k in range(MOE_TOP_E):
                fn(k, k * chunks + c, lane, c * LANES + lane)


def _for_each_token_looped(tp, fn):
    chunks = tp // LANES
    for c in range(chunks):
        def body(lane, carry):
            for k in range(MOE_TOP_E):
                fn(k, k * chunks + c, lane, c * LANES + lane)
            return carry
        lax.fori_loop(0, LANES, body, 0)


def _inverse_kernel(pos_ref, inv_ref):
    tp = pos_ref.shape[1] * LANES // MOE_TOP_E
    base = pl.program_id(0) * tp

    def put(k, row, lane, token):
        inv_ref[pos_ref[0, row, lane]] = base + token

    _for_each_token(tp, put)


def _inverse_positions(pos, n_rows):
    nt, idx_rows, _ = pos.shape
    return pl.pallas_call(
        _inverse_kernel,
        grid=(nt,),
        in_specs=[pl.BlockSpec((1, idx_rows, LANES), lambda i: (i, 0, 0), memory_space=pltpu.SMEM)],
        out_specs=pl.BlockSpec(memory_space=pltpu.SMEM),
        out_shape=jax.ShapeDtypeStruct((n_rows,), jnp.int32),
        compiler_params=pltpu.CompilerParams(dimension_semantics=("arbitrary",)),
    )(pos)


def _gmm_kernel(it_ref, iq_ref, lo_ref, hi_ref, qe_ref, n_ref, nq_ref, tok0_ref, tok1_ref, tok2_ref,
                h_hbm, wi_hbm, wo_hbm, ys_ref, wi_s, wo_s, a_s, acc_s, wi_f, wo_f, xb0, xb1, xb2, sem_i, sem_o, sem_x,
                *, layer):
    w = pl.program_id(0)
    f = wo_s.shape[0]
    tm = acc_s.shape[0]
    n = n_ref[0]
    xbufs = (xb0, xb1, xb2)

    def wi_copy(q, slot):
        return pltpu.make_async_copy(wi_hbm.at[layer, qe_ref[q]], wi_f.at[slot], sem_i.at[slot])

    def wo_copy(q, slot):
        return pltpu.make_async_copy(wo_hbm.at[layer, qe_ref[q]], wo_f.at[slot], sem_o.at[slot])

    def row_copy(tok_ref, half, lane, slot):
        return pltpu.make_async_copy(h_hbm.at[_tile_rows(tok_ref[0, half, lane])],
                                     xbufs[slot].at[_tile_rows(half * LANES + lane)], sem_x.at[slot])

    def wait_rows(slot):
        pltpu.make_async_copy(h_hbm.at[pl.ds(0, tm * SUBLANES)], xbufs[slot], sem_x.at[slot]).wait()

    def gather(tok_ref, slot, unrolled):
        for half in range(tm // LANES):
            if unrolled:
                for lane in range(LANES):
                    row_copy(tok_ref, half, lane, slot).start(priority=lane % 2)
            else:
                def body(lane, carry):
                    row_copy(tok_ref, half, lane, slot).start()
                    return carry
                lax.fori_loop(0, LANES, body, 0)

    @pl.when(w <= n)
    def _():
        ja = jnp.minimum(w, n - 1)
        jb = jnp.maximum(w - 1, 0)
        slot_a = lax.rem(w, 2)
        qa = iq_ref[ja]
        qb = iq_ref[jb]

        @pl.when(w == 0)
        def _():
            wi_copy(0, 0).start()
            wo_copy(0, 0).start()
            gather(tok0_ref, 0, False)
            gather(tok1_ref, 1, False)
            a_s[1] = jnp.zeros(a_s.shape[1:], BF16)
            acc_s[...] = jnp.zeros(acc_s.shape, F32)
            wo_s[...] = jnp.zeros(wo_s.shape, BF16)

        @pl.when(jnp.logical_or(w == 1, jnp.logical_and(w >= 2, qb != iq_ref[jnp.maximum(w - 2, 0)])))
        def _():
            slot = lax.rem(qb, 2)
            wo_copy(qb, slot).wait()
            wo_s[...] = wo_f[slot].astype(BF16)

        @pl.when(jnp.logical_or(w == 0, qa != qb))
        def _():
            @pl.when(qa + 1 < nq_ref[0])
            def _():
                wi_copy(qa + 1, lax.rem(qa + 1, 2)).start()
                wo_copy(qa + 1, lax.rem(qa + 1, 2)).start()

            slot = lax.rem(qa, 2)
            wi_copy(qa, slot).wait()
            wi_s[...] = wi_f[slot].astype(BF16)

        def second_stage():
            y = _dot(a_s[1 - slot_a], wo_s[...])
            row = lax.broadcasted_iota(jnp.int32, (tm, 1), 0)
            mine = jnp.logical_and(jnp.logical_and(row >= lo_ref[jb], row < hi_ref[jb]), w >= 1)
            merged = jnp.where(mine, y, acc_s[...])
            acc_s[...] = merged
            _store_token_tiles(ys_ref, merged)

        for slot in range(XS_RING):
            @pl.when(jnp.logical_and(lax.rem(w, XS_RING) == slot, w < n))
            def _():
                wait_rows(slot)
                x = jnp.concatenate(
                    [_load_token_tiles(xbufs[slot], tm, s).astype(BF16) for s in range(SUBLANES)], axis=-1)
                gather(tok2_ref, (slot + 2) % XS_RING, True)
                second_stage()
                gu = _dot(x, wi_s[...])
                gate, up = gu[:, :f], gu[:, f:]
                a_s[slot_a] = (gate * (1.0 / (1.0 + jnp.exp(-gate))) * up).astype(BF16)

            @pl.when(jnp.logical_and(lax.rem(w, XS_RING) == slot, w == n))
            def _():
                second_stage()
                wait_rows(slot)
                wait_rows((slot + 1) % XS_RING)


def _grouped_mlp(plan, h2, inv, w_in, w_out, layer):
    item_tile, item_seq, item_lo, item_hi, seq_expert, n_items, n_seq = plan
    d, f2 = w_in.shape[-2:]
    f = f2 // 2
    tm = GMM_TILE
    rows = tm * SUBLANES
    tok = inv.reshape(-1, tm // LANES, LANES)

    def item(ahead):
        return lambda w, it, iq, lo, hi, qe, n, nq: (it[jnp.minimum(w + ahead, n[0] - 1)], 0, 0)

    def second(w, n):
        return jnp.minimum(jnp.maximum(w - 1, 0), n[0] - 1)

    tok_spec = lambda ahead: pl.BlockSpec((1, tm // LANES, LANES), item(ahead), memory_space=pltpu.SMEM)
    return pl.pallas_call(
        functools.partial(_gmm_kernel, layer=layer),
        grid_spec=pltpu.PrefetchScalarGridSpec(
            num_scalar_prefetch=7,
            grid=(item_tile.shape[0] + 1,),
            in_specs=[tok_spec(0), tok_spec(1), tok_spec(2)] + [pl.BlockSpec(memory_space=pl.ANY)] * 3,
            out_specs=pl.BlockSpec((rows, LANES), lambda w, it, iq, lo, hi, qe, n, nq: (it[second(w, n)], 0)),
            scratch_shapes=[pltpu.VMEM((d, f2), BF16), pltpu.VMEM((f, d), BF16), pltpu.VMEM((2, tm, f), BF16),
                            pltpu.VMEM((tm, d), F32), pltpu.VMEM((2, d, f2), F32), pltpu.VMEM((2, f, d), F32)]
            + [pltpu.VMEM((rows, LANES), F32)] * XS_RING
            + [pltpu.SemaphoreType.DMA((2,)), pltpu.SemaphoreType.DMA((2,)), pltpu.SemaphoreType.DMA((XS_RING,))],
        ),
        out_shape=jax.ShapeDtypeStruct((inv.shape[0] * SUBLANES, LANES), F32),
        compiler_params=pltpu.CompilerParams(
            dimension_semantics=("arbitrary",), vmem_limit_bytes=VMEM_LIMIT),
    )(item_tile, item_seq, item_lo, item_hi, seq_expert, n_items, n_seq, tok, tok, tok, h2, w_in, w_out)


def _combine_kernel(pos0_ref, pos1_ref, pos2_ref, x1_ref, wt_ref, mod_ref, fg_ref, ys_ref, o_ref, *scratch, final):
    bufs, sems = scratch[:COMBINE_RING], scratch[COMBINE_RING]
    i = pl.program_id(0)
    n = pl.num_programs(0)
    tp = x1_ref.shape[0]

    def issue(idx_ref, slot):
        def one(k, row, lane, token):
            pltpu.make_async_copy(ys_ref.at[_tile_rows(idx_ref[0, row, lane])], bufs[slot].at[k, _tile_rows(token)],
                                  sems.at[slot]).start(priority=k)
        _for_each_token(tp, one)

    def compute(slot):
        wt = wt_ref[...]
        gate2 = mod_ref[0, 5:6]
        chunks = []
        for s in range(SUBLANES):
            rows = pl.ds(s, tp, stride=SUBLANES)
            cols = slice(s * LANES, (s + 1) * LANES)
            y = wt[:, 0:1] * bufs[slot][0, rows, :] + wt[:, 1:2] * bufs[slot][1, rows, :]
            chunks.append(x1_ref[:, cols] + gate2[:, cols] * y)
        x = jnp.concatenate(chunks, axis=-1)
        if final:
            x = x * lax.rsqrt(jnp.mean(x * x, axis=-1, keepdims=True) + NORM_EPS) * fg_ref[...]
        o_ref[...] = x

    @pl.when(i == 0)
    def _():
        issue(pos0_ref, 0)

        @pl.when(n > 1)
        def _():
            issue(pos1_ref, 1)

    for slot in range(COMBINE_RING):
        @pl.when(lax.rem(i, COMBINE_RING) == slot)
        def _():
            for k in range(MOE_TOP_E):
                pltpu.make_async_copy(ys_ref.at[pl.ds(0, tp * SUBLANES)], bufs[slot].at[k], sems.at[slot]).wait()

            @pl.when(i + 2 < n)
            def _():
                issue(pos2_ref, (slot + 2) % COMBINE_RING)
                compute(slot)

            @pl.when(i + 2 >= n)
            def _():
                compute(slot)


def _combine(pos, x1, wts_t, mod, final_g, ys, seq_len, final):
    nt, idx_rows, _ = pos.shape
    t, d = x1.shape
    tp = t // nt
    per_seq = seq_len // tp
    idx_spec = lambda ahead: pl.BlockSpec((1, idx_rows, LANES), lambda i: (jnp.minimum(i + ahead, nt - 1), 0, 0),
                                          memory_space=pltpu.SMEM)
    return pl.pallas_call(
        functools.partial(_combine_kernel, final=final),
        grid=(nt,),
        in_specs=[
            idx_spec(0), idx_spec(1), idx_spec(2),
            pl.BlockSpec((tp, d), lambda i: (i, 0)),
            pl.BlockSpec((tp, MOE_TOP_E), lambda i: (i, 0)),
            pl.BlockSpec((1, 6, d), lambda i: (i // per_seq, 0, 0)),
            pl.BlockSpec((1, d), lambda i: (0, 0)),
            pl.BlockSpec(memory_space=pl.ANY),
        ],
        out_specs=pl.BlockSpec((tp, d), lambda i: (i, 0)),
        out_shape=jax.ShapeDtypeStruct((t, d), F32),
        scratch_shapes=[pltpu.VMEM((MOE_TOP_E, tp * SUBLANES, LANES), F32)] * COMBINE_RING
        + [pltpu.SemaphoreType.DMA((COMBINE_RING,))],
        compiler_params=pltpu.CompilerParams(
            dimension_semantics=("arbitrary",), vmem_limit_bytes=VMEM_LIMIT),
    )(pos, pos, pos, x1, wts_t, mod, final_g, ys)


def _router_weights(w_group, b_group, w_expert, b_expert):
    g, e = w_group.shape[-1], w_expert.shape[-1]
    pad = ROUTER_ROWS - g - e
    w = jnp.swapaxes(jnp.concatenate([w_group, w_expert], axis=2), 1, 2)
    hi, lo = _split_bf16(jnp.pad(w, ((0, 0), (0, pad), (0, 0))))
    bias = jnp.pad(jnp.concatenate([b_group, b_expert], axis=1), ((0, 0), (0, pad)))[:, :, None]
    return hi, lo, bias


def _tile_plan(hist, n_rows):
    n_e = hist.shape[0]
    tm = GMM_TILE
    counts = jnp.sum(hist, axis=1).astype(jnp.int32)
    ends = jnp.cumsum(counts)
    starts = ends - counts
    first_tile = starts // tm
    items_per = jnp.where(counts > 0, (ends - 1) // tm - first_tile + 1, 0)
    item_ends = jnp.cumsum(items_per)
    n_items = item_ends[-1:]
    max_items = n_rows // tm + n_e - 1
    w = jnp.minimum(jnp.arange(max_items, dtype=jnp.int32), n_items[0] - 1)
    item_expert = jnp.sum(w[:, None] >= item_ends[None, :], axis=1).astype(jnp.int32)
    own = item_expert[:, None] == jnp.arange(n_e, dtype=jnp.int32)[None, :]
    pick = lambda per_expert: jnp.sum(jnp.where(own, per_expert[None, :], 0), axis=1)
    item_tile = pick(first_tile) + w - pick(item_ends - items_per)
    item_lo = jnp.maximum(pick(starts) - item_tile * tm, 0)
    item_hi = jnp.minimum(pick(ends) - item_tile * tm, tm)
    offsets = starts.astype(F32).reshape(n_e, 1)
    used_ends = jnp.cumsum((counts > 0).astype(jnp.int32))
    n_seq = used_ends[-1:]
    q = jnp.minimum(jnp.arange(n_e, dtype=jnp.int32), n_seq[0] - 1)
    seq_expert = jnp.sum(q[:, None] >= used_ends[None, :], axis=1).astype(jnp.int32)
    item_seq = pick(used_ends - 1)
    return offsets, (item_tile.astype(jnp.int32), item_seq.astype(jnp.int32), item_lo.astype(jnp.int32),
                     item_hi.astype(jnp.int32), seq_expert, n_items.astype(jnp.int32), n_seq.astype(jnp.int32))


def kernel(x, c, norm_gain, w_mod, b_mod, pool_w, pool_b, pool_scale, gla_w_in, gla_w_gate, gla_b_gate, gla_norm_g, gla_w_out, moe_w_group, moe_b_group, moe_w_expert, moe_b_expert, moe_w_in, moe_w_out, final_norm_g):
    b, s, d = x.shape
    depth = w_mod.shape[0]
    t = b * s
    n_groups = moe_w_group.shape[-1]
    n_experts = moe_w_expert.shape[-1]
    n_rows = MOE_TOP_E * t
    mod_all = _modulation(c, w_mod, b_mod).reshape(depth, b, 6, d)
    fg = final_norm_g.reshape(1, d)
    router = _router_weights(moe_w_group, moe_b_group, moe_w_expert, moe_b_expert)
    moe = None
    for i in range(depth):
        mod = mod_all[i]
        j = i // 2
        if i % 2 == 0:
            x1, h2, ids, wts, hist = _pool_layer(x, moe, mod, norm_gain[i], pool_w[j], pool_b[j], pool_scale[j],
                                                 router, i, n_groups, n_experts)
        else:
            x1, h2, ids, wts, hist = _gla_layer(x, moe, mod, norm_gain[i], gla_w_in[j], gla_w_gate[j], gla_b_gate[j],
                                                gla_norm_g[j], gla_w_out[j], router, i, n_groups, n_experts)
        offsets, plan = _tile_plan(hist, n_rows)
        pos = _positions(ids, offsets)
        inv = _inverse_positions(pos, n_rows)
        ys = _grouped_mlp(plan, h2, inv, moe_w_in, moe_w_out, i)
        x, moe = x1, (pos, wts.T, mod, ys)
    pos, wts_t, mod, ys = moe
    x = _combine(pos, x.reshape(t, d), wts_t, mod, fg, ys, s, final=True).reshape(b, s, d)
    return x
```

```python
import functools

import jax
import jax.numpy as jnp
from jax import lax
from jax.experimental import pallas as pl
from jax.experimental.pallas import tpu as pltpu

F32 = jnp.float32
BF16 = jnp.bfloat16

NORM_EPS = 1e-6
POOL_WINDOWS = (2, 4, 8, 16)
POOL_HALO = 16
GLA_CHUNK = 64
GLA_GATE_TEMP = 16.0
GLA_CHUNK_GROUP = 8
MOE_TOP_E = 2
LANES = 128
SUBLANES = 8
ROUTER_ROWS = 128
VMEM_LIMIT = 56 * 1024 * 1024

SEQ_TILE_POOL = 512
SEQ_TILE_GLA = 512
GMM_TILE = 256
XS_RING = 3
COMBINE_RING = 3
ROW_TILE = 512
POS_TILES_PER_STEP = 4


def _dot(a, b):
    return jnp.dot(a, b, preferred_element_type=F32)


def _dot_nt(a, b):
    return lax.dot_general(a, b, (((1,), (1,)), ((), ())), preferred_element_type=F32)


def _dot_tn(a, b):
    return lax.dot_general(a, b, (((0,), (0,)), ((), ())), preferred_element_type=F32)


def _split_bf16(x):
    hi = x.astype(BF16)
    lo = (x - hi.astype(F32)).astype(BF16)
    return hi, lo


def _store_token_tiles(ref, x):
    n = x.shape[0]
    for s in range(SUBLANES):
        ref[pl.ds(s, n, stride=SUBLANES), :] = x[:, s * LANES:(s + 1) * LANES]


def _load_token_tiles(ref, n, s):
    return ref[pl.ds(s, n, stride=SUBLANES), :]


def _norm_mod(x, gain, scale, shift):
    ms = jnp.mean(x * x, axis=-1, keepdims=True)
    return x * lax.rsqrt(ms + NORM_EPS) * (gain * (1.0 + scale)) + shift


def _mod_kernel(c_ref, w_ref, b_ref, o_ref):
    c = c_ref[...]
    sc = (c * (1.0 / (1.0 + jnp.exp(-c)))).astype(BF16)
    o_ref[0] = _dot(sc, w_ref[0].astype(BF16)) + b_ref[0]


def _modulation(c, w_mod, b_mod):
    depth, d, n = w_mod.shape
    b = c.shape[0]
    tn = n // 4
    return pl.pallas_call(
        _mod_kernel,
        grid=(depth, n // tn),
        in_specs=[
            pl.BlockSpec((b, d), lambda i, j: (0, 0)),
            pl.BlockSpec((1, d, tn), lambda i, j: (i, 0, j)),
            pl.BlockSpec((1, 1, tn), lambda i, j: (i, 0, j)),
        ],
        out_specs=pl.BlockSpec((1, b, tn), lambda i, j: (i, 0, j)),
        out_shape=jax.ShapeDtypeStruct((depth, b, n), F32),
        compiler_params=pltpu.CompilerParams(vmem_limit_bytes=VMEM_LIMIT),
    )(c, w_mod, b_mod.reshape(depth, 1, n))


def _router_epilogue(x1, mod, gain2, wr_hi_ref, wr_lo_ref, br_ref, n_groups, n_experts,
                     h2_ref, ids_ref, wts_ref, hist_ref, first_step):
    h2 = _norm_mod(x1, gain2, mod[4:5], mod[3:4])
    _store_token_tiles(h2_ref, h2)
    hh, hl = _split_bf16(h2)
    wh = wr_hi_ref[0]
    both = _dot_nt(jnp.concatenate([wh, wr_lo_ref[0]], axis=0), hh)
    lt = both[:ROUTER_ROWS] + both[ROUTER_ROWS:] + _dot_nt(wh, hl) + br_ref[0]
    ts = lt.shape[1]
    per = n_experts // n_groups
    lg = lt[0:n_groups]
    mg = jnp.max(lg, axis=0, keepdims=True)
    p_g = 1.0 / jnp.sum(jnp.exp(lg - mg), axis=0, keepdims=True)
    gi = lax.broadcasted_iota(jnp.int32, lg.shape, 0)
    g_idx = jnp.min(jnp.where(lg == mg, gi, n_groups), axis=0, keepdims=True)
    sel = jnp.zeros((per, ts), F32)
    for g in range(n_groups):
        sel = jnp.where(g_idx == g, lt[n_groups + g * per:n_groups + (g + 1) * per], sel)
    ei = lax.broadcasted_iota(jnp.int32, sel.shape, 0)
    m1 = jnp.max(sel, axis=0, keepdims=True)
    i1 = jnp.min(jnp.where(sel == m1, ei, per), axis=0, keepdims=True)
    rest = jnp.where(ei == i1, -jnp.inf, sel)
    m2 = jnp.max(rest, axis=0, keepdims=True)
    i2 = jnp.min(jnp.where(rest == m2, ei, per), axis=0, keepdims=True)
    e21 = jnp.exp(m2 - m1)
    w1 = p_g / (1.0 + e21)
    w2 = p_g * e21 / (1.0 + e21)
    id1 = g_idx * per + i1
    id2 = g_idx * per + i2
    ids_ref[...] = jnp.concatenate([id1, id2], axis=0)
    wts_ref[...] = jnp.concatenate([w1, w2], axis=0)
    xi = lax.broadcasted_iota(jnp.int32, (n_experts, ts), 0)
    cnt = (xi == id1).astype(F32) + (xi == id2).astype(F32)

    @pl.when(first_step)
    def _():
        hist_ref[...] = cnt

    @pl.when(jnp.logical_not(first_step))
    def _():
        hist_ref[...] += cnt


def _router_specs(d, ts, n_seq_tiles, n_experts, layer):
    in_specs = [
        pl.BlockSpec((1, ROUTER_ROWS, d), lambda b, s: (layer, 0, 0)),
        pl.BlockSpec((1, ROUTER_ROWS, d), lambda b, s: (layer, 0, 0)),
        pl.BlockSpec((1, ROUTER_ROWS, 1), lambda b, s: (layer, 0, 0)),
    ]
    out_specs = [
        pl.BlockSpec((1, ts, d), lambda b, s: (b, s, 0)),
        pl.BlockSpec((ts * SUBLANES, LANES), lambda b, s: (b * n_seq_tiles + s, 0)),
        pl.BlockSpec((MOE_TOP_E, ts), lambda b, s: (0, b * n_seq_tiles + s)),
        pl.BlockSpec((MOE_TOP_E, ts), lambda b, s: (0, b * n_seq_tiles + s)),
        pl.BlockSpec((n_experts, ts), lambda b, s: (0, 0)),
    ]
    return in_specs, out_specs


def _router_out_shapes(b, s, d, ts, n_experts):
    assert d == SUBLANES * LANES, "token-per-tile row layout needs d_model == one (8,128) tile"
    return [
        jax.ShapeDtypeStruct((b, s, d), F32),
        jax.ShapeDtypeStruct((b * s * SUBLANES, LANES), F32),
        jax.ShapeDtypeStruct((MOE_TOP_E, b * s), jnp.int32),
        jax.ShapeDtypeStruct((MOE_TOP_E, b * s), F32),
        jax.ShapeDtypeStruct((n_experts, ts), F32),
    ]


def _moe_specs(moe, ts, n_seq_tiles, d):
    pos = moe[0]
    nt, idx_rows, _ = pos.shape
    assert idx_rows * LANES // MOE_TOP_E == ts, "row tile of the index blocks must equal the mixer's sequence tile"
    tile = lambda bb, ss: bb * n_seq_tiles + ss
    idx_spec = lambda ahead: pl.BlockSpec(
        (1, idx_rows, LANES), lambda bb, ss: (jnp.minimum(tile(bb, ss) + ahead, nt - 1), 0, 0),
        memory_space=pltpu.SMEM)
    in_specs = [idx_spec(0), idx_spec(1),
                pl.BlockSpec((ts, MOE_TOP_E), lambda bb, ss: (tile(bb, ss), 0)),
                pl.BlockSpec((1, 6, d), lambda bb, ss: (bb, 0, 0)),
                pl.BlockSpec(memory_space=pl.ANY)]
    scratch = [pltpu.VMEM((MOE_TOP_E, ts * SUBLANES, LANES), F32),
               pltpu.SemaphoreType.DMA(()),
               pltpu.VMEM((ts, d), F32)]
    return in_specs, scratch


N_MOE_REFS = 5


def _moe_args(moe):
    return (moe[0],) * 2 + tuple(moe[1:]) if moe else ()


def _moe_combine(j, moe_refs, x1p_ref, scratch):
    pos0_ref, pos1_ref, wtp_ref, modp_ref, ysp_ref = moe_refs
    buf, sem, x_s = scratch
    tp = x_s.shape[0]

    def copy(idx_ref, k, row, lane, token):
        return pltpu.make_async_copy(ysp_ref.at[_tile_rows(idx_ref[0, row, lane])],
                                     buf.at[k, _tile_rows(token)], sem)

    def wait():
        for k in range(MOE_TOP_E):
            pltpu.make_async_copy(ysp_ref.at[pl.ds(0, tp * SUBLANES)], buf.at[k], sem).wait()

    @pl.when(j == 0)
    def _():
        _for_each_token_looped(tp, lambda k, row, lane, token: copy(pos0_ref, k, row, lane, token).start(priority=k))

    wait()
    wt = wtp_ref[...]
    gate2 = modp_ref[0, 5:6]
    for s in range(SUBLANES):
        rows = pl.ds(s, tp, stride=SUBLANES)
        cols = slice(s * LANES, (s + 1) * LANES)
        y = wt[:, 0:1] * buf[0, rows, :] + wt[:, 1:2] * buf[1, rows, :]
        x_s[:, cols] = x1p_ref[0, :, cols] + gate2[:, cols] * y

    def issue_ahead():
        _for_each_token(tp, lambda k, row, lane, token: copy(pos1_ref, k, row, lane, token).start(priority=k))

    return issue_ahead, wait


def _pool_kernel(*refs, n_groups, n_experts, fused, n_tiles):
    moe_refs, refs = (refs[:N_MOE_REFS], refs[N_MOE_REFS:]) if fused else (None, refs)
    (x_ref, mod_ref, gain_ref, pw_ref, pb_ref, ps_ref, wr_hi_ref, wr_lo_ref, br_ref,
     x1_ref, h2_ref, ids_ref, wts_ref, hist_ref, hbuf_ref, sa_ref, sb_ref) = refs[:17]
    b, s = pl.program_id(0), pl.program_id(1)
    if fused:
        tile = b * pl.num_programs(1) + s
        issue_ahead, drain = _moe_combine(tile, moe_refs, x_ref, refs[17:])
        x = refs[19][...]
    else:
        x = x_ref[0]
    ts, d = x.shape
    mod = mod_ref[0]
    h = _norm_mod(x, gain_ref[0:1], mod[1:2], mod[0:1])

    first = SUBLANES + POOL_HALO
    n = first + ts

    @pl.when(s == 0)
    def _():
        hbuf_ref[0:first] = jnp.zeros((first, d), F32)
        sa_ref[0:SUBLANES] = jnp.zeros((SUBLANES, d), F32)
        sb_ref[0:SUBLANES] = jnp.zeros((SUBLANES, d), F32)

    if fused:
        issue_ahead()
    hbuf_ref[first:n] = h
    pos = (s * ts + 1 + lax.broadcasted_iota(jnp.int32, (ts, 1), 0)).astype(F32)
    cg = d // len(POOL_WINDOWS)
    src, dst = hbuf_ref, sa_ref
    ys = []
    for gi, win in enumerate(POOL_WINDOWS):
        half = win // 2
        assert win == 2 ** (gi + 1) and win <= POOL_HALO
        lo = gi * cg
        dst[SUBLANES:n, lo:] = src[SUBLANES:n, lo:] + src[SUBLANES - half:n - half, lo:]
        cols = slice(lo, lo + cg)
        dgi = dst[first:n, cols] / jnp.minimum(pos, float(win)) - h[:, cols]
        ys.append(_dot(dgi.astype(BF16), pw_ref[gi]))
        src, dst = dst, (sb_ref if dst is sa_ref else sa_ref)
    hbuf_ref[SUBLANES:first] = hbuf_ref[n - POOL_HALO:n]
    x1 = x + (jnp.concatenate(ys, axis=-1) + pb_ref[...]) * (ps_ref[...] * mod[2:3])
    x1_ref[0] = x1
    _router_epilogue(x1, mod, gain_ref[1:2], wr_hi_ref, wr_lo_ref, br_ref, n_groups, n_experts,
                     h2_ref, ids_ref, wts_ref, hist_ref, jnp.logical_and(b == 0, s == 0))
    if fused:
        pl.when(tile == n_tiles - 1)(drain)


def _pool_layer(x, moe, mod, gain, pw, pb, ps, router, layer, n_groups, n_experts):
    b, s, d = x.shape
    ts = min(SEQ_TILE_POOL, s)
    nst = s // ts
    g, cg = pw.shape[0], pw.shape[1]
    r_in, r_out = _router_specs(d, ts, nst, n_experts, layer)
    m_in, m_scratch = _moe_specs(moe, ts, nst, d) if moe else ([], [])
    return pl.pallas_call(
        functools.partial(_pool_kernel, n_groups=n_groups, n_experts=n_experts, fused=bool(moe), n_tiles=b * nst),
        grid=(b, nst),
        in_specs=m_in + [
            pl.BlockSpec((1, ts, d), lambda bb, ss: (bb, ss, 0)),
            pl.BlockSpec((1, 6, d), lambda bb, ss: (bb, 0, 0)),
            pl.BlockSpec((2, d), lambda bb, ss: (0, 0)),
            pl.BlockSpec((g, cg, cg), lambda bb, ss: (0, 0, 0)),
            pl.BlockSpec((1, d), lambda bb, ss: (0, 0)),
            pl.BlockSpec((1, d), lambda bb, ss: (0, 0)),
        ] + r_in,
        out_specs=r_out,
        out_shape=_router_out_shapes(b, s, d, ts, n_experts),
        scratch_shapes=[pltpu.VMEM((SUBLANES + POOL_HALO + ts, d), F32)] * 3 + m_scratch,
        compiler_params=pltpu.CompilerParams(
            dimension_semantics=("arbitrary", "arbitrary"), vmem_limit_bytes=VMEM_LIMIT),
    )(*_moe_args(moe), x, mod, gain, pw.astype(BF16), pb.reshape(1, d), ps.reshape(1, d), *router)


N_GLA_REFS = 28


def _gla_kernel(*refs, n_heads, n_groups, n_experts, fused, n_tiles):
    moe_refs, refs = (refs[:N_MOE_REFS], refs[N_MOE_REFS:]) if fused else (None, refs)
    (x_ref, mod_ref, gain_ref, wp_ref, wz_ref, wg_ref, bg_ref, ng_ref, wo_ref, wr_hi_ref, wr_lo_ref, br_ref,
     x1_ref, h2_ref, ids_ref, wts_ref, hist_ref, state_ref, q_s, k_s, g_s, v_s, r_s, o_s,
     qe_s, kd_s, el_s, oi_s) = refs[:N_GLA_REFS]
    b, s = pl.program_id(0), pl.program_id(1)
    if fused:
        tile = b * pl.num_programs(1) + s
        issue_ahead, drain = _moe_combine(tile, moe_refs, x_ref, refs[N_GLA_REFS:])
        x = refs[N_GLA_REFS + 2][...]
    else:
        x = x_ref[0]
    ts, d = x.shape
    dk_all = q_s.shape[1]
    dk = dk_all // n_heads
    dv = d // n_heads
    mod = mod_ref[0]
    h = _norm_mod(x, gain_ref[0:1], mod[1:2], mod[0:1]).astype(BF16)

    @pl.when(s == 0)
    def _():
        state_ref[...] = jnp.zeros(state_ref.shape, F32)

    if fused:
        issue_ahead()
    q_s[...] = _dot(h, wp_ref[:, 0:dk_all]) * (dk ** -0.5)
    k_s[...] = _dot(h, wp_ref[:, dk_all:2 * dk_all])
    v_s[...] = _dot(h, wp_ref[:, 2 * dk_all:2 * dk_all + d]).astype(BF16)
    r = _dot(h, wp_ref[:, 2 * dk_all + d:2 * dk_all + 2 * d])
    r_s[...] = r * (1.0 / (1.0 + jnp.exp(-r))) * jnp.concatenate([ng_ref[...]] * n_heads, axis=-1)
    z = _dot(h, wz_ref[...]).astype(BF16)
    u = _dot(z, wg_ref[...]) + bg_ref[...]
    g_s[...] = (jnp.minimum(u, 0.0) - jnp.log(1.0 + jnp.exp(-jnp.abs(u)))) * (1.0 / GLA_GATE_TEMP)

    c = GLA_CHUNK
    ri = lax.broadcasted_iota(jnp.int32, (c, c), 0)
    ci = lax.broadcasted_iota(jnp.int32, (c, c), 1)
    causal = ri >= ci
    tril = jnp.where(causal, 1.0, 0.0).astype(BF16)

    heads = [(slice(hd * dk, (hd + 1) * dk), slice(hd * dv, (hd + 1) * dv)) for hd in range(n_heads)]
    group = min(GLA_CHUNK_GROUP, ts // c)

    def intra(m, carry):
        ns = [m * group + i for i in range(group)]
        rows = [pl.ds(pl.multiple_of(n * c, c), c) for n in ns]
        gs = [_split_bf16(g_s[r, :]) for r in rows]
        bcs = [_dot(tril, hi) + _dot(tril, lo) for hi, lo in gs]
        qes, kes = [], []
        for n, r, bc in zip(ns, rows, bcs):
            bl = bc[c - 1:c, :]
            kk = k_s[r, :]
            qes.append((q_s[r, :] * jnp.exp(bc)).astype(BF16))
            kes.append((kk * jnp.exp(-bc)).astype(BF16))
            qe_s[r, :] = qes[-1]
            kd_s[r, :] = (kk * jnp.exp(bl - bc)).astype(BF16)
            el_s[pl.ds(n, 1), :] = jnp.exp(bl)
        scs = [[_dot_nt(q_e[:, ks], k_e[:, ks]) for ks, _ in heads] for q_e, k_e in zip(qes, kes)]
        scs = [[jnp.where(causal, sc, 0.0).astype(BF16) for sc in per_chunk] for per_chunk in scs]
        for r, per_chunk in zip(rows, scs):
            vv = v_s[r, :]
            for sc, (_, vs) in zip(per_chunk, heads):
                oi_s[r, vs] = _dot(sc, vv[:, vs])
        return carry

    lax.fori_loop(0, ts // c // group, intra, 0)

    def inter(m, carry):
        ns = [m * group + i for i in range(group)]
        rows = [pl.ds(pl.multiple_of(n * c, c), c) for n in ns]
        incs = []
        for r in rows:
            vv, k_d = v_s[r, :], kd_s[r, :]
            incs.append([_dot_tn(vv[:, vs], k_d[:, ks]) for ks, vs in heads])
        for n, r, inc in zip(ns, rows, incs):
            q_e = qe_s[r, :]
            e_l = el_s[pl.ds(n, 1), :]
            sts = [state_ref[hd] for hd in range(n_heads)]
            carried = [_dot_nt(q_e[:, ks], st.astype(BF16)) for (ks, _), st in zip(heads, sts)]
            for hd, (ks, _) in enumerate(heads):
                state_ref[hd] = sts[hd] * e_l[:, ks] + inc[hd]
            for o_c, (_, vs) in zip(carried, heads):
                o = oi_s[r, vs] + o_c
                o = o * lax.rsqrt(jnp.mean(o * o, axis=-1, keepdims=True) + NORM_EPS)
                o_s[r, vs] = (o * r_s[r, vs]).astype(BF16)
        return carry

    lax.fori_loop(0, ts // c // group, inter, 0)
    x1 = x + mod[2:3] * _dot(o_s[...], wo_ref[...])
    x1_ref[0] = x1
    _router_epilogue(x1, mod, gain_ref[1:2], wr_hi_ref, wr_lo_ref, br_ref, n_groups, n_experts,
                     h2_ref, ids_ref, wts_ref, hist_ref, jnp.logical_and(b == 0, s == 0))
    if fused:
        pl.when(tile == n_tiles - 1)(drain)


def _gla_layer(x, moe, mod, gain, w_in, w_gate, b_gate, norm_g, w_out, router, layer, n_groups, n_experts):
    b, s, d = x.shape
    ts = min(SEQ_TILE_GLA, s)
    nst = s // ts
    rank, dk_all = w_gate.shape
    dv = norm_g.shape[0]
    n_heads = d // dv
    n_proj = 2 * dk_all + 2 * d
    wp = w_in[:, :n_proj].astype(BF16)
    wz = jnp.pad(w_in[:, n_proj:], ((0, 0), (0, LANES - rank))).astype(BF16)
    wg = jnp.pad(w_gate, ((0, LANES - rank), (0, 0))).astype(BF16)
    r_in, r_out = _router_specs(d, ts, nst, n_experts, layer)
    m_in, m_scratch = _moe_specs(moe, ts, nst, d) if moe else ([], [])
    const = lambda bb, ss: (0, 0)
    once = pl.Buffered(1)
    return pl.pallas_call(
        functools.partial(_gla_kernel, n_heads=n_heads, n_groups=n_groups, n_experts=n_experts, fused=bool(moe),
                          n_tiles=b * nst),
        grid=(b, nst),
        in_specs=m_in + [
            pl.BlockSpec((1, ts, d), lambda bb, ss: (bb, ss, 0)),
            pl.BlockSpec((1, 6, d), lambda bb, ss: (bb, 0, 0)),
            pl.BlockSpec((2, d), const),
            pl.BlockSpec((d, n_proj), const, pipeline_mode=once),
            pl.BlockSpec((d, LANES), const),
            pl.BlockSpec((LANES, dk_all), const),
            pl.BlockSpec((1, dk_all), const),
            pl.BlockSpec((1, dv), const),
            pl.BlockSpec((d, d), const, pipeline_mode=once),
        ] + r_in,
        out_specs=r_out,
        out_shape=_router_out_shapes(b, s, d, ts, n_experts),
        scratch_shapes=[
            pltpu.VMEM((n_heads, dv, dk_all // n_heads), F32),
            pltpu.VMEM((ts, dk_all), F32),
            pltpu.VMEM((ts, dk_all), F32),
            pltpu.VMEM((ts, dk_all), F32),
            pltpu.VMEM((ts, d), BF16),
            pltpu.VMEM((ts, d), F32),
            pltpu.VMEM((ts, d), BF16),
            pltpu.VMEM((ts, dk_all), BF16),
            pltpu.VMEM((ts, dk_all), BF16),
            pltpu.VMEM((ts // GLA_CHUNK, dk_all), F32),
            pltpu.VMEM((ts, d), F32),
        ] + m_scratch,
        compiler_params=pltpu.CompilerParams(
            dimension_semantics=("arbitrary", "arbitrary"), vmem_limit_bytes=VMEM_LIMIT),
    )(*_moe_args(moe), x, mod, gain, wp, wz, wg, b_gate.reshape(1, dk_all), norm_g.reshape(1, dv), w_out.astype(BF16),
      *router)


def _pos_kernel(ids_ref, off_ref, pos_ref, run_ref):
    i = pl.program_id(0)
    tiles, idx_rows, _ = pos_ref.shape
    tp = idx_rows * LANES // MOE_TOP_E
    n_e = off_ref.shape[0]

    @pl.when(i == 0)
    def _():
        run_ref[...] = jnp.zeros(run_ref.shape, F32)

    ei = lax.broadcasted_iota(jnp.int32, (n_e, tp), 0)
    tri = jnp.where(lax.broadcasted_iota(jnp.int32, (tp, tp), 0) <= lax.broadcasted_iota(jnp.int32, (tp, tp), 1),
                    1.0, 0.0).astype(BF16)
    base = off_ref[...] + run_ref[...]
    for u in range(tiles):
        ids = ids_ref[:, u * tp:(u + 1) * tp]
        oh0 = ei == ids[0:1]
        oh1 = ei == ids[1:2]
        inc0 = _dot(jnp.where(oh0, 1.0, 0.0).astype(BF16), tri)
        inc1 = _dot(jnp.where(oh1, 1.0, 0.0).astype(BF16), tri)
        tot0 = inc0[:, tp - 1:tp]
        tot1 = inc1[:, tp - 1:tp]
        p0 = jnp.sum(jnp.where(oh0, base + inc0 - 1.0, 0.0), axis=0, keepdims=True)
        p1 = jnp.sum(jnp.where(oh1, base + tot0 + inc1 - 1.0, 0.0), axis=0, keepdims=True)
        p = [pk[:, c * LANES:(c + 1) * LANES] for pk in (p0, p1) for c in range(tp // LANES)]
        pos_ref[u] = jnp.concatenate(p, axis=0).astype(jnp.int32)
        base = base + tot0 + tot1
    run_ref[...] = base - off_ref[...]


def _positions(ids, offsets):
    t = ids.shape[1]
    tp = min(ROW_TILE, t)
    tiles = min(POS_TILES_PER_STEP, t // tp)
    n_e = offsets.shape[0]
    return pl.pallas_call(
        _pos_kernel,
        grid=(t // (tp * tiles),),
        in_specs=[pl.BlockSpec((MOE_TOP_E, tp * tiles), lambda i: (0, i)),
                  pl.BlockSpec((n_e, 1), lambda i: (0, 0))],
        out_specs=pl.BlockSpec((tiles, MOE_TOP_E * tp // LANES, LANES), lambda i: (i, 0, 0)),
        out_shape=jax.ShapeDtypeStruct((t // tp, MOE_TOP_E * tp // LANES, LANES), jnp.int32),
        scratch_shapes=[pltpu.VMEM((n_e, 1), F32)],
        compiler_params=pltpu.CompilerParams(dimension_semantics=("arbitrary",)),
    )(ids, offsets)


def _tile_rows(row):
    return pl.ds(pl.multiple_of(row * SUBLANES, SUBLANES), SUBLANES)


def _for_each_token(tp, fn):
    chunks = tp // LANES
    for c in range(chunks):
        for lane in range(LANES):
            for k in range(MOE_TOP_E):
                fn(k, k * chunks + c, lane, c * LANES + lane)


def _for_each_token_looped(tp, fn):
    chunks = tp // LANES
    for c in range(chunks):
        def body(lane, carry):
            for k in range(MOE_TOP_E):
                fn(k, k * chunks + c, lane, c * LANES + lane)
            return carry
        lax.fori_loop(0, LANES, body, 0)


def _inverse_kernel(pos_ref, inv_ref):
    tp = pos_ref.shape[1] * LANES // MOE_TOP_E
    base = pl.program_id(0) * tp

    def put(k, row, lane, token):
        inv_ref[pos_ref[0, row, lane]] = base + token

    _for_each_token(tp, put)


def _inverse_positions(pos, n_rows):
    nt, idx_rows, _ = pos.shape
    return pl.pallas_call(
        _inverse_kernel,
        grid=(nt,),
        in_specs=[pl.BlockSpec((1, idx_rows, LANES), lambda i: (i, 0, 0), memory_space=pltpu.SMEM)],
        out_specs=pl.BlockSpec(memory_space=pltpu.SMEM),
        out_shape=jax.ShapeDtypeStruct((n_rows,), jnp.int32),
        compiler_params=pltpu.CompilerParams(dimension_semantics=("arbitrary",)),
    )(pos)


def _gmm_kernel(it_ref, iq_ref, lo_ref, hi_ref, qe_ref, n_ref, nq_ref, tok0_ref, tok1_ref, tok2_ref,
                h_hbm, wi_hbm, wo_hbm, ys_ref, wi_s, wo_s, a_s, acc_s, wi_f, wo_f, xb0, xb1, xb2, sem_i, sem_o, sem_x,
                *, layer):
    w = pl.program_id(0)
    f = wo_s.shape[0]
    tm = acc_s.shape[0]
    n = n_ref[0]
    xbufs = (xb0, xb1, xb2)

    def wi_copy(q, slot):
        return pltpu.make_async_copy(wi_hbm.at[layer, qe_ref[q]], wi_f.at[slot], sem_i.at[slot])

    def wo_copy(q, slot):
        return pltpu.make_async_copy(wo_hbm.at[layer, qe_ref[q]], wo_f.at[slot], sem_o.at[slot])

    def row_copy(tok_ref, half, lane, slot):
        return pltpu.make_async_copy(h_hbm.at[_tile_rows(tok_ref[0, half, lane])],
                                     xbufs[slot].at[_tile_rows(half * LANES + lane)], sem_x.at[slot])

    def wait_rows(slot):
        pltpu.make_async_copy(h_hbm.at[pl.ds(0, tm * SUBLANES)], xbufs[slot], sem_x.at[slot]).wait()

    def gather(tok_ref, slot, unrolled):
        for half in range(tm // LANES):
            if unrolled:
                for lane in range(LANES):
                    row_copy(tok_ref, half, lane, slot).start(priority=lane % 2)
            else:
                def body(lane, carry):
                    row_copy(tok_ref, half, lane, slot).start()
                    return carry
                lax.fori_loop(0, LANES, body, 0)

    @pl.when(w <= n)
    def _():
        ja = jnp.minimum(w, n - 1)
        jb = jnp.maximum(w - 1, 0)
        slot_a = lax.rem(w, 2)
        qa = iq_ref[ja]
        qb = iq_ref[jb]

        @pl.when(w == 0)
        def _():
            wi_copy(0, 0).start()
            wo_copy(0, 0).start()
            gather(tok0_ref, 0, False)
            gather(tok1_ref, 1, False)
            a_s[1] = jnp.zeros(a_s.shape[1:], BF16)
            acc_s[...] = jnp.zeros(acc_s.shape, F32)
            wo_s[...] = jnp.zeros(wo_s.shape, BF16)

        @pl.when(jnp.logical_or(w == 1, jnp.logical_and(w >= 2, qb != iq_ref[jnp.maximum(w - 2, 0)])))
        def _():
            slot = lax.rem(qb, 2)
            wo_copy(qb, slot).wait()
            wo_s[...] = wo_f[slot].astype(BF16)

        @pl.when(jnp.logical_or(w == 0, qa != qb))
        def _():
            @pl.when(qa + 1 < nq_ref[0])
            def _():
                wi_copy(qa + 1, lax.rem(qa + 1, 2)).start()
                wo_copy(qa + 1, lax.rem(qa + 1, 2)).start()

            slot = lax.rem(qa, 2)
            wi_copy(qa, slot).wait()
            wi_s[...] = wi_f[slot].astype(BF16)

        def second_stage():
            y = _dot(a_s[1 - slot_a], wo_s[...])
            row = lax.broadcasted_iota(jnp.int32, (tm, 1), 0)
            mine = jnp.logical_and(jnp.logical_and(row >= lo_ref[jb], row < hi_ref[jb]), w >= 1)
            merged = jnp.where(mine, y, acc_s[...])
            acc_s[...] = merged
            _store_token_tiles(ys_ref, merged)

        for slot in range(XS_RING):
            @pl.when(jnp.logical_and(lax.rem(w, XS_RING) == slot, w < n))
            def _():
                wait_rows(slot)
                x = jnp.concatenate(
                    [_load_token_tiles(xbufs[slot], tm, s).astype(BF16) for s in range(SUBLANES)], axis=-1)
                gather(tok2_ref, (slot + 2) % XS_RING, True)
                second_stage()
                gu = _dot(x, wi_s[...])
                gate, up = gu[:, :f], gu[:, f:]
                a_s[slot_a] = (gate * (1.0 / (1.0 + jnp.exp(-gate))) * up).astype(BF16)

            @pl.when(jnp.logical_and(lax.rem(w, XS_RING) == slot, w == n))
            def _():
                second_stage()
                wait_rows(slot)
                wait_rows((slot + 1) % XS_RING)


def _grouped_mlp(plan, h2, inv, w_in, w_out, layer):
    item_tile, item_seq, item_lo, item_hi, seq_expert, n_items, n_seq = plan
    d, f2 = w_in.shape[-2:]
    f = f2 // 2
    tm = GMM_TILE
    rows = tm * SUBLANES
    tok = inv.reshape(-1, tm // LANES, LANES)

    def item(ahead):
        return lambda w, it, iq, lo, hi, qe, n, nq: (it[jnp.minimum(w + ahead, n[0] - 1)], 0, 0)

    def second(w, n):
        return jnp.minimum(jnp.maximum(w - 1, 0), n[0] - 1)

    tok_spec = lambda ahead: pl.BlockSpec((1, tm // LANES, LANES), item(ahead), memory_space=pltpu.SMEM)
    return pl.pallas_call(
        functools.partial(_gmm_kernel, layer=layer),
        grid_spec=pltpu.PrefetchScalarGridSpec(
            num_scalar_prefetch=7,
            grid=(item_tile.shape[0] + 1,),
            in_specs=[tok_spec(0), tok_spec(1), tok_spec(2)] + [pl.BlockSpec(memory_space=pl.ANY)] * 3,
            out_specs=pl.BlockSpec((rows, LANES), lambda w, it, iq, lo, hi, qe, n, nq: (it[second(w, n)], 0)),
            scratch_shapes=[pltpu.VMEM((d, f2), BF16), pltpu.VMEM((f, d), BF16), pltpu.VMEM((2, tm, f), BF16),
                            pltpu.VMEM((tm, d), F32), pltpu.VMEM((2, d, f2), F32), pltpu.VMEM((2, f, d), F32)]
            + [pltpu.VMEM((rows, LANES), F32)] * XS_RING
            + [pltpu.SemaphoreType.DMA((2,)), pltpu.SemaphoreType.DMA((2,)), pltpu.SemaphoreType.DMA((XS_RING,))],
        ),
        out_shape=jax.ShapeDtypeStruct((inv.shape[0] * SUBLANES, LANES), F32),
        compiler_params=pltpu.CompilerParams(
            dimension_semantics=("arbitrary",), vmem_limit_bytes=VMEM_LIMIT),
    )(item_tile, item_seq, item_lo, item_hi, seq_expert, n_items, n_seq, tok, tok, tok, h2, w_in, w_out)


def _combine_kernel(pos0_ref, pos1_ref, pos2_ref, x1_ref, wt_ref, mod_ref, fg_ref, ys_ref, o_ref, *scratch, final):
    bufs, sems = scratch[:COMBINE_RING], scratch[COMBINE_RING]
    i = pl.program_id(0)
    n = pl.num_programs(0)
    tp = x1_ref.shape[0]

    def issue(idx_ref, slot):
        def one(k, row, lane, token):
            pltpu.make_async_copy(ys_ref.at[_tile_rows(idx_ref[0, row, lane])], bufs[slot].at[k, _tile_rows(token)],
                                  sems.at[slot]).start(priority=k)
        _for_each_token(tp, one)

    def compute(slot):
        wt = wt_ref[...]
        gate2 = mod_ref[0, 5:6]
        chunks = []
        for s in range(SUBLANES):
            rows = pl.ds(s, tp, stride=SUBLANES)
            cols = slice(s * LANES, (s + 1) * LANES)
            y = wt[:, 0:1] * bufs[slot][0, rows, :] + wt[:, 1:2] * bufs[slot][1, rows, :]
            chunks.append(x1_ref[:, cols] + gate2[:, cols] * y)
        x = jnp.concatenate(chunks, axis=-1)
        if final:
            x = x * lax.rsqrt(jnp.mean(x * x, axis=-1, keepdims=True) + NORM_EPS) * fg_ref[...]
        o_ref[...] = x

    @pl.when(i == 0)
    def _():
        issue(pos0_ref, 0)

        @pl.when(n > 1)
        def _():
            issue(pos1_ref, 1)

    for slot in range(COMBINE_RING):
        @pl.when(lax.rem(i, COMBINE_RING) == slot)
        def _():
            for k in range(MOE_TOP_E):
                pltpu.make_async_copy(ys_ref.at[pl.ds(0, tp * SUBLANES)], bufs[slot].at[k], sems.at[slot]).wait()

            @pl.when(i + 2 < n)
            def _():
                issue(pos2_ref, (slot + 2) % COMBINE_RING)
                compute(slot)

            @pl.when(i + 2 >= n)
            def _():
                compute(slot)


def _combine(pos, x1, wts_t, mod, final_g, ys, seq_len, final):
    nt, idx_rows, _ = pos.shape
    t, d = x1.shape
    tp = t // nt
    per_seq = seq_len // tp
    idx_spec = lambda ahead: pl.BlockSpec((1, idx_rows, LANES), lambda i: (jnp.minimum(i + ahead, nt - 1), 0, 0),
                                          memory_space=pltpu.SMEM)
    return pl.pallas_call(
        functools.partial(_combine_kernel, final=final),
        grid=(nt,),
        in_specs=[
            idx_spec(0), idx_spec(1), idx_spec(2),
            pl.BlockSpec((tp, d), lambda i: (i, 0)),
            pl.BlockSpec((tp, MOE_TOP_E), lambda i: (i, 0)),
            pl.BlockSpec((1, 6, d), lambda i: (i // per_seq, 0, 0)),
            pl.BlockSpec((1, d), lambda i: (0, 0)),
            pl.BlockSpec(memory_space=pl.ANY),
        ],
        out_specs=pl.BlockSpec((tp, d), lambda i: (i, 0)),
        out_shape=jax.ShapeDtypeStruct((t, d), F32),
        scratch_shapes=[pltpu.VMEM((MOE_TOP_E, tp * SUBLANES, LANES), F32)] * COMBINE_RING
        + [pltpu.SemaphoreType.DMA((COMBINE_RING,))],
        compiler_params=pltpu.CompilerParams(
            dimension_semantics=("arbitrary",), vmem_limit_bytes=VMEM_LIMIT),
    )(pos, pos, pos, x1, wts_t, mod, final_g, ys)


def _router_weights(w_group, b_group, w_expert, b_expert):
    g, e = w_group.shape[-1], w_expert.shape[-1]
    pad = ROUTER_ROWS - g - e
    w = jnp.swapaxes(jnp.concatenate([w_group, w_expert], axis=2), 1, 2)
    hi, lo = _split_bf16(jnp.pad(w, ((0, 0), (0, pad), (0, 0))))
    bias = jnp.pad(jnp.concatenate([b_group, b_expert], axis=1), ((0, 0), (0, pad)))[:, :, None]
    return hi, lo, bias


def _tile_plan(hist, n_rows):
    n_e = hist.shape[0]
    tm = GMM_TILE
    counts = jnp.sum(hist, axis=1).astype(jnp.int32)
    ends = jnp.cumsum(counts)
    starts = ends - counts
    first_tile = starts // tm
    items_per = jnp.where(counts > 0, (ends - 1) // tm - first_tile + 1, 0)
    item_ends = jnp.cumsum(items_per)
    n_items = item_ends[-1:]
    max_items = n_rows // tm + n_e - 1
    w = jnp.minimum(jnp.arange(max_items, dtype=jnp.int32), n_items[0] - 1)
    item_expert = jnp.sum(w[:, None] >= item_ends[None, :], axis=1).astype(jnp.int32)
    own = item_expert[:, None] == jnp.arange(n_e, dtype=jnp.int32)[None, :]
    pick = lambda per_expert: jnp.sum(jnp.where(own, per_expert[None, :], 0), axis=1)
    item_tile = pick(first_tile) + w - pick(item_ends - items_per)
    item_lo = jnp.maximum(pick(starts) - item_tile * tm, 0)
    item_hi = jnp.minimum(pick(ends) - item_tile * tm, tm)
    offsets = starts.astype(F32).reshape(n_e, 1)
    used_ends = jnp.cumsum((counts > 0).astype(jnp.int32))
    n_seq = used_ends[-1:]
    q = jnp.minimum(jnp.arange(n_e, dtype=jnp.int32), n_seq[0] - 1)
    seq_expert = jnp.sum(q[:, None] >= used_ends[None, :], axis=1).astype(jnp.int32)
    item_seq = pick(used_ends - 1)
    return offsets, (item_tile.astype(jnp.int32), item_seq.astype(jnp.int32), item_lo.astype(jnp.int32),
                     item_hi.astype(jnp.int32), seq_expert, n_items.astype(jnp.int32), n_seq.astype(jnp.int32))


def kernel(x, c, norm_gain, w_mod, b_mod, pool_w, pool_b, pool_scale, gla_w_in, gla_w_gate, gla_b_gate, gla_norm_g, gla_w_out, moe_w_group, moe_b_group, moe_w_expert, moe_b_expert, moe_w_in, moe_w_out, final_norm_g):
    b, s, d = x.shape
    depth = w_mod.shape[0]
    t = b * s
    n_groups = moe_w_group.shape[-1]
    n_experts = moe_w_expert.shape[-1]
    n_rows = MOE_TOP_E * t
    mod_all = _modulation(c, w_mod, b_mod).reshape(depth, b, 6, d)
    fg = final_norm_g.reshape(1, d)
    router = _router_weights(moe_w_group, moe_b_group, moe_w_expert, moe_b_expert)
    moe = None
    for i in range(depth):
        mod = mod_all[i]
        j = i // 2
        if i % 2 == 0:
            x1, h2, ids, wts, hist = _pool_layer(x, moe, mod, norm_gain[i], pool_w[j], pool_b[j], pool_scale[j],
                                                 router, i, n_groups, n_experts)
        else:
            x1, h2, ids, wts, hist = _gla_layer(x, moe, mod, norm_gain[i], gla_w_in[j], gla_w_gate[j], gla_b_gate[j],
                                                gla_norm_g[j], gla_w_out[j], router, i, n_groups, n_experts)
        offsets, plan = _tile_plan(hist, n_rows)
        pos = _positions(ids, offsets)
        inv = _inverse_positions(pos, n_rows)
        ys = _grouped_mlp(plan, h2, inv, moe_w_in, moe_w_out, i)
        x, moe = x1, (pos, wts.T, mod, ys)
    pos, wts_t, mod, ys = moe
    x = _combine(pos, x.reshape(t, d), wts_t, mod, fg, ys, s, final=True).reshape(b, s, d)
    return x
```

```python
import functools

import jax
import jax.numpy as jnp
from jax import lax
from jax.experimental import pallas as pl
from jax.experimental.pallas import tpu as pltpu

F32 = jnp.float32
BF16 = jnp.bfloat16

NORM_EPS = 1e-6
POOL_WINDOWS = (2, 4, 8, 16)
POOL_HALO = 16
GLA_CHUNK = 64
GLA_GATE_TEMP = 16.0
GLA_CHUNK_GROUP = 8
MOE_TOP_E = 2
LANES = 128
SUBLANES = 8
ROUTER_ROWS = 128
VMEM_LIMIT = 56 * 1024 * 1024

MOD_COL_TILES = 4
SEQ_TILE_POOL = 512
SEQ_TILE_GLA = 512
GMM_TILE = 256
XS_RING = 3
COMBINE_RING = 3
ROW_TILE = 512
POS_TILES_PER_STEP = 4


def _dot(a, b):
    return jnp.dot(a, b, preferred_element_type=F32)


def _dot_nt(a, b):
    return lax.dot_general(a, b, (((1,), (1,)), ((), ())), preferred_element_type=F32)


def _dot_tn(a, b):
    return lax.dot_general(a, b, (((0,), (0,)), ((), ())), preferred_element_type=F32)


def _split_bf16(x):
    hi = x.astype(BF16)
    lo = (x - hi.astype(F32)).astype(BF16)
    return hi, lo


def _store_token_tiles(ref, x):
    n = x.shape[0]
    for s in range(SUBLANES):
        ref[pl.ds(s, n, stride=SUBLANES), :] = x[:, s * LANES:(s + 1) * LANES]


def _load_token_tiles(ref, n, s):
    return ref[pl.ds(s, n, stride=SUBLANES), :]


def _norm_mod(x, gain, scale, shift):
    ms = jnp.mean(x * x, axis=-1, keepdims=True)
    return x * lax.rsqrt(ms + NORM_EPS) * (gain * (1.0 + scale)) + shift


def _mod_kernel(c_ref, w_ref, b_ref, o_ref):
    c = c_ref[...]
    sc = (c * (1.0 / (1.0 + jnp.exp(-c)))).astype(BF16)
    o_ref[0] = _dot(sc, w_ref[0].astype(BF16)) + b_ref[0]


def _modulation(c, w_mod, b_mod):
    depth, d, n = w_mod.shape
    b = c.shape[0]
    tn = n // MOD_COL_TILES
    return pl.pallas_call(
        _mod_kernel,
        grid=(depth, n // tn),
        in_specs=[
            pl.BlockSpec((b, d), lambda i, j: (0, 0)),
            pl.BlockSpec((1, d, tn), lambda i, j: (i, 0, j)),
            pl.BlockSpec((1, 1, tn), lambda i, j: (i, 0, j)),
        ],
        out_specs=pl.BlockSpec((1, b, tn), lambda i, j: (i, 0, j)),
        out_shape=jax.ShapeDtypeStruct((depth, b, n), F32),
        compiler_params=pltpu.CompilerParams(vmem_limit_bytes=VMEM_LIMIT),
    )(c, w_mod, b_mod.reshape(depth, 1, n))


def _router_epilogue(x1, mod, gain2, wr_hi_ref, wr_lo_ref, br_ref, n_groups, n_experts,
                     h2_ref, ids_ref, wts_ref, hist_ref, first_step):
    h2 = _norm_mod(x1, gain2, mod[4:5], mod[3:4])
    _store_token_tiles(h2_ref, h2)
    hh, hl = _split_bf16(h2)
    wh = wr_hi_ref[0]
    both = _dot_nt(jnp.concatenate([wh, wr_lo_ref[0]], axis=0), hh)
    lt = both[:ROUTER_ROWS] + both[ROUTER_ROWS:] + _dot_nt(wh, hl) + br_ref[0]
    ts = lt.shape[1]
    per = n_experts // n_groups
    lg = lt[0:n_groups]
    mg = jnp.max(lg, axis=0, keepdims=True)
    p_g = 1.0 / jnp.sum(jnp.exp(lg - mg), axis=0, keepdims=True)
    gi = lax.broadcasted_iota(jnp.int32, lg.shape, 0)
    g_idx = jnp.min(jnp.where(lg == mg, gi, n_groups), axis=0, keepdims=True)
    sel = jnp.zeros((per, ts), F32)
    for g in range(n_groups):
        sel = jnp.where(g_idx == g, lt[n_groups + g * per:n_groups + (g + 1) * per], sel)
    ei = lax.broadcasted_iota(jnp.int32, sel.shape, 0)
    m1 = jnp.max(sel, axis=0, keepdims=True)
    i1 = jnp.min(jnp.where(sel == m1, ei, per), axis=0, keepdims=True)
    rest = jnp.where(ei == i1, -jnp.inf, sel)
    m2 = jnp.max(rest, axis=0, keepdims=True)
    i2 = jnp.min(jnp.where(rest == m2, ei, per), axis=0, keepdims=True)
    e21 = jnp.exp(m2 - m1)
    w1 = p_g / (1.0 + e21)
    w2 = p_g * e21 / (1.0 + e21)
    id1 = g_idx * per + i1
    id2 = g_idx * per + i2
    ids_ref[...] = jnp.concatenate([id1, id2], axis=0)
    wts_ref[...] = jnp.concatenate([w1, w2], axis=0)
    xi = lax.broadcasted_iota(jnp.int32, (n_experts, ts), 0)
    cnt = (xi == id1).astype(F32) + (xi == id2).astype(F32)

    @pl.when(first_step)
    def _():
        hist_ref[...] = cnt

    @pl.when(jnp.logical_not(first_step))
    def _():
        hist_ref[...] += cnt


def _router_specs(d, ts, n_seq_tiles, n_experts, layer):
    in_specs = [
        pl.BlockSpec((1, ROUTER_ROWS, d), lambda b, s: (layer, 0, 0)),
        pl.BlockSpec((1, ROUTER_ROWS, d), lambda b, s: (layer, 0, 0)),
        pl.BlockSpec((1, ROUTER_ROWS, 1), lambda b, s: (layer, 0, 0)),
    ]
    out_specs = [
        pl.BlockSpec((1, ts, d), lambda b, s: (b, s, 0)),
        pl.BlockSpec((ts * SUBLANES, LANES), lambda b, s: (b * n_seq_tiles + s, 0)),
        pl.BlockSpec((MOE_TOP_E, ts), lambda b, s: (0, b * n_seq_tiles + s)),
        pl.BlockSpec((MOE_TOP_E, ts), lambda b, s: (0, b * n_seq_tiles + s)),
        pl.BlockSpec((n_experts, ts), lambda b, s: (0, 0)),
    ]
    return in_specs, out_specs


def _router_out_shapes(b, s, d, ts, n_experts):
    assert d == SUBLANES * LANES, "token-per-tile row layout needs d_model == one (8,128) tile"
    return [
        jax.ShapeDtypeStruct((b, s, d), F32),
        jax.ShapeDtypeStruct((b * s * SUBLANES, LANES), F32),
        jax.ShapeDtypeStruct((MOE_TOP_E, b * s), jnp.int32),
        jax.ShapeDtypeStruct((MOE_TOP_E, b * s), F32),
        jax.ShapeDtypeStruct((n_experts, ts), F32),
    ]


def _moe_specs(moe, ts, n_seq_tiles, d):
    pos = moe[0]
    nt, idx_rows, _ = pos.shape
    assert idx_rows * LANES // MOE_TOP_E == ts, "row tile of the index blocks must equal the mixer's sequence tile"
    tile = lambda bb, ss: bb * n_seq_tiles + ss
    idx_spec = lambda ahead: pl.BlockSpec(
        (1, idx_rows, LANES), lambda bb, ss: (jnp.minimum(tile(bb, ss) + ahead, nt - 1), 0, 0),
        memory_space=pltpu.SMEM)
    in_specs = [idx_spec(0), idx_spec(1),
                pl.BlockSpec((ts, MOE_TOP_E), lambda bb, ss: (tile(bb, ss), 0)),
                pl.BlockSpec((1, 6, d), lambda bb, ss: (bb, 0, 0)),
                pl.BlockSpec(memory_space=pl.ANY)]
    scratch = [pltpu.VMEM((MOE_TOP_E, ts * SUBLANES, LANES), F32),
               pltpu.SemaphoreType.DMA(()),
               pltpu.VMEM((ts, d), F32)]
    return in_specs, scratch


N_MOE_REFS = 5


def _moe_args(moe):
    return (moe[0],) * 2 + tuple(moe[1:]) if moe else ()


def _moe_combine(j, moe_refs, x1p_ref, scratch):
    pos0_ref, pos1_ref, wtp_ref, modp_ref, ysp_ref = moe_refs
    buf, sem, x_s = scratch
    tp = x_s.shape[0]

    def copy(idx_ref, k, row, lane, token):
        return pltpu.make_async_copy(ysp_ref.at[_tile_rows(idx_ref[0, row, lane])],
                                     buf.at[k, _tile_rows(token)], sem)

    def wait():
        for k in range(MOE_TOP_E):
            pltpu.make_async_copy(ysp_ref.at[pl.ds(0, tp * SUBLANES)], buf.at[k], sem).wait()

    @pl.when(j == 0)
    def _():
        _for_each_token_looped(tp, lambda k, row, lane, token: copy(pos0_ref, k, row, lane, token).start(priority=k))

    wait()
    wt = wtp_ref[...]
    gate2 = modp_ref[0, 5:6]
    for s in range(SUBLANES):
        rows = pl.ds(s, tp, stride=SUBLANES)
        cols = slice(s * LANES, (s + 1) * LANES)
        y = wt[:, 0:1] * buf[0, rows, :] + wt[:, 1:2] * buf[1, rows, :]
        x_s[:, cols] = x1p_ref[0, :, cols] + gate2[:, cols] * y

    def issue_ahead():
        _for_each_token(tp, lambda k, row, lane, token: copy(pos1_ref, k, row, lane, token).start(priority=k))

    return issue_ahead, wait


def _pool_kernel(*refs, n_groups, n_experts, fused, n_tiles):
    moe_refs, refs = (refs[:N_MOE_REFS], refs[N_MOE_REFS:]) if fused else (None, refs)
    (x_ref, mod_ref, gain_ref, pw_ref, pb_ref, ps_ref, wr_hi_ref, wr_lo_ref, br_ref,
     x1_ref, h2_ref, ids_ref, wts_ref, hist_ref, hbuf_ref, sa_ref, sb_ref) = refs[:17]
    b, s = pl.program_id(0), pl.program_id(1)
    if fused:
        tile = b * pl.num_programs(1) + s
        issue_ahead, drain = _moe_combine(tile, moe_refs, x_ref, refs[17:])
        x = refs[19][...]
    else:
        x = x_ref[0]
    ts, d = x.shape
    mod = mod_ref[0]
    h = _norm_mod(x, gain_ref[0:1], mod[1:2], mod[0:1])

    first = SUBLANES + POOL_HALO
    n = first + ts

    @pl.when(s == 0)
    def _():
        hbuf_ref[0:first] = jnp.zeros((first, d), F32)
        sa_ref[0:SUBLANES] = jnp.zeros((SUBLANES, d), F32)
        sb_ref[0:SUBLANES] = jnp.zeros((SUBLANES, d), F32)

    if fused:
        issue_ahead()
    hbuf_ref[first:n] = h
    pos = (s * ts + 1 + lax.broadcasted_iota(jnp.int32, (ts, 1), 0)).astype(F32)
    cg = d // len(POOL_WINDOWS)
    src, dst = hbuf_ref, sa_ref
    ys = []
    for gi, win in enumerate(POOL_WINDOWS):
        half = win // 2
        assert win == 2 ** (gi + 1) and win <= POOL_HALO
        lo = gi * cg
        dst[SUBLANES:n, lo:] = src[SUBLANES:n, lo:] + src[SUBLANES - half:n - half, lo:]
        cols = slice(lo, lo + cg)
        dgi = dst[first:n, cols] / jnp.minimum(pos, float(win)) - h[:, cols]
        ys.append(_dot(dgi.astype(BF16), pw_ref[gi]))
        src, dst = dst, (sb_ref if dst is sa_ref else sa_ref)
    hbuf_ref[SUBLANES:first] = hbuf_ref[n - POOL_HALO:n]
    x1 = x + (jnp.concatenate(ys, axis=-1) + pb_ref[...]) * (ps_ref[...] * mod[2:3])
    x1_ref[0] = x1
    _router_epilogue(x1, mod, gain_ref[1:2], wr_hi_ref, wr_lo_ref, br_ref, n_groups, n_experts,
                     h2_ref, ids_ref, wts_ref, hist_ref, jnp.logical_and(b == 0, s == 0))
    if fused:
        pl.when(tile == n_tiles - 1)(drain)


def _pool_layer(x, moe, mod, gain, pw, pb, ps, router, layer, n_groups, n_experts):
    b, s, d = x.shape
    ts = min(SEQ_TILE_POOL, s)
    nst = s // ts
    g, cg = pw.shape[0], pw.shape[1]
    r_in, r_out = _router_specs(d, ts, nst, n_experts, layer)
    m_in, m_scratch = _moe_specs(moe, ts, nst, d) if moe else ([], [])
    return pl.pallas_call(
        functools.partial(_pool_kernel, n_groups=n_groups, n_experts=n_experts, fused=bool(moe), n_tiles=b * nst),
        grid=(b, nst),
        in_specs=m_in + [
            pl.BlockSpec((1, ts, d), lambda bb, ss: (bb, ss, 0)),
            pl.BlockSpec((1, 6, d), lambda bb, ss: (bb, 0, 0)),
            pl.BlockSpec((2, d), lambda bb, ss: (0, 0)),
            pl.BlockSpec((g, cg, cg), lambda bb, ss: (0, 0, 0)),
            pl.BlockSpec((1, d), lambda bb, ss: (0, 0)),
            pl.BlockSpec((1, d), lambda bb, ss: (0, 0)),
        ] + r_in,
        out_specs=r_out,
        out_shape=_router_out_shapes(b, s, d, ts, n_experts),
        scratch_shapes=[pltpu.VMEM((SUBLANES + POOL_HALO + ts, d), F32)] * 3 + m_scratch,
        compiler_params=pltpu.CompilerParams(
            dimension_semantics=("arbitrary", "arbitrary"), vmem_limit_bytes=VMEM_LIMIT),
    )(*_moe_args(moe), x, mod, gain, pw.astype(BF16), pb.reshape(1, d), ps.reshape(1, d), *router)


N_GLA_REFS = 28


def _gla_kernel(*refs, n_heads, n_groups, n_experts, fused, n_tiles):
    moe_refs, refs = (refs[:N_MOE_REFS], refs[N_MOE_REFS:]) if fused else (None, refs)
    (x_ref, mod_ref, gain_ref, wp_ref, wz_ref, wg_ref, bg_ref, ng_ref, wo_ref, wr_hi_ref, wr_lo_ref, br_ref,
     x1_ref, h2_ref, ids_ref, wts_ref, hist_ref, state_ref, q_s, k_s, g_s, v_s, r_s, o_s,
     qe_s, kd_s, el_s, oi_s) = refs[:N_GLA_REFS]
    b, s = pl.program_id(0), pl.program_id(1)
    if fused:
        tile = b * pl.num_programs(1) + s
        issue_ahead, drain = _moe_combine(tile, moe_refs, x_ref, refs[N_GLA_REFS:])
        x = refs[N_GLA_REFS + 2][...]
    else:
        x = x_ref[0]
    ts, d = x.shape
    dk_all = q_s.shape[1]
    dk = dk_all // n_heads
    dv = d // n_heads
    mod = mod_ref[0]
    h = _norm_mod(x, gain_ref[0:1], mod[1:2], mod[0:1]).astype(BF16)

    @pl.when(s == 0)
    def _():
        state_ref[...] = jnp.zeros(state_ref.shape, F32)

    if fused:
        issue_ahead()
    q_s[...] = _dot(h, wp_ref[:, 0:dk_all]) * (dk ** -0.5)
    k_s[...] = _dot(h, wp_ref[:, dk_all:2 * dk_all])
    v_s[...] = _dot(h, wp_ref[:, 2 * dk_all:2 * dk_all + d]).astype(BF16)
    r = _dot(h, wp_ref[:, 2 * dk_all + d:2 * dk_all + 2 * d])
    r_s[...] = r * (1.0 / (1.0 + jnp.exp(-r))) * jnp.concatenate([ng_ref[...]] * n_heads, axis=-1)
    z = _dot(h, wz_ref[...]).astype(BF16)
    u = _dot(z, wg_ref[...]) + bg_ref[...]
    g_s[...] = (jnp.minimum(u, 0.0) - jnp.log(1.0 + jnp.exp(-jnp.abs(u)))) * (1.0 / GLA_GATE_TEMP)

    c = GLA_CHUNK
    ri = lax.broadcasted_iota(jnp.int32, (c, c), 0)
    ci = lax.broadcasted_iota(jnp.int32, (c, c), 1)
    causal = ri >= ci
    tril = jnp.where(causal, 1.0, 0.0).astype(BF16)

    heads = [(slice(hd * dk, (hd + 1) * dk), slice(hd * dv, (hd + 1) * dv)) for hd in range(n_heads)]
    group = min(GLA_CHUNK_GROUP, ts // c)

    def intra(m, carry):
        ns = [m * group + i for i in range(group)]
        rows = [pl.ds(pl.multiple_of(n * c, c), c) for n in ns]
        gs = [_split_bf16(g_s[r, :]) for r in rows]
        bcs = [_dot(tril, hi) + _dot(tril, lo) for hi, lo in gs]
        qes, kes = [], []
        for n, r, bc in zip(ns, rows, bcs):
            bl = bc[c - 1:c, :]
            kk = k_s[r, :]
            qes.append((q_s[r, :] * jnp.exp(bc)).astype(BF16))
            kes.append((kk * jnp.exp(-bc)).astype(BF16))
            qe_s[r, :] = qes[-1]
            kd_s[r, :] = (kk * jnp.exp(bl - bc)).astype(BF16)
            el_s[pl.ds(n, 1), :] = jnp.exp(bl)
        scs = [[_dot_nt(q_e[:, ks], k_e[:, ks]) for ks, _ in heads] for q_e, k_e in zip(qes, kes)]
        scs = [[jnp.where(causal, sc, 0.0).astype(BF16) for sc in per_chunk] for per_chunk in scs]
        for r, per_chunk in zip(rows, scs):
            vv = v_s[r, :]
            for sc, (_, vs) in zip(per_chunk, heads):
                oi_s[r, vs] = _dot(sc, vv[:, vs])
        return carry

    lax.fori_loop(0, ts // c // group, intra, 0)

    def inter(m, carry):
        ns = [m * group + i for i in range(group)]
        rows = [pl.ds(pl.multiple_of(n * c, c), c) for n in ns]
        incs = []
        for r in rows:
            vv, k_d = v_s[r, :], kd_s[r, :]
            incs.append([_dot_tn(vv[:, vs], k_d[:, ks]) for ks, vs in heads])
        for n, r, inc in zip(ns, rows, incs):
            q_e = qe_s[r, :]
            e_l = el_s[pl.ds(n, 1), :]
            sts = [state_ref[hd] for hd in range(n_heads)]
            carried = [_dot_nt(q_e[:, ks], st.astype(BF16)) for (ks, _), st in zip(heads, sts)]
            for hd, (ks, _) in enumerate(heads):
                state_ref[hd] = sts[hd] * e_l[:, ks] + inc[hd]
            for o_c, (_, vs) in zip(carried, heads):
                o = oi_s[r, vs] + o_c
                o = o * lax.rsqrt(jnp.mean(o * o, axis=-1, keepdims=True) + NORM_EPS)
                o_s[r, vs] = (o * r_s[r, vs]).astype(BF16)
        return carry

    lax.fori_loop(0, ts // c // group, inter, 0)
    x1 = x + mod[2:3] * _dot(o_s[...], wo_ref[...])
    x1_ref[0] = x1
    _router_epilogue(x1, mod, gain_ref[1:2], wr_hi_ref, wr_lo_ref, br_ref, n_groups, n_experts,
                     h2_ref, ids_ref, wts_ref, hist_ref, jnp.logical_and(b == 0, s == 0))
    if fused:
        pl.when(tile == n_tiles - 1)(drain)


def _gla_layer(x, moe, mod, gain, w_in, w_gate, b_gate, norm_g, w_out, router, layer, n_groups, n_experts):
    b, s, d = x.shape
    ts = min(SEQ_TILE_GLA, s)
    nst = s // ts
    rank, dk_all = w_gate.shape
    dv = norm_g.shape[0]
    n_heads = d // dv
    n_proj = 2 * dk_all + 2 * d
    wp = w_in[:, :n_proj].astype(BF16)
    wz = jnp.pad(w_in[:, n_proj:], ((0, 0), (0, LANES - rank))).astype(BF16)
    wg = jnp.pad(w_gate, ((0, LANES - rank), (0, 0))).astype(BF16)
    r_in, r_out = _router_specs(d, ts, nst, n_experts, layer)
    m_in, m_scratch = _moe_specs(moe, ts, nst, d) if moe else ([], [])
    const = lambda bb, ss: (0, 0)
    once = pl.Buffered(1)
    return pl.pallas_call(
        functools.partial(_gla_kernel, n_heads=n_heads, n_groups=n_groups, n_experts=n_experts, fused=bool(moe),
                          n_tiles=b * nst),
        grid=(b, nst),
        in_specs=m_in + [
            pl.BlockSpec((1, ts, d), lambda bb, ss: (bb, ss, 0)),
            pl.BlockSpec((1, 6, d), lambda bb, ss: (bb, 0, 0)),
            pl.BlockSpec((2, d), const),
            pl.BlockSpec((d, n_proj), const, pipeline_mode=once),
            pl.BlockSpec((d, LANES), const),
            pl.BlockSpec((LANES, dk_all), const),
            pl.BlockSpec((1, dk_all), const),
            pl.BlockSpec((1, dv), const),
            pl.BlockSpec((d, d), const, pipeline_mode=once),
        ] + r_in,
        out_specs=r_out,
        out_shape=_router_out_shapes(b, s, d, ts, n_experts),
        scratch_shapes=[
            pltpu.VMEM((n_heads, dv, dk_all // n_heads), F32),
            pltpu.VMEM((ts, dk_all), F32),
            pltpu.VMEM((ts, dk_all), F32),
            pltpu.VMEM((ts, dk_all), F32),
            pltpu.VMEM((ts, d), BF16),
            pltpu.VMEM((ts, d), F32),
            pltpu.VMEM((ts, d), BF16),
            pltpu.VMEM((ts, dk_all), BF16),
            pltpu.VMEM((ts, dk_all), BF16),
            pltpu.VMEM((ts // GLA_CHUNK, dk_all), F32),
            pltpu.VMEM((ts, d), F32),
        ] + m_scratch,
        compiler_params=pltpu.CompilerParams(
            dimension_semantics=("arbitrary", "arbitrary"), vmem_limit_bytes=VMEM_LIMIT),
    )(*_moe_args(moe), x, mod, gain, wp, wz, wg, b_gate.reshape(1, dk_all), norm_g.reshape(1, dv), w_out.astype(BF16),
      *router)


def _pos_kernel(ids_ref, off_ref, pos_ref, run_ref):
    i = pl.program_id(0)
    tiles, idx_rows, _ = pos_ref.shape
    tp = idx_rows * LANES // MOE_TOP_E
    n_e = off_ref.shape[0]

    @pl.when(i == 0)
    def _():
        run_ref[...] = jnp.zeros(run_ref.shape, F32)

    ei = lax.broadcasted_iota(jnp.int32, (n_e, tp), 0)
    tri = jnp.where(lax.broadcasted_iota(jnp.int32, (tp, tp), 0) <= lax.broadcasted_iota(jnp.int32, (tp, tp), 1),
                    1.0, 0.0).astype(BF16)
    base = off_ref[...] + run_ref[...]
    for u in range(tiles):
        ids = ids_ref[:, u * tp:(u + 1) * tp]
        oh0 = ei == ids[0:1]
        oh1 = ei == ids[1:2]
        inc0 = _dot(jnp.where(oh0, 1.0, 0.0).astype(BF16), tri)
        inc1 = _dot(jnp.where(oh1, 1.0, 0.0).astype(BF16), tri)
        tot0 = inc0[:, tp - 1:tp]
        tot1 = inc1[:, tp - 1:tp]
        p0 = jnp.sum(jnp.where(oh0, base + inc0 - 1.0, 0.0), axis=0, keepdims=True)
        p1 = jnp.sum(jnp.where(oh1, base + tot0 + inc1 - 1.0, 0.0), axis=0, keepdims=True)
        p = [pk[:, c * LANES:(c + 1) * LANES] for pk in (p0, p1) for c in range(tp // LANES)]
        pos_ref[u] = jnp.concatenate(p, axis=0).astype(jnp.int32)
        base = base + tot0 + tot1
    run_ref[...] = base - off_ref[...]


def _positions(ids, offsets):
    t = ids.shape[1]
    tp = min(ROW_TILE, t)
    tiles = min(POS_TILES_PER_STEP, t // tp)
    n_e = offsets.shape[0]
    return pl.pallas_call(
        _pos_kernel,
        grid=(t // (tp * tiles),),
        in_specs=[pl.BlockSpec((MOE_TOP_E, tp * tiles), lambda i: (0, i)),
                  pl.BlockSpec((n_e, 1), lambda i: (0, 0))],
        out_specs=pl.BlockSpec((tiles, MOE_TOP_E * tp // LANES, LANES), lambda i: (i, 0, 0)),
        out_shape=jax.ShapeDtypeStruct((t // tp, MOE_TOP_E * tp // LANES, LANES), jnp.int32),
        scratch_shapes=[pltpu.VMEM((n_e, 1), F32)],
        compiler_params=pltpu.CompilerParams(dimension_semantics=("arbitrary",)),
    )(ids, offsets)


def _tile_rows(row):
    return pl.ds(pl.multiple_of(row * SUBLANES, SUBLANES), SUBLANES)


def _for_each_token(tp, fn):
    chunks = tp // LANES
    for c in range(chunks):
        for lane in range(LANES):
            for k in range(MOE_TOP_E):
                fn(k, k * chunks + c, lane, c * LANES + lane)


def _for_each_token_looped(tp, fn):
    chunks = tp // LANES
    for c in range(chunks):
        def body(lane, carry):
            for k in range(MOE_TOP_E):
                fn(k, k * chunks + c, lane, c * LANES + lane)
            return carry
        lax.fori_loop(0, LANES, body, 0)


def _inverse_kernel(pos_ref, inv_ref):
    tp = pos_ref.shape[1] * LANES // MOE_TOP_E
    base = pl.program_id(0) * tp

    def put(k, row, lane, token):
        inv_ref[pos_ref[0, row, lane]] = base + token

    _for_each_token(tp, put)


def _inverse_positions(pos, n_rows):
    nt, idx_rows, _ = pos.shape
    return pl.pallas_call(
        _inverse_kernel,
        grid=(nt,),
        in_specs=[pl.BlockSpec((1, idx_rows, LANES), lambda i: (i, 0, 0), memory_space=pltpu.SMEM)],
        out_specs=pl.BlockSpec(memory_space=pltpu.SMEM),
        out_shape=jax.ShapeDtypeStruct((n_rows,), jnp.int32),
        compiler_params=pltpu.CompilerParams(dimension_semantics=("arbitrary",)),
    )(pos)


def _gmm_kernel(it_ref, iq_ref, lo_ref, hi_ref, qe_ref, n_ref, nq_ref, tok0_ref, tok1_ref, tok2_ref,
                h_hbm, wi_hbm, wo_hbm, ys_ref, wi_s, wo_s, a_s, acc_s, wi_f, wo_f, xb0, xb1, xb2, sem_i, sem_o, sem_x,
                *, layer):
    w = pl.program_id(0)
    f = wo_s.shape[0]
    tm = acc_s.shape[0]
    n = n_ref[0]
    xbufs = (xb0, xb1, xb2)

    def wi_copy(q, slot):
        return pltpu.make_async_copy(wi_hbm.at[layer, qe_ref[q]], wi_f.at[slot], sem_i.at[slot])

    def wo_copy(q, slot):
        return pltpu.make_async_copy(wo_hbm.at[layer, qe_ref[q]], wo_f.at[slot], sem_o.at[slot])

    def row_copy(tok_ref, half, lane, slot):
        return pltpu.make_async_copy(h_hbm.at[_tile_rows(tok_ref[0, half, lane])],
                                     xbufs[slot].at[_tile_rows(half * LANES + lane)], sem_x.at[slot])

    def wait_rows(slot):
        pltpu.make_async_copy(h_hbm.at[pl.ds(0, tm * SUBLANES)], xbufs[slot], sem_x.at[slot]).wait()

    def gather(tok_ref, slot, unrolled):
        for half in range(tm // LANES):
            if unrolled:
                for lane in range(LANES):
                    row_copy(tok_ref, half, lane, slot).start(priority=lane % 2)
            else:
                def body(lane, carry):
                    row_copy(tok_ref, half, lane, slot).start()
                    return carry
                lax.fori_loop(0, LANES, body, 0)

    @pl.when(w <= n)
    def _():
        ja = jnp.minimum(w, n - 1)
        jb = jnp.maximum(w - 1, 0)
        slot_a = lax.rem(w, 2)
        qa = iq_ref[ja]
        qb = iq_ref[jb]

        @pl.when(w == 0)
        def _():
            wi_copy(0, 0).start()
            wo_copy(0, 0).start()
            gather(tok0_ref, 0, False)
            gather(tok1_ref, 1, False)
            a_s[1] = jnp.zeros(a_s.shape[1:], BF16)
            acc_s[...] = jnp.zeros(acc_s.shape, F32)
            wo_s[...] = jnp.zeros(wo_s.shape, BF16)

        @pl.when(jnp.logical_or(w == 1, jnp.logical_and(w >= 2, qb != iq_ref[jnp.maximum(w - 2, 0)])))
        def _():
            slot = lax.rem(qb, 2)
            wo_copy(qb, slot).wait()
            wo_s[...] = wo_f[slot].astype(BF16)

        @pl.when(jnp.logical_or(w == 0, qa != qb))
        def _():
            @pl.when(qa + 1 < nq_ref[0])
            def _():
                wi_copy(qa + 1, lax.rem(qa + 1, 2)).start()
                wo_copy(qa + 1, lax.rem(qa + 1, 2)).start()

            slot = lax.rem(qa, 2)
            wi_copy(qa, slot).wait()
            wi_s[...] = wi_f[slot].astype(BF16)

        def second_stage():
            y = _dot(a_s[1 - slot_a], wo_s[...])
            row = lax.broadcasted_iota(jnp.int32, (tm, 1), 0)
            mine = jnp.logical_and(jnp.logical_and(row >= lo_ref[jb], row < hi_ref[jb]), w >= 1)
            merged = jnp.where(mine, y, acc_s[...])
            acc_s[...] = merged
            _store_token_tiles(ys_ref, merged)

        for slot in range(XS_RING):
            @pl.when(jnp.logical_and(lax.rem(w, XS_RING) == slot, w < n))
            def _():
                wait_rows(slot)
                x = jnp.concatenate(
                    [_load_token_tiles(xbufs[slot], tm, s).astype(BF16) for s in range(SUBLANES)], axis=-1)
                gather(tok2_ref, (slot + 2) % XS_RING, True)
                second_stage()
                gu = _dot(x, wi_s[...])
                gate, up = gu[:, :f], gu[:, f:]
                a_s[slot_a] = (gate * (1.0 / (1.0 + jnp.exp(-gate))) * up).astype(BF16)

            @pl.when(jnp.logical_and(lax.rem(w, XS_RING) == slot, w == n))
            def _():
                second_stage()
                wait_rows(slot)
                wait_rows((slot + 1) % XS_RING)


def _grouped_mlp(plan, h2, inv, w_in, w_out, layer):
    item_tile, item_seq, item_lo, item_hi, seq_expert, n_items, n_seq = plan
    d, f2 = w_in.shape[-2:]
    f = f2 // 2
    tm = GMM_TILE
    rows = tm * SUBLANES
    tok = inv.reshape(-1, tm // LANES, LANES)

    def item(ahead):
        return lambda w, it, iq, lo, hi, qe, n, nq: (it[jnp.minimum(w + ahead, n[0] - 1)], 0, 0)

    def second(w, n):
        return jnp.minimum(jnp.maximum(w - 1, 0), n[0] - 1)

    tok_spec = lambda ahead: pl.BlockSpec((1, tm // LANES, LANES), item(ahead), memory_space=pltpu.SMEM)
    return pl.pallas_call(
        functools.partial(_gmm_kernel, layer=layer),
        grid_spec=pltpu.PrefetchScalarGridSpec(
            num_scalar_prefetch=7,
            grid=(item_tile.shape[0] + 1,),
            in_specs=[tok_spec(0), tok_spec(1), tok_spec(2)] + [pl.BlockSpec(memory_space=pl.ANY)] * 3,
            out_specs=pl.BlockSpec((rows, LANES), lambda w, it, iq, lo, hi, qe, n, nq: (it[second(w, n)], 0)),
            scratch_shapes=[pltpu.VMEM((d, f2), BF16), pltpu.VMEM((f, d), BF16), pltpu.VMEM((2, tm, f), BF16),
                            pltpu.VMEM((tm, d), F32), pltpu.VMEM((2, d, f2), F32), pltpu.VMEM((2, f, d), F32)]
            + [pltpu.VMEM((rows, LANES), F32)] * XS_RING
            + [pltpu.SemaphoreType.DMA((2,)), pltpu.SemaphoreType.DMA((2,)), pltpu.SemaphoreType.DMA((XS_RING,))],
        ),
        out_shape=jax.ShapeDtypeStruct((inv.shape[0] * SUBLANES, LANES), F32),
        compiler_params=pltpu.CompilerParams(
            dimension_semantics=("arbitrary",), vmem_limit_bytes=VMEM_LIMIT),
    )(item_tile, item_seq, item_lo, item_hi, seq_expert, n_items, n_seq, tok, tok, tok, h2, w_in, w_out)


def _combine_kernel(pos0_ref, pos1_ref, pos2_ref, x1_ref, wt_ref, mod_ref, fg_ref, ys_ref, o_ref, *scratch):
    bufs, sems = scratch[:COMBINE_RING], scratch[COMBINE_RING]
    i = pl.program_id(0)
    n = pl.num_programs(0)
    tp = x1_ref.shape[0]

    def issue(idx_ref, slot):
        def one(k, row, lane, token):
            pltpu.make_async_copy(ys_ref.at[_tile_rows(idx_ref[0, row, lane])], bufs[slot].at[k, _tile_rows(token)],
                                  sems.at[slot]).start(priority=k)
        _for_each_token(tp, one)

    def compute(slot):
        wt = wt_ref[...]
        gate2 = mod_ref[0, 5:6]
        chunks = []
        for s in range(SUBLANES):
            rows = pl.ds(s, tp, stride=SUBLANES)
            cols = slice(s * LANES, (s + 1) * LANES)
            y = wt[:, 0:1] * bufs[slot][0, rows, :] + wt[:, 1:2] * bufs[slot][1, rows, :]
            chunks.append(x1_ref[:, cols] + gate2[:, cols] * y)
        x = jnp.concatenate(chunks, axis=-1)
        o_ref[...] = x * lax.rsqrt(jnp.mean(x * x, axis=-1, keepdims=True) + NORM_EPS) * fg_ref[...]

    @pl.when(i == 0)
    def _():
        issue(pos0_ref, 0)

        @pl.when(n > 1)
        def _():
            issue(pos1_ref, 1)

    for slot in range(COMBINE_RING):
        @pl.when(lax.rem(i, COMBINE_RING) == slot)
        def _():
            for k in range(MOE_TOP_E):
                pltpu.make_async_copy(ys_ref.at[pl.ds(0, tp * SUBLANES)], bufs[slot].at[k], sems.at[slot]).wait()

            @pl.when(i + 2 < n)
            def _():
                issue(pos2_ref, (slot + 2) % COMBINE_RING)
                compute(slot)

            @pl.when(i + 2 >= n)
            def _():
                compute(slot)


def _combine(pos, x1, wts_t, mod, final_g, ys, seq_len):
    nt, idx_rows, _ = pos.shape
    t, d = x1.shape
    tp = t // nt
    per_seq = seq_len // tp
    idx_spec = lambda ahead: pl.BlockSpec((1, idx_rows, LANES), lambda i: (jnp.minimum(i + ahead, nt - 1), 0, 0),
                                          memory_space=pltpu.SMEM)
    return pl.pallas_call(
        _combine_kernel,
        grid=(nt,),
        in_specs=[
            idx_spec(0), idx_spec(1), idx_spec(2),
            pl.BlockSpec((tp, d), lambda i: (i, 0)),
            pl.BlockSpec((tp, MOE_TOP_E), lambda i: (i, 0)),
            pl.BlockSpec((1, 6, d), lambda i: (i // per_seq, 0, 0)),
            pl.BlockSpec((1, d), lambda i: (0, 0)),
            pl.BlockSpec(memory_space=pl.ANY),
        ],
        out_specs=pl.BlockSpec((tp, d), lambda i: (i, 0)),
        out_shape=jax.ShapeDtypeStruct((t, d), F32),
        scratch_shapes=[pltpu.VMEM((MOE_TOP_E, tp * SUBLANES, LANES), F32)] * COMBINE_RING
        + [pltpu.SemaphoreType.DMA((COMBINE_RING,))],
        compiler_params=pltpu.CompilerParams(
            dimension_semantics=("arbitrary",), vmem_limit_bytes=VMEM_LIMIT),
    )(pos, pos, pos, x1, wts_t, mod, final_g, ys)


def _router_weights(w_group, b_group, w_expert, b_expert):
    g, e = w_group.shape[-1], w_expert.shape[-1]
    pad = ROUTER_ROWS - g - e
    w = jnp.swapaxes(jnp.concatenate([w_group, w_expert], axis=2), 1, 2)
    hi, lo = _split_bf16(jnp.pad(w, ((0, 0), (0, pad), (0, 0))))
    bias = jnp.pad(jnp.concatenate([b_group, b_expert], axis=1), ((0, 0), (0, pad)))[:, :, None]
    return hi, lo, bias


def _tile_plan(hist, n_rows):
    n_e = hist.shape[0]
    tm = GMM_TILE
    counts = jnp.sum(hist, axis=1).astype(jnp.int32)
    ends = jnp.cumsum(counts)
    starts = ends - counts
    first_tile = starts // tm
    items_per = jnp.where(counts > 0, (ends - 1) // tm - first_tile + 1, 0)
    item_ends = jnp.cumsum(items_per)
    n_items = item_ends[-1:]
    max_items = n_rows // tm + n_e - 1
    w = jnp.minimum(jnp.arange(max_items, dtype=jnp.int32), n_items[0] - 1)
    item_expert = jnp.sum(w[:, None] >= item_ends[None, :], axis=1).astype(jnp.int32)
    own = item_expert[:, None] == jnp.arange(n_e, dtype=jnp.int32)[None, :]
    pick = lambda per_expert: jnp.sum(jnp.where(own, per_expert[None, :], 0), axis=1)
    item_tile = pick(first_tile) + w - pick(item_ends - items_per)
    item_lo = jnp.maximum(pick(starts) - item_tile * tm, 0)
    item_hi = jnp.minimum(pick(ends) - item_tile * tm, tm)
    offsets = starts.astype(F32).reshape(n_e, 1)
    used_ends = jnp.cumsum((counts > 0).astype(jnp.int32))
    n_seq = used_ends[-1:]
    q = jnp.minimum(jnp.arange(n_e, dtype=jnp.int32), n_seq[0] - 1)
    seq_expert = jnp.sum(q[:, None] >= used_ends[None, :], axis=1).astype(jnp.int32)
    item_seq = pick(used_ends - 1)
    return offsets, (item_tile.astype(jnp.int32), item_seq.astype(jnp.int32), item_lo.astype(jnp.int32),
                     item_hi.astype(jnp.int32), seq_expert, n_items.astype(jnp.int32), n_seq.astype(jnp.int32))


def kernel(x, c, norm_gain, w_mod, b_mod, pool_w, pool_b, pool_scale, gla_w_in, gla_w_gate, gla_b_gate, gla_norm_g, gla_w_out, moe_w_group, moe_b_group, moe_w_expert, moe_b_expert, moe_w_in, moe_w_out, final_norm_g):
    b, s, d = x.shape
    depth = w_mod.shape[0]
    t = b * s
    n_groups = moe_w_group.shape[-1]
    n_experts = moe_w_expert.shape[-1]
    n_rows = MOE_TOP_E * t
    mod_all = _modulation(c, w_mod, b_mod).reshape(depth, b, 6, d)
    fg = final_norm_g.reshape(1, d)
    router = _router_weights(moe_w_group, moe_b_group, moe_w_expert, moe_b_expert)
    moe = None
    for i in range(depth):
        mod = mod_all[i]
        j = i // 2
        if i % 2 == 0:
            x1, h2, ids, wts, hist = _pool_layer(x, moe, mod, norm_gain[i], pool_w[j], pool_b[j], pool_scale[j],
                                                 router, i, n_groups, n_experts)
        else:
            x1, h2, ids, wts, hist = _gla_layer(x, moe, mod, norm_gain[i], gla_w_in[j], gla_w_gate[j], gla_b_gate[j],
                                                gla_norm_g[j], gla_w_out[j], router, i, n_groups, n_experts)
        offsets, plan = _tile_plan(hist, n_rows)
        pos = _positions(ids, offsets)
        inv = _inverse_positions(pos, n_rows)
        ys = _grouped_mlp(plan, h2, inv, moe_w_in, moe_w_out, i)
        x, moe = x1, (pos, wts.T, mod, ys)
    pos, wts_t, mod, ys = moe
    x = _combine(pos, x.reshape(t, d), wts_t, mod, fg, ys, s).reshape(b, s, d)
    return x
```

```python
import functools

import jax
import jax.numpy as jnp
from jax import lax
from jax.experimental import pallas as pl
from jax.experimental.pallas import tpu as pltpu

F32 = jnp.float32
BF16 = jnp.bfloat16

NORM_EPS = 1e-6
POOL_WINDOWS = (2, 4, 8, 16)
POOL_HALO = 16
GLA_CHUNK = 64
GLA_GATE_TEMP = 16.0
GLA_CHUNK_GROUP = 8
MOE_TOP_E = 2
LANES = 128
SUBLANES = 8
ROUTER_ROWS = 128
VMEM_LIMIT = 56 * 1024 * 1024

MOD_COL_TILES = 4
SEQ_TILE_POOL = 512
SEQ_TILE_GLA = 512
GMM_TILE = 256
XS_RING = 3
COMBINE_RING = 3
ROW_TILE = 512
POS_TILES_PER_STEP = 4


def _dot(a, b):
    return jnp.dot(a, b, preferred_element_type=F32)


def _dot_nt(a, b):
    return lax.dot_general(a, b, (((1,), (1,)), ((), ())), preferred_element_type=F32)


def _dot_tn(a, b):
    return lax.dot_general(a, b, (((0,), (0,)), ((), ())), preferred_element_type=F32)


def _split_bf16(x):
    hi = x.astype(BF16)
    lo = (x - hi.astype(F32)).astype(BF16)
    return hi, lo


def _store_token_tiles(ref, x):
    n = x.shape[0]
    for s in range(SUBLANES):
        ref[pl.ds(s, n, stride=SUBLANES), :] = x[:, s * LANES:(s + 1) * LANES]


def _load_token_tiles(ref, n, s):
    return ref[pl.ds(s, n, stride=SUBLANES), :]


def _norm_mod(x, gain, scale, shift):
    ms = jnp.mean(x * x, axis=-1, keepdims=True)
    return x * lax.rsqrt(ms + NORM_EPS) * (gain * (1.0 + scale)) + shift


def _mod_kernel(c_ref, w_ref, b_ref, o_ref):
    c = c_ref[...]
    sc = (c * (1.0 / (1.0 + jnp.exp(-c)))).astype(BF16)
    o_ref[0] = _dot(sc, w_ref[0].astype(BF16)) + b_ref[0]


def _modulation(c, w_mod, b_mod):
    depth, d, n = w_mod.shape
    b = c.shape[0]
    tn = n // MOD_COL_TILES
    return pl.pallas_call(
        _mod_kernel,
        grid=(depth, n // tn),
        in_specs=[
            pl.BlockSpec((b, d), lambda i, j: (0, 0)),
            pl.BlockSpec((1, d, tn), lambda i, j: (i, 0, j)),
            pl.BlockSpec((1, 1, tn), lambda i, j: (i, 0, j)),
        ],
        out_specs=pl.BlockSpec((1, b, tn), lambda i, j: (i, 0, j)),
        out_shape=jax.ShapeDtypeStruct((depth, b, n), F32),
        compiler_params=pltpu.CompilerParams(vmem_limit_bytes=VMEM_LIMIT),
    )(c, w_mod, b_mod.reshape(depth, 1, n))


def _router_epilogue(x1, mod, gain2, wr_hi_ref, wr_lo_ref, br_ref, n_groups, n_experts,
                     h2_ref, ids_ref, wts_ref, hist_ref, first_step):
    h2 = _norm_mod(x1, gain2, mod[4:5], mod[3:4])
    _store_token_tiles(h2_ref, h2)
    hh, hl = _split_bf16(h2)
    wh = wr_hi_ref[0]
    both = _dot_nt(jnp.concatenate([wh, wr_lo_ref[0]], axis=0), hh)
    lt = both[:ROUTER_ROWS] + both[ROUTER_ROWS:] + _dot_nt(wh, hl) + br_ref[0]
    ts = lt.shape[1]
    per = n_experts // n_groups
    lg = lt[0:n_groups]
    mg = jnp.max(lg, axis=0, keepdims=True)
    p_g = 1.0 / jnp.sum(jnp.exp(lg - mg), axis=0, keepdims=True)
    gi = lax.broadcasted_iota(jnp.int32, lg.shape, 0)
    g_idx = jnp.min(jnp.where(lg == mg, gi, n_groups), axis=0, keepdims=True)
    sel = jnp.zeros((per, ts), F32)
    for g in range(n_groups):
        sel = jnp.where(g_idx == g, lt[n_groups + g * per:n_groups + (g + 1) * per], sel)
    ei = lax.broadcasted_iota(jnp.int32, sel.shape, 0)
    m1 = jnp.max(sel, axis=0, keepdims=True)
    i1 = jnp.min(jnp.where(sel == m1, ei, per), axis=0, keepdims=True)
    rest = jnp.where(ei == i1, -jnp.inf, sel)
    m2 = jnp.max(rest, axis=0, keepdims=True)
    i2 = jnp.min(jnp.where(rest == m2, ei, per), axis=0, keepdims=True)
    e21 = jnp.exp(m2 - m1)
    w1 = p_g / (1.0 + e21)
    w2 = p_g * e21 / (1.0 + e21)
    id1 = g_idx * per + i1
    id2 = g_idx * per + i2
    ids_ref[...] = jnp.concatenate([id1, id2], axis=0)
    wts_ref[...] = jnp.concatenate([w1, w2], axis=0)
    xi = lax.broadcasted_iota(jnp.int32, (n_experts, ts), 0)
    cnt = (xi == id1).astype(F32) + (xi == id2).astype(F32)

    @pl.when(first_step)
    def _():
        hist_ref[...] = cnt

    @pl.when(jnp.logical_not(first_step))
    def _():
        hist_ref[...] += cnt


def _router_specs(d, ts, n_seq_tiles, n_experts, layer):
    in_specs = [
        pl.BlockSpec((1, ROUTER_ROWS, d), lambda b, s: (layer, 0, 0)),
        pl.BlockSpec((1, ROUTER_ROWS, d), lambda b, s: (layer, 0, 0)),
        pl.BlockSpec((1, ROUTER_ROWS, 1), lambda b, s: (layer, 0, 0)),
    ]
    out_specs = [
        pl.BlockSpec((1, ts, d), lambda b, s: (b, s, 0)),
        pl.BlockSpec((ts * SUBLANES, LANES), lambda b, s: (b * n_seq_tiles + s, 0)),
        pl.BlockSpec((MOE_TOP_E, ts), lambda b, s: (0, b * n_seq_tiles + s)),
        pl.BlockSpec((MOE_TOP_E, ts), lambda b, s: (0, b * n_seq_tiles + s)),
        pl.BlockSpec((n_experts, ts), lambda b, s: (0, 0)),
    ]
    return in_specs, out_specs


def _router_out_shapes(b, s, d, ts, n_experts):
    assert d == SUBLANES * LANES, "token-per-tile row layout needs d_model == one (8,128) tile"
    return [
        jax.ShapeDtypeStruct((b, s, d), F32),
        jax.ShapeDtypeStruct((b * s * SUBLANES, LANES), F32),
        jax.ShapeDtypeStruct((MOE_TOP_E, b * s), jnp.int32),
        jax.ShapeDtypeStruct((MOE_TOP_E, b * s), F32),
        jax.ShapeDtypeStruct((n_experts, ts), F32),
    ]


def _moe_specs(moe, ts, n_seq_tiles, d):
    pos = moe[0]
    nt, idx_rows, _ = pos.shape
    assert idx_rows * LANES // MOE_TOP_E == ts, "row tile of the index blocks must equal the mixer's sequence tile"
    tile = lambda bb, ss: bb * n_seq_tiles + ss
    idx_spec = lambda ahead: pl.BlockSpec(
        (1, idx_rows, LANES), lambda bb, ss: (jnp.minimum(tile(bb, ss) + ahead, nt - 1), 0, 0),
        memory_space=pltpu.SMEM)
    in_specs = [idx_spec(0), idx_spec(1),
                pl.BlockSpec((ts, MOE_TOP_E), lambda bb, ss: (tile(bb, ss), 0)),
                pl.BlockSpec((1, 6, d), lambda bb, ss: (bb, 0, 0)),
                pl.BlockSpec(memory_space=pl.ANY)]
    scratch = [pltpu.VMEM((MOE_TOP_E, ts * SUBLANES, LANES), F32),
               pltpu.SemaphoreType.DMA(()),
               pltpu.VMEM((ts, d), F32)]
    return in_specs, scratch


N_MOE_REFS = 5


def _moe_args(moe):
    return (moe[0],) * 2 + tuple(moe[1:]) if moe else ()


def _moe_combine(j, moe_refs, x1p_ref, scratch):
    pos0_ref, pos1_ref, wtp_ref, modp_ref, ysp_ref = moe_refs
    buf, sem, x_s = scratch
    tp = x_s.shape[0]

    def copy(idx_ref, k, row, lane, token):
        return pltpu.make_async_copy(ysp_ref.at[_tile_rows(idx_ref[0, row, lane])],
                                     buf.at[k, _tile_rows(token)], sem)

    def wait():
        for k in range(MOE_TOP_E):
            pltpu.make_async_copy(ysp_ref.at[pl.ds(0, tp * SUBLANES)], buf.at[k], sem).wait()

    @pl.when(j == 0)
    def _():
        _for_each_token_looped(tp, lambda k, row, lane, token: copy(pos0_ref, k, row, lane, token).start(priority=k))

    wait()
    wt = wtp_ref[...]
    gate2 = modp_ref[0, 5:6]
    for s in range(SUBLANES):
        rows = pl.ds(s, tp, stride=SUBLANES)
        cols = slice(s * LANES, (s + 1) * LANES)
        y = wt[:, 0:1] * buf[0, rows, :] + wt[:, 1:2] * buf[1, rows, :]
        x_s[:, cols] = x1p_ref[0, :, cols] + gate2[:, cols] * y

    def issue_ahead():
        _for_each_token(tp, lambda k, row, lane, token: copy(pos1_ref, k, row, lane, token).start(priority=k))

    return issue_ahead, wait


def _pool_kernel(*refs, n_groups, n_experts, fused, n_tiles):
    moe_refs, refs = (refs[:N_MOE_REFS], refs[N_MOE_REFS:]) if fused else (None, refs)
    (x_ref, mod_ref, gain_ref, pw_ref, pb_ref, ps_ref, wr_hi_ref, wr_lo_ref, br_ref,
     x1_ref, h2_ref, ids_ref, wts_ref, hist_ref, hbuf_ref, sa_ref, sb_ref) = refs[:17]
    b, s = pl.program_id(0), pl.program_id(1)
    if fused:
        tile = b * pl.num_programs(1) + s
        issue_ahead, drain = _moe_combine(tile, moe_refs, x_ref, refs[17:])
        x = refs[19][...]
    else:
        x = x_ref[0]
    ts, d = x.shape
    mod = mod_ref[0]
    h = _norm_mod(x, gain_ref[0:1], mod[1:2], mod[0:1])

    first = SUBLANES + POOL_HALO
    n = first + ts

    @pl.when(s == 0)
    def _():
        hbuf_ref[0:first] = jnp.zeros((first, d), F32)
        sa_ref[0:SUBLANES] = jnp.zeros((SUBLANES, d), F32)
        sb_ref[0:SUBLANES] = jnp.zeros((SUBLANES, d), F32)

    if fused:
        issue_ahead()
    hbuf_ref[first:n] = h
    pos = (s * ts + 1 + lax.broadcasted_iota(jnp.int32, (ts, 1), 0)).astype(F32)
    cg = d // len(POOL_WINDOWS)
    src, dst = hbuf_ref, sa_ref
    ys = []
    for gi, win in enumerate(POOL_WINDOWS):
        half = win // 2
        assert win == 2 ** (gi + 1) and win <= POOL_HALO
        lo = gi * cg
        dst[SUBLANES:n, lo:] = src[SUBLANES:n, lo:] + src[SUBLANES - half:n - half, lo:]
        cols = slice(lo, lo + cg)
        dgi = dst[first:n, cols] / jnp.minimum(pos, float(win)) - h[:, cols]
        ys.append(_dot(dgi.astype(BF16), pw_ref[gi]))
        src, dst = dst, (sb_ref if dst is sa_ref else sa_ref)
    hbuf_ref[SUBLANES:first] = hbuf_ref[n - POOL_HALO:n]
    x1 = x + (jnp.concatenate(ys, axis=-1) + pb_ref[...]) * (ps_ref[...] * mod[2:3])
    x1_ref[0] = x1
    _router_epilogue(x1, mod, gain_ref[1:2], wr_hi_ref, wr_lo_ref, br_ref, n_groups, n_experts,
                     h2_ref, ids_ref, wts_ref, hist_ref, jnp.logical_and(b == 0, s == 0))
    if fused:
        pl.when(tile == n_tiles - 1)(drain)


def _pool_layer(x, moe, mod, gain, pw, pb, ps, router, layer, n_groups, n_experts):
    b, s, d = x.shape
    ts = min(SEQ_TILE_POOL, s)
    nst = s // ts
    g, cg = pw.shape[0], pw.shape[1]
    r_in, r_out = _router_specs(d, ts, nst, n_experts, layer)
    m_in, m_scratch = _moe_specs(moe, ts, nst, d) if moe else ([], [])
    return pl.pallas_call(
        functools.partial(_pool_kernel, n_groups=n_groups, n_experts=n_experts, fused=bool(moe), n_tiles=b * nst),
        grid=(b, nst),
        in_specs=m_in + [
            pl.BlockSpec((1, ts, d), lambda bb, ss: (bb, ss, 0)),
            pl.BlockSpec((1, 6, d), lambda bb, ss: (bb, 0, 0)),
            pl.BlockSpec((2, d), lambda bb, ss: (0, 0)),
            pl.BlockSpec((g, cg, cg), lambda bb, ss: (0, 0, 0)),
            pl.BlockSpec((1, d), lambda bb, ss: (0, 0)),
            pl.BlockSpec((1, d), lambda bb, ss: (0, 0)),
        ] + r_in,
        out_specs=r_out,
        out_shape=_router_out_shapes(b, s, d, ts, n_experts),
        scratch_shapes=[pltpu.VMEM((SUBLANES + POOL_HALO + ts, d), F32)] * 3 + m_scratch,
        compiler_params=pltpu.CompilerParams(
            dimension_semantics=("arbitrary", "arbitrary"), vmem_limit_bytes=VMEM_LIMIT),
    )(*_moe_args(moe), x, mod, gain, pw.astype(BF16), pb.reshape(1, d), ps.reshape(1, d), *router)


N_GLA_REFS = 28


def _gla_kernel(*refs, n_heads, n_groups, n_experts, fused, n_tiles):
    moe_refs, refs = (refs[:N_MOE_REFS], refs[N_MOE_REFS:]) if fused else (None, refs)
    (x_ref, mod_ref, gain_ref, wp_ref, wz_ref, wg_ref, bg_ref, ng_ref, wo_ref, wr_hi_ref, wr_lo_ref, br_ref,
     x1_ref, h2_ref, ids_ref, wts_ref, hist_ref, state_ref, q_s, k_s, g_s, v_s, r_s, o_s,
     qe_s, kd_s, el_s, oi_s) = refs[:N_GLA_REFS]
    b, s = pl.program_id(0), pl.program_id(1)
    if fused:
        tile = b * pl.num_programs(1) + s
        issue_ahead, drain = _moe_combine(tile, moe_refs, x_ref, refs[N_GLA_REFS:])
        x = refs[N_GLA_REFS + 2][...]
    else:
        x = x_ref[0]
    ts, d = x.shape
    dk_all = q_s.shape[1]
    dk = dk_all // n_heads
    dv = d // n_heads
    mod = mod_ref[0]
    h = _norm_mod(x, gain_ref[0:1], mod[1:2], mod[0:1]).astype(BF16)

    @pl.when(s == 0)
    def _():
        state_ref[...] = jnp.zeros(state_ref.shape, F32)

    if fused:
        issue_ahead()
    q_s[...] = _dot(h, wp_ref[:, 0:dk_all]) * (dk ** -0.5)
    k_s[...] = _dot(h, wp_ref[:, dk_all:2 * dk_all])
    v_s[...] = _dot(h, wp_ref[:, 2 * dk_all:2 * dk_all + d]).astype(BF16)
    r = _dot(h, wp_ref[:, 2 * dk_all + d:2 * dk_all + 2 * d])
    r_s[...] = r * (1.0 / (1.0 + jnp.exp(-r))) * jnp.concatenate([ng_ref[...]] * n_heads, axis=-1)
    z = _dot(h, wz_ref[...]).astype(BF16)
    u = _dot(z, wg_ref[...]) + bg_ref[...]
    g_s[...] = (jnp.minimum(u, 0.0) - jnp.log(1.0 + jnp.exp(-jnp.abs(u)))) * (1.0 / GLA_GATE_TEMP)

    c = GLA_CHUNK
    ri = lax.broadcasted_iota(jnp.int32, (c, c), 0)
    ci = lax.broadcasted_iota(jnp.int32, (c, c), 1)
    causal = ri >= ci
    tril = jnp.where(causal, 1.0, 0.0).astype(BF16)

    heads = [(slice(hd * dk, (hd + 1) * dk), slice(hd * dv, (hd + 1) * dv)) for hd in range(n_heads)]
    group = min(GLA_CHUNK_GROUP, ts // c)

    def intra(m, carry):
        ns = [m * group + i for i in range(group)]
        rows = [pl.ds(pl.multiple_of(n * c, c), c) for n in ns]
        gs = [_split_bf16(g_s[r, :]) for r in rows]
        bcs = [_dot(tril, hi) + _dot(tril, lo) for hi, lo in gs]
        qes, kes = [], []
        for n, r, bc in zip(ns, rows, bcs):
            bl = bc[c - 1:c, :]
            kk = k_s[r, :]
            qes.append((q_s[r, :] * jnp.exp(bc)).astype(BF16))
            kes.append((kk * jnp.exp(-bc)).astype(BF16))
            qe_s[r, :] = qes[-1]
            kd_s[r, :] = (kk * jnp.exp(bl - bc)).astype(BF16)
            el_s[pl.ds(n, 1), :] = jnp.exp(bl)
        scs = [[_dot_nt(q_e[:, ks], k_e[:, ks]) for ks, _ in heads] for q_e, k_e in zip(qes, kes)]
        scs = [[jnp.where(causal, sc, 0.0).astype(BF16) for sc in per_chunk] for per_chunk in scs]
        for r, per_chunk in zip(rows, scs):
            vv = v_s[r, :]
            for sc, (_, vs) in zip(per_chunk, heads):
                oi_s[r, vs] = _dot(sc, vv[:, vs])
        return carry

    lax.fori_loop(0, ts // c // group, intra, 0)

    def inter(m, carry):
        ns = [m * group + i for i in range(group)]
        rows = [pl.ds(pl.multiple_of(n * c, c), c) for n in ns]
        incs = []
        for r in rows:
            vv, k_d = v_s[r, :], kd_s[r, :]
            incs.append([_dot_tn(vv[:, vs], k_d[:, ks]) for ks, vs in heads])
        for n, r, inc in zip(ns, rows, incs):
            q_e = qe_s[r, :]
            e_l = el_s[pl.ds(n, 1), :]
            sts = [state_ref[hd] for hd in range(n_heads)]
            carried = [_dot_nt(q_e[:, ks], st.astype(BF16)) for (ks, _), st in zip(heads, sts)]
            for hd, (ks, _) in enumerate(heads):
                state_ref[hd] = sts[hd] * e_l[:, ks] + inc[hd]
            for o_c, (_, vs) in zip(carried, heads):
                o = oi_s[r, vs] + o_c
                o = o * lax.rsqrt(jnp.mean(o * o, axis=-1, keepdims=True) + NORM_EPS)
                o_s[r, vs] = (o * r_s[r, vs]).astype(BF16)
        return carry

    lax.fori_loop(0, ts // c // group, inter, 0)
    x1 = x + mod[2:3] * _dot(o_s[...], wo_ref[...])
    x1_ref[0] = x1
    _router_epilogue(x1, mod, gain_ref[1:2], wr_hi_ref, wr_lo_ref, br_ref, n_groups, n_experts,
                     h2_ref, ids_ref, wts_ref, hist_ref, jnp.logical_and(b == 0, s == 0))
    if fused:
        pl.when(tile == n_tiles - 1)(drain)


def _gla_layer(x, moe, mod, gain, w_in, w_gate, b_gate, norm_g, w_out, router, layer, n_groups, n_experts):
    b, s, d = x.shape
    ts = min(SEQ_TILE_GLA, s)
    nst = s // ts
    rank, dk_all = w_gate.shape
    dv = norm_g.shape[0]
    n_heads = d // dv
    n_proj = 2 * dk_all + 2 * d
    wp = w_in[:, :n_proj].astype(BF16)
    wz = jnp.pad(w_in[:, n_proj:], ((0, 0), (0, LANES - rank))).astype(BF16)
    wg = jnp.pad(w_gate, ((0, LANES - rank), (0, 0))).astype(BF16)
    r_in, r_out = _router_specs(d, ts, nst, n_experts, layer)
    m_in, m_scratch = _moe_specs(moe, ts, nst, d) if moe else ([], [])
    const = lambda bb, ss: (0, 0)
    once = pl.Buffered(1)
    return pl.pallas_call(
        functools.partial(_gla_kernel, n_heads=n_heads, n_groups=n_groups, n_experts=n_experts, fused=bool(moe),
                          n_tiles=b * nst),
        grid=(b, nst),
        in_specs=m_in + [
            pl.BlockSpec((1, ts, d), lambda bb, ss: (bb, ss, 0)),
            pl.BlockSpec((1, 6, d), lambda bb, ss: (bb, 0, 0)),
            pl.BlockSpec((2, d), const),
            pl.BlockSpec((d, n_proj), const, pipeline_mode=once),
            pl.BlockSpec((d, LANES), const),
            pl.BlockSpec((LANES, dk_all), const),
            pl.BlockSpec((1, dk_all), const),
            pl.BlockSpec((1, dv), const),
            pl.BlockSpec((d, d), const, pipeline_mode=once),
        ] + r_in,
        out_specs=r_out,
        out_shape=_router_out_shapes(b, s, d, ts, n_experts),
        scratch_shapes=[
            pltpu.VMEM((n_heads, dv, dk_all // n_heads), F32),
            pltpu.VMEM((ts, dk_all), F32),
            pltpu.VMEM((ts, dk_all), F32),
            pltpu.VMEM((ts, dk_all), F32),
            pltpu.VMEM((ts, d), BF16),
            pltpu.VMEM((ts, d), F32),
            pltpu.VMEM((ts, d), BF16),
            pltpu.VMEM((ts, dk_all), BF16),
            pltpu.VMEM((ts, dk_all), BF16),
            pltpu.VMEM((ts // GLA_CHUNK, dk_all), F32),
            pltpu.VMEM((ts, d), F32),
        ] + m_scratch,
        compiler_params=pltpu.CompilerParams(
            dimension_semantics=("arbitrary", "arbitrary"), vmem_limit_bytes=VMEM_LIMIT),
    )(*_moe_args(moe), x, mod, gain, wp, wz, wg, b_gate.reshape(1, dk_all), norm_g.reshape(1, dv), w_out.astype(BF16),
      *router)


def _pos_kernel(ids_ref, off_ref, pos_ref, run_ref):
    i = pl.program_id(0)
    tiles, idx_rows, _ = pos_ref.shape
    tp = idx_rows * LANES // MOE_TOP_E
    n_e = off_ref.shape[0]

    @pl.when(i == 0)
    def _():
        run_ref[...] = jnp.zeros(run_ref.shape, F32)

    ei = lax.broadcasted_iota(jnp.int32, (n_e, tp), 0)
    tri = jnp.where(lax.broadcasted_iota(jnp.int32, (tp, tp), 0) <= lax.broadcasted_iota(jnp.int32, (tp, tp), 1),
                    1.0, 0.0).astype(BF16)
    base = off_ref[...] + run_ref[...]
    for u in range(tiles):
        ids = ids_ref[:, u * tp:(u + 1) * tp]
        oh0 = ei == ids[0:1]
        oh1 = ei == ids[1:2]
        inc0 = _dot(jnp.where(oh0, 1.0, 0.0).astype(BF16), tri)
        inc1 = _dot(jnp.where(oh1, 1.0, 0.0).astype(BF16), tri)
        tot0 = inc0[:, tp - 1:tp]
        tot1 = inc1[:, tp - 1:tp]
        p0 = jnp.sum(jnp.where(oh0, base + inc0 - 1.0, 0.0), axis=0, keepdims=True)
        p1 = jnp.sum(jnp.where(oh1, base + tot0 + inc1 - 1.0, 0.0), axis=0, keepdims=True)
        p = [pk[:, c * LANES:(c + 1) * LANES] for pk in (p0, p1) for c in range(tp // LANES)]
        pos_ref[u] = jnp.concatenate(p, axis=0).astype(jnp.int32)
        base = base + tot0 + tot1
    run_ref[...] = base - off_ref[...]


def _positions(ids, offsets):
    t = ids.shape[1]
    tp = min(ROW_TILE, t)
    tiles = min(POS_TILES_PER_STEP, t // tp)
    n_e = offsets.shape[0]
    return pl.pallas_call(
        _pos_kernel,
        grid=(t // (tp * tiles),),
        in_specs=[pl.BlockSpec((MOE_TOP_E, tp * tiles), lambda i: (0, i)),
                  pl.BlockSpec((n_e, 1), lambda i: (0, 0))],
        out_specs=pl.BlockSpec((tiles, MOE_TOP_E * tp // LANES, LANES), lambda i: (i, 0, 0)),
        out_shape=jax.ShapeDtypeStruct((t // tp, MOE_TOP_E * tp // LANES, LANES), jnp.int32),
        scratch_shapes=[pltpu.VMEM((n_e, 1), F32)],
        compiler_params=pltpu.CompilerParams(dimension_semantics=("arbitrary",)),
    )(ids, offsets)


def _tile_rows(row):
    return pl.ds(pl.multiple_of(row * SUBLANES, SUBLANES), SUBLANES)


def _for_each_token(tp, fn):
    chunks = tp // LANES
    for c in range(chunks):
        for lane in range(LANES):
            for k in range(MOE_TOP_E):
                fn(k, k * chunks + c, lane, c * LANES + lane)


def _for_each_token_looped(tp, fn):
    chunks = tp // LANES
    for c in range(chunks):
        def body(lane, carry):
            for k in range(MOE_TOP_E):
                fn(k, k * chunks + c, lane, c * LANES + lane)
            return carry
        lax.fori_loop(0, LANES, body, 0)


def _inverse_kernel(pos_ref, inv_ref):
    tp = pos_ref.shape[1] * LANES // MOE_TOP_E
    base = pl.program_id(0) * tp

    def put(k, row, lane, token):
        inv_ref[pos_ref[0, row, lane]] = base + token

    _for_each_token(tp, put)


def _inverse_positions(pos, n_rows):
    nt, idx_rows, _ = pos.shape
    return pl.pallas_call(
        _inverse_kernel,
        grid=(nt,),
        in_specs=[pl.BlockSpec((1, idx_rows, LANES), lambda i: (i, 0, 0), memory_space=pltpu.SMEM)],
        out_specs=pl.BlockSpec(memory_space=pltpu.SMEM),
        out_shape=jax.ShapeDtypeStruct((n_rows,), jnp.int32),
        compiler_params=pltpu.CompilerParams(dimension_semantics=("arbitrary",)),
    )(pos)


def _gmm_kernel(it_ref, iq_ref, lo_ref, hi_ref, qe_ref, n_ref, nq_ref, tok0_ref, tok1_ref, tok2_ref,
                h_hbm, wi_hbm, wo_hbm, ys_ref, wi_s, wo_s, a_s, acc_s, wi_f, wo_f, xb0, xb1, xb2, sem_i, sem_o, sem_x,
                *, layer):
    w = pl.program_id(0)
    f = wo_s.shape[0]
    tm = acc_s.shape[0]
    n = n_ref[0]
    xbufs = (xb0, xb1, xb2)

    def wi_copy(q, slot):
        return pltpu.make_async_copy(wi_hbm.at[layer, qe_ref[q]], wi_f.at[slot], sem_i.at[slot])

    def wo_copy(q, slot):
        return pltpu.make_async_copy(wo_hbm.at[layer, qe_ref[q]], wo_f.at[slot], sem_o.at[slot])

    def row_copy(tok_ref, half, lane, slot):
        return pltpu.make_async_copy(h_hbm.at[_tile_rows(tok_ref[0, half, lane])],
                                     xbufs[slot].at[_tile_rows(half * LANES + lane)], sem_x.at[slot])

    def wait_rows(slot):
        pltpu.make_async_copy(h_hbm.at[pl.ds(0, tm * SUBLANES)], xbufs[slot], sem_x.at[slot]).wait()

    def gather(tok_ref, slot, unrolled):
        for half in range(tm // LANES):
            if unrolled:
                for lane in range(LANES):
                    row_copy(tok_ref, half, lane, slot).start()
            else:
                def body(lane, carry):
                    row_copy(tok_ref, half, lane, slot).start()
                    return carry
                lax.fori_loop(0, LANES, body, 0)

    @pl.when(w <= n)
    def _():
        ja = jnp.minimum(w, n - 1)
        jb = jnp.maximum(w - 1, 0)
        slot_a = lax.rem(w, 2)
        qa = iq_ref[ja]
        qb = iq_ref[jb]

        @pl.when(w == 0)
        def _():
            wi_copy(0, 0).start(priority=1)
            wo_copy(0, 0).start(priority=1)
            gather(tok0_ref, 0, False)
            gather(tok1_ref, 1, False)
            a_s[1] = jnp.zeros(a_s.shape[1:], BF16)
            acc_s[...] = jnp.zeros(acc_s.shape, F32)
            wo_s[...] = jnp.zeros(wo_s.shape, BF16)

        @pl.when(jnp.logical_or(w == 1, jnp.logical_and(w >= 2, qb != iq_ref[jnp.maximum(w - 2, 0)])))
        def _():
            slot = lax.rem(qb, 2)
            wo_copy(qb, slot).wait()
            wo_s[...] = wo_f[slot].astype(BF16)

        @pl.when(jnp.logical_or(w == 0, qa != qb))
        def _():
            @pl.when(qa + 1 < nq_ref[0])
            def _():
                wi_copy(qa + 1, lax.rem(qa + 1, 2)).start(priority=1)
                wo_copy(qa + 1, lax.rem(qa + 1, 2)).start(priority=1)

            slot = lax.rem(qa, 2)
            wi_copy(qa, slot).wait()
            wi_s[...] = wi_f[slot].astype(BF16)

        def second_stage():
            y = _dot(a_s[1 - slot_a], wo_s[...])
            row = lax.broadcasted_iota(jnp.int32, (tm, 1), 0)
            mine = jnp.logical_and(jnp.logical_and(row >= lo_ref[jb], row < hi_ref[jb]), w >= 1)
            merged = jnp.where(mine, y, acc_s[...])
            acc_s[...] = merged
            _store_token_tiles(ys_ref, merged)

        for slot in range(XS_RING):
            @pl.when(jnp.logical_and(lax.rem(w, XS_RING) == slot, w < n))
            def _():
                wait_rows(slot)
                x = jnp.concatenate(
                    [_load_token_tiles(xbufs[slot], tm, s).astype(BF16) for s in range(SUBLANES)], axis=-1)
                gather(tok2_ref, (slot + 2) % XS_RING, True)
                second_stage()
                gu = _dot(x, wi_s[...])
                gate, up = gu[:, :f], gu[:, f:]
                a_s[slot_a] = (gate * (1.0 / (1.0 + jnp.exp(-gate))) * up).astype(BF16)

            @pl.when(jnp.logical_and(lax.rem(w, XS_RING) == slot, w == n))
            def _():
                second_stage()
                wait_rows(slot)
                wait_rows((slot + 1) % XS_RING)


def _grouped_mlp(plan, h2, inv, w_in, w_out, layer):
    item_tile, item_seq, item_lo, item_hi, seq_expert, n_items, n_seq = plan
    d, f2 = w_in.shape[-2:]
    f = f2 // 2
    tm = GMM_TILE
    rows = tm * SUBLANES
    tok = inv.reshape(-1, tm // LANES, LANES)

    def item(ahead):
        return lambda w, it, iq, lo, hi, qe, n, nq: (it[jnp.minimum(w + ahead, n[0] - 1)], 0, 0)

    def second(w, n):
        return jnp.minimum(jnp.maximum(w - 1, 0), n[0] - 1)

    tok_spec = lambda ahead: pl.BlockSpec((1, tm // LANES, LANES), item(ahead), memory_space=pltpu.SMEM)
    return pl.pallas_call(
        functools.partial(_gmm_kernel, layer=layer),
        grid_spec=pltpu.PrefetchScalarGridSpec(
            num_scalar_prefetch=7,
            grid=(item_tile.shape[0] + 1,),
            in_specs=[tok_spec(0), tok_spec(1), tok_spec(2)] + [pl.BlockSpec(memory_space=pl.ANY)] * 3,
            out_specs=pl.BlockSpec((rows, LANES), lambda w, it, iq, lo, hi, qe, n, nq: (it[second(w, n)], 0)),
            scratch_shapes=[pltpu.VMEM((d, f2), BF16), pltpu.VMEM((f, d), BF16), pltpu.VMEM((2, tm, f), BF16),
                            pltpu.VMEM((tm, d), F32), pltpu.VMEM((2, d, f2), F32), pltpu.VMEM((2, f, d), F32)]
            + [pltpu.VMEM((rows, LANES), F32)] * XS_RING
            + [pltpu.SemaphoreType.DMA((2,)), pltpu.SemaphoreType.DMA((2,)), pltpu.SemaphoreType.DMA((XS_RING,))],
        ),
        out_shape=jax.ShapeDtypeStruct((inv.shape[0] * SUBLANES, LANES), F32),
        compiler_params=pltpu.CompilerParams(
            dimension_semantics=("arbitrary",), vmem_limit_bytes=VMEM_LIMIT),
    )(item_tile, item_seq, item_lo, item_hi, seq_expert, n_items, n_seq, tok, tok, tok, h2, w_in, w_out)


def _combine_kernel(pos0_ref, pos1_ref, pos2_ref, x1_ref, wt_ref, mod_ref, fg_ref, ys_ref, o_ref, *scratch):
    bufs, sems = scratch[:COMBINE_RING], scratch[COMBINE_RING]
    i = pl.program_id(0)
    n = pl.num_programs(0)
    tp = x1_ref.shape[0]

    def issue(idx_ref, slot):
        def one(k, row, lane, token):
            pltpu.make_async_copy(ys_ref.at[_tile_rows(idx_ref[0, row, lane])], bufs[slot].at[k, _tile_rows(token)],
                                  sems.at[slot]).start(priority=k)
        _for_each_token(tp, one)

    def compute(slot):
        wt = wt_ref[...]
        gate2 = mod_ref[0, 5:6]
        chunks = []
        for s in range(SUBLANES):
            rows = pl.ds(s, tp, stride=SUBLANES)
            cols = slice(s * LANES, (s + 1) * LANES)
            y = wt[:, 0:1] * bufs[slot][0, rows, :] + wt[:, 1:2] * bufs[slot][1, rows, :]
            chunks.append(x1_ref[:, cols] + gate2[:, cols] * y)
        x = jnp.concatenate(chunks, axis=-1)
        o_ref[...] = x * lax.rsqrt(jnp.mean(x * x, axis=-1, keepdims=True) + NORM_EPS) * fg_ref[...]

    @pl.when(i == 0)
    def _():
        issue(pos0_ref, 0)

        @pl.when(n > 1)
        def _():
            issue(pos1_ref, 1)

    for slot in range(COMBINE_RING):
        @pl.when(lax.rem(i, COMBINE_RING) == slot)
        def _():
            for k in range(MOE_TOP_E):
                pltpu.make_async_copy(ys_ref.at[pl.ds(0, tp * SUBLANES)], bufs[slot].at[k], sems.at[slot]).wait()

            @pl.when(i + 2 < n)
            def _():
                issue(pos2_ref, (slot + 2) % COMBINE_RING)
                compute(slot)

            @pl.when(i + 2 >= n)
            def _():
                compute(slot)


def _combine(pos, x1, wts_t, mod, final_g, ys, seq_len):
    nt, idx_rows, _ = pos.shape
    t, d = x1.shape
    tp = t // nt
    per_seq = seq_len // tp
    idx_spec = lambda ahead: pl.BlockSpec((1, idx_rows, LANES), lambda i: (jnp.minimum(i + ahead, nt - 1), 0, 0),
                                          memory_space=pltpu.SMEM)
    return pl.pallas_call(
        _combine_kernel,
        grid=(nt,),
        in_specs=[
            idx_spec(0), idx_spec(1), idx_spec(2),
            pl.BlockSpec((tp, d), lambda i: (i, 0)),
            pl.BlockSpec((tp, MOE_TOP_E), lambda i: (i, 0)),
            pl.BlockSpec((1, 6, d), lambda i: (i // per_seq, 0, 0)),
            pl.BlockSpec((1, d), lambda i: (0, 0)),
            pl.BlockSpec(memory_space=pl.ANY),
        ],
        out_specs=pl.BlockSpec((tp, d), lambda i: (i, 0)),
        out_shape=jax.ShapeDtypeStruct((t, d), F32),
        scratch_shapes=[pltpu.VMEM((MOE_TOP_E, tp * SUBLANES, LANES), F32)] * COMBINE_RING
        + [pltpu.SemaphoreType.DMA((COMBINE_RING,))],
        compiler_params=pltpu.CompilerParams(
            dimension_semantics=("arbitrary",), vmem_limit_bytes=VMEM_LIMIT),
    )(pos, pos, pos, x1, wts_t, mod, final_g, ys)


def _router_weights(w_group, b_group, w_expert, b_expert):
    g, e = w_group.shape[-1], w_expert.shape[-1]
    pad = ROUTER_ROWS - g - e
    w = jnp.swapaxes(jnp.concatenate([w_group, w_expert], axis=2), 1, 2)
    hi, lo = _split_bf16(jnp.pad(w, ((0, 0), (0, pad), (0, 0))))
    bias = jnp.pad(jnp.concatenate([b_group, b_expert], axis=1), ((0, 0), (0, pad)))[:, :, None]
    return hi, lo, bias


def _tile_plan(hist, n_rows):
    n_e = hist.shape[0]
    tm = GMM_TILE
    counts = jnp.sum(hist, axis=1).astype(jnp.int32)
    ends = jnp.cumsum(counts)
    starts = ends - counts
    first_tile = starts // tm
    items_per = jnp.where(counts > 0, (ends - 1) // tm - first_tile + 1, 0)
    item_ends = jnp.cumsum(items_per)
    n_items = item_ends[-1:]
    max_items = n_rows // tm + n_e - 1
    w = jnp.minimum(jnp.arange(max_items, dtype=jnp.int32), n_items[0] - 1)
    item_expert = jnp.sum(w[:, None] >= item_ends[None, :], axis=1).astype(jnp.int32)
    own = item_expert[:, None] == jnp.arange(n_e, dtype=jnp.int32)[None, :]
    pick = lambda per_expert: jnp.sum(jnp.where(own, per_expert[None, :], 0), axis=1)
    item_tile = pick(first_tile) + w - pick(item_ends - items_per)
    item_lo = jnp.maximum(pick(starts) - item_tile * tm, 0)
    item_hi = jnp.minimum(pick(ends) - item_tile * tm, tm)
    offsets = starts.astype(F32).reshape(n_e, 1)
    used_ends = jnp.cumsum((counts > 0).astype(jnp.int32))
    n_seq = used_ends[-1:]
    q = jnp.minimum(jnp.arange(n_e, dtype=jnp.int32), n_seq[0] - 1)
    seq_expert = jnp.sum(q[:, None] >= used_ends[None, :], axis=1).astype(jnp.int32)
    item_seq = pick(used_ends - 1)
    return offsets, (item_tile.astype(jnp.int32), item_seq.astype(jnp.int32), item_lo.astype(jnp.int32),
                     item_hi.astype(jnp.int32), seq_expert, n_items.astype(jnp.int32), n_seq.astype(jnp.int32))


def kernel(x, c, norm_gain, w_mod, b_mod, pool_w, pool_b, pool_scale, gla_w_in, gla_w_gate, gla_b_gate, gla_norm_g, gla_w_out, moe_w_group, moe_b_group, moe_w_expert, moe_b_expert, moe_w_in, moe_w_out, final_norm_g):
    b, s, d = x.shape
    depth = w_mod.shape[0]
    t = b * s
    n_groups = moe_w_group.shape[-1]
    n_experts = moe_w_expert.shape[-1]
    n_rows = MOE_TOP_E * t
    mod_all = _modulation(c, w_mod, b_mod).reshape(depth, b, 6, d)
    fg = final_norm_g.reshape(1, d)
    router = _router_weights(moe_w_group, moe_b_group, moe_w_expert, moe_b_expert)
    moe = None
    for i in range(depth):
        mod = mod_all[i]
        j = i // 2
        if i % 2 == 0:
            x1, h2, ids, wts, hist = _pool_layer(x, moe, mod, norm_gain[i], pool_w[j], pool_b[j], pool_scale[j],
                                                 router, i, n_groups, n_experts)
        else:
            x1, h2, ids, wts, hist = _gla_layer(x, moe, mod, norm_gain[i], gla_w_in[j], gla_w_gate[j], gla_b_gate[j],
                                                gla_norm_g[j], gla_w_out[j], router, i, n_groups, n_experts)
        offsets, plan = _tile_plan(hist, n_rows)
        pos = _positions(ids, offsets)
        inv = _inverse_positions(pos, n_rows)
        ys = _grouped_mlp(plan, h2, inv, moe_w_in, moe_w_out, i)
        x, moe = x1, (pos, wts.T, mod, ys)
    pos, wts_t, mod, ys = moe
    x = _combine(pos, x.reshape(t, d), wts_t, mod, fg, ys, s).reshape(b, s, d)
    return x
```
